```python
import math
import jax, jax.numpy as jnp
from jax import lax
import numpy as np

D_MODEL = 2048
BATCH = 4
SEQ = 8192
DEPTH = 2

HEAD_DIM = 128
Q_BLOCK = 128
NORM_EPS = 1e-6
FOX_HEADS = 8
FOX_W = FOX_HEADS * HEAD_DIM
DIFF_HEADS = 4
DIFF_V_DIM = 2 * HEAD_DIM
DIFF_QK_W = DIFF_HEADS * 2 * HEAD_DIM
DIFF_V_W = DIFF_HEADS * DIFF_V_DIM
MLA_HEADS = 8
MLA_Q_RANK = 512
MLA_KV_RANK = 512
MLA_NOPE = 128
MLA_ROPE = 64
MLA_V = 128
ROPE_THETA = 10000.0
MOBA_HEADS = 8
MOBA_W = MOBA_HEADS * HEAD_DIM
MOBA_BLOCK = 256
MOBA_TOPK = 3
MOBA_Q_CHUNK = 16
MEM_LEN = 256
XATTN_HEADS = 4
XATTN_DIM = 128
XATTN_W = XATTN_HEADS * XATTN_DIM
FFN_DIM = 4 * D_MODEL
AB_IN = 3 * FOX_W + FOX_HEADS + 2 * DIFF_QK_W + DIFF_V_W
AB_OUT = FOX_W + DIFF_V_W
CD_IN = MLA_Q_RANK + MLA_KV_RANK + MLA_ROPE + 3 * MOBA_W
CD_OUT = MLA_HEADS * MLA_V + MOBA_W
N_EVEN = (DEPTH + 1) // 2
N_ODD = DEPTH // 2

kernel_name = 'hybrid_fox_diff_mla_moba_trunk'


def rmsnorm(x, g):
    xf = x.astype(jnp.float32)
    y = xf * lax.rsqrt(jnp.mean(xf * xf, axis=-1, keepdims=True) + NORM_EPS)
    return (y * g.astype(jnp.float32)).astype(x.dtype)


def split_heads(t, n_heads):
    b, s, _ = t.shape
    return t.reshape(b, s, n_heads, -1).transpose(0, 2, 1, 3)


def merge_heads(t):
    b, h, s, d = t.shape
    return t.transpose(0, 2, 1, 3).reshape(b, s, h * d)


def alibi_slopes(n):
    return jnp.asarray([2.0 ** (-8.0 * (i + 1) / n) for i in range(n)], dtype=jnp.float32)


def alibi_bias(slopes):
    def fn(q0, q1):
        dist = jnp.abs(jnp.arange(q0, q1)[:, None] - jnp.arange(q1)[None, :]).astype(jnp.float32)
        return -slopes[None, :, None, None] * dist[None, None]
    return fn


def blocked_causal_attention(q, k, v, scale, bias_fn=None):
    s = q.shape[2]
    neg = jnp.finfo(jnp.float32).min
    outs = []
    for i in range(s // Q_BLOCK):
        q0, q1 = i * Q_BLOCK, (i + 1) * Q_BLOCK
        logits = jnp.einsum('bhqd,bhkd->bhqk', q[:, :, q0:q1], k[:, :, :q1]).astype(jnp.float32) * scale
        if bias_fn is not None:
            logits = logits + bias_fn(q0, q1)
        causal = jnp.arange(q1)[None, :] <= jnp.arange(q0, q1)[:, None]
        p = jax.nn.softmax(jnp.where(causal, logits, neg), axis=-1).astype(v.dtype)
        outs.append(jnp.einsum('bhqk,bhkd->bhqd', p, v[:, :, :q1]))
    return jnp.concatenate(outs, axis=2)


def rope_tables(s):
    inv = ROPE_THETA ** (-jnp.arange(0, MLA_ROPE, 2, dtype=jnp.float32) / MLA_ROPE)
    ang = jnp.arange(s, dtype=jnp.float32)[:, None] * inv[None, :]
    return jnp.cos(ang), jnp.sin(ang)


def apply_rope(t, cos, sin):
    half = t.shape[-1] // 2
    t1, t2 = t[..., :half], t[..., half:]
    c, s_ = cos.astype(t.dtype), sin.astype(t.dtype)
    return jnp.concatenate([t1 * c - t2 * s_, t2 * c + t1 * s_], axis=-1)


def moba_attention(q, k, v, slopes):
    b, h, s, dh = q.shape
    nblk = -(-s // MOBA_BLOCK)
    pad = nblk * MOBA_BLOCK - s
    kp = jnp.pad(k, ((0, 0), (0, 0), (0, pad), (0, 0))).reshape(b, h, nblk, MOBA_BLOCK, dh)
    vp = jnp.pad(v, ((0, 0), (0, 0), (0, pad), (0, 0))).reshape(b, h, nblk, MOBA_BLOCK, dh)
    kmean = jnp.mean(kp.astype(jnp.float32), axis=3)
    gate = jnp.einsum('bhsd,bhnd->bhsn', q.astype(jnp.float32), kmean)
    tpos = jnp.arange(s)
    own = tpos // MOBA_BLOCK
    past = jnp.arange(nblk)[None, :] < own[:, None]
    gate = jnp.where(past[None, None], gate, -jnp.inf)
    k_sel = min(MOBA_TOPK, nblk)
    top_val, top_idx = lax.top_k(gate, k_sel)
    own_b = jnp.broadcast_to(own[None, None, :, None], (b, h, s, 1)).astype(top_idx.dtype)
    idx = jnp.concatenate([top_idx, own_b], axis=-1)
    valid = jnp.concatenate([jnp.isfinite(top_val), jnp.ones((b, h, s, 1), dtype=bool)], axis=-1)
    nc = s // MOBA_Q_CHUNK

    def chunk_view(t):
        return jnp.moveaxis(t.reshape(b, h, nc, MOBA_Q_CHUNK, *t.shape[3:]), 2, 0)

    bi = jnp.arange(b)[:, None, None, None]
    hi = jnp.arange(h)[None, :, None, None]
    slopes_b = slopes[None, :, None, None, None]
    scale = dh ** -0.5
    neg = jnp.finfo(jnp.float32).min

    def one_chunk(args):
        qc, ic, vc, pc = args
        kg = kp[bi, hi, ic]
        vg = vp[bi, hi, ic]
        kpos = ic[..., None] * MOBA_BLOCK + jnp.arange(MOBA_BLOCK)
        qpos = pc[None, None, :, None, None]
        dist = jnp.abs(qpos - kpos).astype(jnp.float32)
        logits = jnp.einsum('bhcd,bhcjpd->bhcjp', qc, kg).astype(jnp.float32) * scale - slopes_b * dist
        mask = vc[..., None] & (kpos <= qpos)
        logits = jnp.where(mask, logits, neg)
        shp = logits.shape
        p = jax.nn.softmax(logits.reshape(shp[0], shp[1], shp[2], -1), axis=-1).reshape(shp).astype(vg.dtype)
        return jnp.einsum('bhcjp,bhcjpd->bhcd', p, vg)

    out = lax.map(one_chunk, (chunk_view(q), chunk_view(idx), chunk_view(valid), tpos.reshape(nc, MOBA_Q_CHUNK)))
    return jnp.moveaxis(out, 0, 2).reshape(b, h, s, dh)


def mixer_fox_diff(u, w_in, w_out, b_f, lam, subln, layer_idx):
    b, s, _ = u.shape
    sizes = [FOX_W, FOX_W, FOX_W, FOX_HEADS, DIFF_QK_W, DIFF_QK_W, DIFF_V_W]
    fq, fk, fv, fg, dq, dk, dv = jnp.split(u @ w_in, np.cumsum(sizes)[:-1].tolist(), axis=-1)
    logf = jax.nn.log_sigmoid(fg.astype(jnp.float32) + b_f.astype(jnp.float32))
    cum = jnp.cumsum(logf, axis=1).transpose(0, 2, 1)
    fox_bias = lambda q0, q1: cum[:, :, q0:q1, None] - cum[:, :, None, :q1]
    a = blocked_causal_attention(split_heads(fq, FOX_HEADS), split_heads(fk, FOX_HEADS),
                                 split_heads(fv, FOX_HEADS), HEAD_DIM ** -0.5, fox_bias)
    qd = dq.reshape(b, s, DIFF_HEADS, 2, HEAD_DIM).transpose(0, 3, 2, 1, 4).reshape(b, 2 * DIFF_HEADS, s, HEAD_DIM)
    kd = dk.reshape(b, s, DIFF_HEADS, 2, HEAD_DIM).transpose(0, 3, 2, 1, 4).reshape(b, 2 * DIFF_HEADS, s, HEAD_DIM)
    vd = split_heads(dv, DIFF_HEADS)
    od = blocked_causal_attention(qd, kd, jnp.concatenate([vd, vd], axis=1), HEAD_DIM ** -0.5,
                                  alibi_bias(jnp.tile(alibi_slopes(DIFF_HEADS), 2)))
    od = od.reshape(b, 2, DIFF_HEADS, s, DIFF_V_DIM)
    lam_init = 0.8 - 0.6 * math.exp(-0.3 * layer_idx)
    lf = lam.astype(jnp.float32)
    lam_full = jnp.exp(jnp.sum(lf[0] * lf[1])) - jnp.exp(jnp.sum(lf[2] * lf[3])) + lam_init
    d = od[:, 0] - lam_full.astype(od.dtype) * od[:, 1]
    d = rmsnorm(d, subln) * (1.0 - lam_init)
    return jnp.concatenate([merge_heads(a), merge_heads(d)], axis=-1) @ w_out


def mixer_mla_moba(u, w_in, w_out, q_norm, kv_norm, w_uq, w_ukv, cos, sin):
    b, s, _ = u.shape
    sizes = [MLA_Q_RANK, MLA_KV_RANK, MLA_ROPE, MOBA_W, MOBA_W, MOBA_W]
    cq, ckv, kpe, mq, mk, mv = jnp.split(u @ w_in, np.cumsum(sizes)[:-1].tolist(), axis=-1)
    q = (rmsnorm(cq, q_norm) @ w_uq).reshape(b, s, MLA_HEADS, MLA_NOPE + MLA_ROPE).transpose(0, 2, 1, 3)
    kv = (rmsnorm(ckv, kv_norm) @ w_ukv).reshape(b, s, MLA_HEADS, MLA_NOPE + MLA_V).transpose(0, 2, 1, 3)
    q_pe = apply_rope(q[..., MLA_NOPE:], cos, sin)
    k_pe = apply_rope(kpe[:, None], cos, sin)
    qf = jnp.concatenate([q[..., :MLA_NOPE], q_pe], axis=-1)
    kf = jnp.concatenate([kv[..., :MLA_NOPE], jnp.broadcast_to(k_pe, (b, MLA_HEADS, s, MLA_ROPE))], axis=-1)
    c = blocked_causal_attention(qf, kf, kv[..., MLA_NOPE:], (MLA_NOPE + MLA_ROPE) ** -0.5)
    dout = moba_attention(split_heads(mq, MOBA_HEADS), split_heads(mk, MOBA_HEADS),
                          split_heads(mv, MOBA_HEADS), alibi_slopes(MOBA_HEADS))
    return jnp.concatenate([merge_heads(c), merge_heads(dout)], axis=-1) @ w_out


def memory_attention(u, wq, wo, mem_k, mem_v):
    q = split_heads(u @ wq, XATTN_HEADS)
    logits = jnp.einsum('bhsd,bhmd->bhsm', q, mem_k).astype(jnp.float32) * XATTN_DIM ** -0.5
    p = jax.nn.softmax(logits, axis=-1).astype(mem_v.dtype)
    return merge_heads(jnp.einsum('bhsm,bhmd->bhsd', p, mem_v)) @ wo


def squared_relu_mlp(u, w1, w2):
    return jnp.square(jax.nn.relu(u @ w1)) @ w2


def setup_inputs(seed: int = 0) -> dict:
    key = jax.random.key(seed)
    ks = jax.random.split(key, 20)
    nrm = lambda k, shp, sc: jax.random.normal(k, shp, jnp.float32) * sc
    gain = lambda k, shp: 1.0 + 0.02 * jax.random.normal(k, shp, jnp.float32)
    return {
        'x': nrm(ks[0], (BATCH, SEQ, D_MODEL), 1.0),
        'mem': nrm(ks[1], (BATCH, MEM_LEN, D_MODEL), 1.0),
        'mem_norm': gain(ks[2], (D_MODEL,)),
        'mem_wkv': nrm(ks[3], (D_MODEL, 2 * XATTN_W), D_MODEL ** -0.5),
        'norms': gain(ks[4], (DEPTH, 6, D_MODEL)),
        'xattn_wq': nrm(ks[5], (DEPTH, D_MODEL, XATTN_W), D_MODEL ** -0.5),
        'xattn_wo': nrm(ks[6], (DEPTH, XATTN_W, D_MODEL), XATTN_W ** -0.5),
        'ffn_w1': nrm(ks[7], (DEPTH, D_MODEL, FFN_DIM), D_MODEL ** -0.5),
        'ffn_w2': nrm(ks[8], (DEPTH, FFN_DIM, D_MODEL), FFN_DIM ** -0.5),
        'ab_w_in': nrm(ks[9], (N_EVEN, D_MODEL, AB_IN), D_MODEL ** -0.5),
        'ab_w_out': nrm(ks[10], (N_EVEN, AB_OUT, D_MODEL), AB_OUT ** -0.5),
        'fox_b_f': 2.0 + 0.1 * jax.random.normal(ks[11], (N_EVEN, FOX_HEADS), jnp.float32),
        'diff_lambda': nrm(ks[12], (N_EVEN, 4, HEAD_DIM), 0.1),
        'diff_subln': gain(ks[13], (N_EVEN, DIFF_V_DIM)),
        'cd_w_in': nrm(ks[14], (N_ODD, D_MODEL, CD_IN), D_MODEL ** -0.5),
        'cd_w_out': nrm(ks[15], (N_ODD, CD_OUT, D_MODEL), CD_OUT ** -0.5),
        'mla_q_norm': gain(ks[16], (N_ODD, MLA_Q_RANK)),
        'mla_kv_norm': gain(ks[17], (N_ODD, MLA_KV_RANK)),
        'mla_w_uq': nrm(ks[18], (N_ODD, MLA_Q_RANK, MLA_HEADS * (MLA_NOPE + MLA_ROPE)), MLA_Q_RANK ** -0.5),
        'mla_w_ukv': nrm(ks[19], (N_ODD, MLA_KV_RANK, MLA_HEADS * (MLA_NOPE + MLA_V)), MLA_KV_RANK ** -0.5),
    }


def reference(x, mem, mem_norm, mem_wkv, norms, xattn_wq, xattn_wo, ffn_w1, ffn_w2,
              ab_w_in, ab_w_out, fox_b_f, diff_lambda, diff_subln,
              cd_w_in, cd_w_out, mla_q_norm, mla_kv_norm, mla_w_uq, mla_w_ukv):
    s = x.shape[1]
    cos, sin = rope_tables(s)
    mkv = rmsnorm(mem, mem_norm) @ mem_wkv
    mem_k = split_heads(mkv[..., :XATTN_W], XATTN_HEADS)
    mem_v = split_heads(mkv[..., XATTN_W:], XATTN_HEADS)
    h = x
    for i in range(DEPTH):
        n = norms[i]
        j = i // 2
        u = rmsnorm(h, n[0])
        if i % 2 == 0:
            u = mixer_fox_diff(u, ab_w_in[j], ab_w_out[j], fox_b_f[j], diff_lambda[j], diff_subln[j], i)
        else:
            u = mixer_mla_moba(u, cd_w_in[j], cd_w_out[j], mla_q_norm[j], mla_kv_norm[j],
                               mla_w_uq[j], mla_w_ukv[j], cos, sin)
        h = h + rmsnorm(u, n[1])
        h = h + rmsnorm(memory_attention(rmsnorm(h, n[2]), xattn_wq[i], xattn_wo[i], mem_k, mem_v), n[3])
        h = h + rmsnorm(squared_relu_mlp(rmsnorm(h, n[4]), ffn_w1[i], ffn_w2[i]), n[5])
    return h
```

```python
import functools
import math

import jax
import jax.numpy as jnp
from jax import lax
from jax.experimental import pallas as pl
from jax.experimental.pallas import tpu as pltpu

F32 = jnp.float32
BF16 = jnp.bfloat16

NORM_EPS = 1e-6
HEAD_DIM = 128
FOX_HEADS = 8
DIFF_HEADS = 4
MLA_HEADS = 8
MLA_NOPE = 128
MLA_ROPE = 64
MLA_V = 128
MLA_RANK = 512
ROPE_THETA = 10000.0
MOBA_HEADS = 8
MOBA_BLOCK = 256
MOBA_TOPK = 3
XATTN_HEADS = 4
LANES = 128
MASKED = -1e30
VMEM_LIMIT = 48 * 1024 * 1024

_NT = (((1,), (1,)), ((), ()))


def _params(sem):
    return pltpu.CompilerParams(dimension_semantics=sem, vmem_limit_bytes=VMEM_LIMIT)


def _rms(x, g):
    ms = jnp.mean(x * x, axis=-1, keepdims=True)
    return x * lax.rsqrt(ms + NORM_EPS) * g


def _inproj_kernel(x_ref, g_ref, w_ref, aux_w_ref, *rest, aux_mode):
    if aux_mode == "rope":
        cos_ref, sin_ref, o_ref, aux_ref, xn_ref = rest
    else:
        o_ref, aux_ref, xn_ref = rest

    @pl.when(pl.program_id(1) == 0)
    def _():
        xn = _rms(x_ref[...], g_ref[...]).astype(BF16)
        xn_ref[...] = xn
        if aux_mode == "gate_t":
            r = lax.dot_general(aux_w_ref[...], xn, _NT, preferred_element_type=F32)
            aux_ref[...] = r[:8]
        else:
            ab = jnp.dot(xn, aux_w_ref[...], preferred_element_type=F32)
            aux_ref[...] = (ab[:, :LANES] * cos_ref[...] + ab[:, LANES:] * sin_ref[...]).astype(aux_ref.dtype)

    o_ref[...] = jnp.dot(xn_ref[...], w_ref[...], preferred_element_type=F32).astype(o_ref.dtype)


def _inproj(x, g, w, aux_w, aux_mode, tables=None, *, tm=512, tn=512):
    t, k = x.shape
    n = w.shape[1]
    tm = min(tm, t)
    grid = (t // tm, n // tn)
    in_specs = [
        pl.BlockSpec((tm, k), lambda i, j: (i, 0)),
        pl.BlockSpec((1, k), lambda i, j: (0, 0)),
        pl.BlockSpec((k, tn), lambda i, j: (0, j)),
        pl.BlockSpec(aux_w.shape, lambda i, j: (0, 0)),
    ]
    args = [x, g.reshape(1, k), w, aux_w]
    if aux_mode == "rope":
        cos_t, sin_t = tables
        nrep = cos_t.shape[0] // tm
        in_specs += [pl.BlockSpec((tm, LANES), lambda i, j: (i % nrep, 0))] * 2
        args += [cos_t, sin_t]
        aux_shape = jax.ShapeDtypeStruct((t, LANES), BF16)
        aux_spec = pl.BlockSpec((tm, LANES), lambda i, j: (i, 0))
    else:
        aux_shape = jax.ShapeDtypeStruct((8, t), F32)
        aux_spec = pl.BlockSpec((8, tm), lambda i, j: (0, i))
    return pl.pallas_call(
        functools.partial(_inproj_kernel, aux_mode=aux_mode),
        grid=grid,
        in_specs=in_specs,
        out_specs=[pl.BlockSpec((tm, tn), lambda i, j: (i, j)), aux_spec],
        out_shape=[jax.ShapeDtypeStruct((t, n), BF16), aux_shape],
        scratch_shapes=[pltpu.VMEM((tm, k), BF16)],
        compiler_params=_params(("parallel", "arbitrary")),
        name="inproj_" + aux_mode,
    )(*args)


def _norm_matmul_kernel(x_ref, g_ref, w_ref, o_ref):
    xn = _rms(x_ref[...], g_ref[...]).astype(BF16)
    o_ref[...] = jnp.dot(xn, w_ref[...], preferred_element_type=F32).astype(o_ref.dtype)


def _norm_matmul(x, g, w, *, tm=256):
    t, k = x.shape
    n = w.shape[1]
    return pl.pallas_call(
        _norm_matmul_kernel,
        grid=(t // tm,),
        in_specs=[pl.BlockSpec((tm, k), lambda i: (i, 0)),
                  pl.BlockSpec((1, k), lambda i: (0, 0)),
                  pl.BlockSpec((k, n), lambda i: (0, 0))],
        out_specs=pl.BlockSpec((tm, n), lambda i: (i, 0)),
        out_shape=jax.ShapeDtypeStruct((t, n), BF16),
        compiler_params=_params(("parallel",)),
        name="mem_kv_proj",
    )(x, g.reshape(1, k), w)


def _fox_bias_kernel(g_ref, b_ref, o_ref, carry_ref, *, tc):
    @pl.when(pl.program_id(1) == 0)
    def _():
        carry_ref[...] = jnp.zeros_like(carry_ref)

    z = g_ref[...] + b_ref[...]
    logf = jnp.minimum(z, 0.0) - jnp.log(1.0 + jnp.exp(-jnp.abs(z)))
    upper = (lax.broadcasted_iota(jnp.int32, (tc, tc), 0)
             <= lax.broadcasted_iota(jnp.int32, (tc, tc), 1)).astype(F32)
    cum = jnp.dot(logf, upper, preferred_element_type=F32, precision=lax.Precision.HIGHEST) + carry_ref[...]
    carry_ref[...] = cum[:, tc - 1:tc]
    o_ref[0] = -cum


def _fox_bias(gate_t, b_f, batch, seq, *, tc=256):
    ns = seq // tc
    return pl.pallas_call(
        functools.partial(_fox_bias_kernel, tc=tc),
        grid=(batch, ns),
        in_specs=[pl.BlockSpec((8, tc), lambda b, s: (0, b * ns + s)),
                  pl.BlockSpec((8, 1), lambda b, s: (0, 0))],
        out_specs=pl.BlockSpec((1, 8, tc), lambda b, s: (b, 0, s)),
        out_shape=jax.ShapeDtypeStruct((batch, 8, seq), F32),
        scratch_shapes=[pltpu.VMEM((8, 1), F32)],
        compiler_params=_params(("parallel", "arbitrary")),
        name="fox_gate_cumsum",
    )(gate_t, b_f.reshape(8, 1))


def _causal_stream(q, k_ref, v_ref, kb_ref, qi, t, m_ref, l_ref, acc_ref, row_keep=None):
    q0 = pl.multiple_of(qi * t, t)
    kb_d = kb_ref[0, 0, qi]
    shift = jnp.max(kb_d, axis=-1, keepdims=True)
    s = lax.dot_general(q, k_ref[0, pl.ds(q0, t), :], _NT, preferred_element_type=F32) + (kb_d - shift)
    row = lax.broadcasted_iota(jnp.int32, (t, t), 0)
    col = lax.broadcasted_iota(jnp.int32, (t, t), 1)
    s = jnp.where(col <= row, s, MASKED)
    m = jnp.max(s, axis=-1, keepdims=True)
    p = jnp.exp(s - m)
    m_ref[...] = m
    l_ref[...] = jnp.sum(p, axis=-1, keepdims=True)
    acc_ref[...] = jnp.dot(p.astype(BF16), v_ref[0, pl.ds(q0, t), :], preferred_element_type=F32)

    def body(j, carry):
        k0 = pl.multiple_of(j * t, t)
        s = lax.dot_general(q, k_ref[0, pl.ds(k0, t), :], _NT, preferred_element_type=F32)
        s = s + (kb_ref[0, 0, j] - shift)
        if row_keep is not None:
            s = jnp.where(row_keep(j), s, MASKED)
        m_old = m_ref[...]
        m_new = jnp.maximum(m_old, jnp.max(s, axis=-1, keepdims=True))
        alpha = jnp.exp(m_old - m_new)
        p = jnp.exp(s - m_new)
        l_ref[...] = alpha * l_ref[...] + jnp.sum(p, axis=-1, keepdims=True)
        acc_ref[...] = alpha * acc_ref[...] + jnp.dot(p.astype(BF16), v_ref[0, pl.ds(k0, t), :],
                                                      preferred_element_type=F32)
        m_ref[...] = m_new
        return carry

    lax.fori_loop(0, qi, body, 0)
    return acc_ref[...] / l_ref[...]


def _flash_kernel(q_ref, k_ref, v_ref, kb_ref, o_ref, m_ref, l_ref, acc_ref, *, t):
    o = _causal_stream(q_ref[0], k_ref, v_ref, kb_ref, pl.program_id(2), t, m_ref, l_ref, acc_ref)
    o_ref[0] = o.astype(o_ref.dtype)


def _flash(q_arr, k_arr, v_arr, kbias, *, heads, dk, dv, q_blk, k_blk, v_blk, t, name):
    b, s, _ = q_arr.shape
    t = min(t, s)
    nq = s // t
    kb = kbias.reshape(kbias.shape[0], heads, nq, 1, t)
    kb_b = (lambda bi: bi) if kbias.shape[0] == b else (lambda bi: 0)
    return pl.pallas_call(
        functools.partial(_flash_kernel, t=t),
        grid=(b, heads, nq),
        in_specs=[
            pl.BlockSpec((1, t, dk), lambda bi, h, i: (bi, i, q_blk + h)),
            pl.BlockSpec((1, s, dk), lambda bi, h, i: (bi, 0, k_blk + h)),
            pl.BlockSpec((1, s, dv), lambda bi, h, i: (bi, 0, v_blk + h)),
            pl.BlockSpec((1, 1, nq, 1, t), lambda bi, h, i: (kb_b(bi), h, 0, 0, 0)),
        ],
        out_specs=pl.BlockSpec((1, t, dv), lambda bi, h, i: (bi, i, h)),
        out_shape=jax.ShapeDtypeStruct((b, s, heads * dv), BF16),
        scratch_shapes=[pltpu.VMEM((t, 1), F32), pltpu.VMEM((t, 1), F32), pltpu.VMEM((t, dv), F32)],
        compiler_params=_params(("parallel", "parallel", "arbitrary")),
        name=name,
    )(q_arr, k_arr, v_arr, kb)


def _diff_kernel(q1_ref, q2_ref, k1_ref, k2_ref, v_ref, kb_ref, lam_ref, g_ref, o_ref,
                 m_ref, l_ref, acc_ref, *, t, lam_init):
    qi = pl.program_id(2)
    o1 = _causal_stream(q1_ref[0], k1_ref, v_ref, kb_ref, qi, t, m_ref, l_ref, acc_ref)
    o2 = _causal_stream(q2_ref[0], k2_ref, v_ref, kb_ref, qi, t, m_ref, l_ref, acc_ref)
    lam = lam_ref[...]
    lam_full = (jnp.exp(jnp.sum(lam[0:1] * lam[1:2], axis=-1, keepdims=True))
                - jnp.exp(jnp.sum(lam[2:3] * lam[3:4], axis=-1, keepdims=True)) + lam_init)
    d = o1 - lam_full * o2
    o_ref[0] = (_rms(d, g_ref[...]) * (1.0 - lam_init)).astype(o_ref.dtype)


def _diff_attention(proj, kbias, lam, subln, *, q_blk, k_blk, v_blk, t, lam_init):
    b, s, _ = proj.shape
    t = min(t, s)
    nq = s // t
    dv = 2 * HEAD_DIM
    kb = kbias.reshape(1, DIFF_HEADS, nq, 1, t)
    return pl.pallas_call(
        functools.partial(_diff_kernel, t=t, lam_init=lam_init),
        grid=(b, DIFF_HEADS, nq),
        in_specs=[
            pl.BlockSpec((1, t, HEAD_DIM), lambda bi, h, i: (bi, i, q_blk + 2 * h)),
            pl.BlockSpec((1, t, HEAD_DIM), lambda bi, h, i: (bi, i, q_blk + 2 * h + 1)),
            pl.BlockSpec((1, s, HEAD_DIM), lambda bi, h, i: (bi, 0, k_blk + 2 * h)),
            pl.BlockSpec((1, s, HEAD_DIM), lambda bi, h, i: (bi, 0, k_blk + 2 * h + 1)),
            pl.BlockSpec((1, s, dv), lambda bi, h, i: (bi, 0, v_blk + h)),
            pl.BlockSpec((1, 1, nq, 1, t), lambda bi, h, i: (0, h, 0, 0, 0)),
            pl.BlockSpec((4, HEAD_DIM), lambda bi, h, i: (0, 0)),
            pl.BlockSpec((1, dv), lambda bi, h, i: (0, 0)),
        ],
        out_specs=pl.BlockSpec((1, t, dv), lambda bi, h, i: (bi, i, h)),
        out_shape=jax.ShapeDtypeStruct((b, s, DIFF_HEADS * dv), BF16),
        scratch_shapes=[pltpu.VMEM((t, 1), F32), pltpu.VMEM((t, 1), F32), pltpu.VMEM((t, dv), F32)],
        compiler_params=_params(("parallel", "parallel", "arbitrary")),
        name="diff_attention",
    )(proj, proj, proj, proj, proj, kb, lam, subln.reshape(1, dv))


def _mla_up_kernel(cq_ref, ckv_ref, kpe_ref, gq_ref, gkv_ref, wq_ref, wkv_ref, cos_ref, sin_ref,
                   qf_ref, kf_ref, v_ref, *, scale):
    cqn = _rms(cq_ref[...].astype(F32), gq_ref[...]).astype(BF16)
    ckvn = _rms(ckv_ref[...].astype(F32), gkv_ref[...]).astype(BF16)
    cos = cos_ref[...]
    sin = sin_ref[...]
    kpe = kpe_ref[...]
    kv = jnp.dot(ckvn, wkv_ref[...], preferred_element_type=F32)
    nk = MLA_HEADS * MLA_NOPE
    v_ref[...] = kv[:, nk:].astype(v_ref.dtype)
    for h in range(MLA_HEADS):
        q3 = jnp.dot(cqn, wq_ref[:, h * 3 * LANES:(h + 1) * 3 * LANES], preferred_element_type=F32)
        rot = q3[:, LANES:2 * LANES] * cos + q3[:, 2 * LANES:] * sin
        qf_ref[:, 2 * h * LANES:(2 * h + 1) * LANES] = (q3[:, :LANES] * scale).astype(qf_ref.dtype)
        qf_ref[:, (2 * h + 1) * LANES:(2 * h + 2) * LANES] = (rot * scale).astype(qf_ref.dtype)
        kf_ref[:, 2 * h * LANES:(2 * h + 1) * LANES] = kv[:, h * LANES:(h + 1) * LANES].astype(kf_ref.dtype)
        kf_ref[:, (2 * h + 1) * LANES:(2 * h + 2) * LANES] = kpe


def _mla_up(proj, kpe_r, gq, gkv, wq, wkv, cos_t, sin_t, *, tm=512):
    t = proj.shape[0]
    tm = min(tm, t)
    nrep = cos_t.shape[0] // tm
    scale = float((MLA_NOPE + MLA_ROPE) ** -0.5)
    wide = 2 * LANES * MLA_HEADS
    row = lambda i: (i, 0)
    fixed = lambda i: (0, 0)
    return pl.pallas_call(
        functools.partial(_mla_up_kernel, scale=scale),
        grid=(t // tm,),
        in_specs=[
            pl.BlockSpec((tm, MLA_RANK), lambda i: (i, 0)),
            pl.BlockSpec((tm, MLA_RANK), lambda i: (i, 1)),
            pl.BlockSpec((tm, LANES), row),
            pl.BlockSpec((1, MLA_RANK), fixed),
            pl.BlockSpec((1, MLA_RANK), fixed),
            pl.BlockSpec(wq.shape, fixed),
            pl.BlockSpec(wkv.shape, fixed),
            pl.BlockSpec((tm, LANES), lambda i: (i % nrep, 0)),
            pl.BlockSpec((tm, LANES), lambda i: (i % nrep, 0)),
        ],
        out_specs=[pl.BlockSpec((tm, wide), row), pl.BlockSpec((tm, wide), row),
                   pl.BlockSpec((tm, MLA_HEADS * MLA_V), row)],
        out_shape=[jax.ShapeDtypeStruct((t, wide), BF16), jax.ShapeDtypeStruct((t, wide), BF16),
                   jax.ShapeDtypeStruct((t, MLA_HEADS * MLA_V), BF16)],
        compiler_params=_params(("parallel",)),
        name="mla_up",
    )(proj, proj, kpe_r, gq.reshape(1, -1), gkv.reshape(1, -1), wq, wkv, cos_t, sin_t)


def _kmean_kernel(k_ref, o_ref):
    j = pl.program_id(1)
    o_ref[0, pl.ds(j, 1), :] = jnp.mean(k_ref[0].astype(F32), axis=0, keepdims=True)


def _kmean(proj, *, k_blk_wide):
    b, s, _ = proj.shape
    nblk = s // MOBA_BLOCK
    w = MOBA_HEADS * HEAD_DIM
    return pl.pallas_call(
        _kmean_kernel,
        grid=(b, nblk),
        in_specs=[pl.BlockSpec((1, MOBA_BLOCK, w), lambda bi, j: (bi, j, k_blk_wide))],
        out_specs=pl.BlockSpec((1, nblk, w), lambda bi, j: (bi, 0, 0)),
        out_shape=jax.ShapeDtypeStruct((b, nblk, w), F32),
        compiler_params=_params(("parallel", "arbitrary")),
        name="moba_kmean",
    )(proj)


def _moba_kernel(q_ref, k_ref, v_ref, km_ref, kb_ref, o_ref, m_ref, l_ref, acc_ref, sel_ref, *, nblk):
    qi = pl.program_id(2)
    t = MOBA_BLOCK
    q = q_ref[0]
    km = km_ref[0]
    if nblk < LANES:
        km = jnp.concatenate([km, jnp.zeros((LANES - nblk, HEAD_DIM), F32)], axis=0)
    gate = lax.dot_general(q.astype(F32), km, _NT, preferred_element_type=F32,
                           precision=lax.Precision.HIGHEST)
    lane = lax.broadcasted_iota(jnp.int32, (t, LANES), 1)
    past = lane < qi
    neg_inf = -jnp.inf
    g0 = jnp.where(past, gate, neg_inf)
    g = g0
    kth = jnp.max(g, axis=-1, keepdims=True)
    for _ in range(MOBA_TOPK - 1):
        g = jnp.where(g >= kth, neg_inf, g)
        kth = jnp.max(g, axis=-1, keepdims=True)
    sel_ref[...] = jnp.where((g0 >= kth) & past, 1.0, 0.0)

    def row_keep(j):
        return jnp.max(jnp.where(lane == j, sel_ref[...], 0.0), axis=-1, keepdims=True) > 0.0

    o = _causal_stream(q, k_ref, v_ref, kb_ref, qi, t, m_ref, l_ref, acc_ref, row_keep=row_keep)
    o_ref[0] = o.astype(o_ref.dtype)


def _moba(proj, kmean, kbias, *, q_blk, k_blk, v_blk):
    b, s, _ = proj.shape
    t = MOBA_BLOCK
    nblk = s // t
    kb = kbias.reshape(1, MOBA_HEADS, nblk, 1, t)
    return pl.pallas_call(
        functools.partial(_moba_kernel, nblk=nblk),
        grid=(b, MOBA_HEADS, nblk),
        in_specs=[
            pl.BlockSpec((1, t, HEAD_DIM), lambda bi, h, i: (bi, i, q_blk + h)),
            pl.BlockSpec((1, s, HEAD_DIM), lambda bi, h, i: (bi, 0, k_blk + h)),
            pl.BlockSpec((1, s, HEAD_DIM), lambda bi, h, i: (bi, 0, v_blk + h)),
            pl.BlockSpec((1, nblk, HEAD_DIM), lambda bi, h, i: (bi, 0, h)),
            pl.BlockSpec((1, 1, nblk, 1, t), lambda bi, h, i: (0, h, 0, 0, 0)),
        ],
        out_specs=pl.BlockSpec((1, t, HEAD_DIM), lambda bi, h, i: (bi, i, h)),
        out_shape=jax.ShapeDtypeStruct((b, s, MOBA_HEADS * HEAD_DIM), BF16),
        scratch_shapes=[pltpu.VMEM((t, 1), F32), pltpu.VMEM((t, 1), F32), pltpu.VMEM((t, HEAD_DIM), F32),
                        pltpu.VMEM((t, LANES), F32)],
        compiler_params=_params(("parallel", "parallel", "arbitrary")),
        name="moba_attention",
    )(proj, proj, proj, kmean, kb)


def _outproj_kernel(a_ref, d_ref, wa_ref, wd_ref, g_ref, h_ref, o_ref):
    y = jnp.dot(a_ref[...], wa_ref[...], preferred_element_type=F32)
    y = y + jnp.dot(d_ref[...], wd_ref[...], preferred_element_type=F32)
    o_ref[...] = h_ref[...] + _rms(y, g_ref[...])


def _outproj(a, d, wa, wd, g, h, *, tm=512):
    t, dm = h.shape
    tm = min(tm, t)
    ka, kd = a.shape[1], d.shape[1]
    return pl.pallas_call(
        _outproj_kernel,
        grid=(t // tm,),
        in_specs=[pl.BlockSpec((tm, ka), lambda i: (i, 0)),
                  pl.BlockSpec((tm, kd), lambda i: (i, 0)),
                  pl.BlockSpec((ka, dm), lambda i: (0, 0)),
                  pl.BlockSpec((kd, dm), lambda i: (0, 0)),
                  pl.BlockSpec((1, dm), lambda i: (0, 0)),
                  pl.BlockSpec((tm, dm), lambda i: (i, 0))],
        out_specs=pl.BlockSpec((tm, dm), lambda i: (i, 0)),
        out_shape=jax.ShapeDtypeStruct((t, dm), F32),
        compiler_params=_params(("parallel",)),
        name="mixer_outproj",
    )(a, d, wa, wd, g.reshape(1, dm), h)


def _xattn_kernel(h_ref, g2_ref, wq_ref, mkv_ref, wo_ref, g3_ref, o_ref, *, scale):
    h = h_ref[0]
    xn = _rms(h, g2_ref[...]).astype(BF16)
    q = (jnp.dot(xn, wq_ref[...], preferred_element_type=F32) * scale).astype(BF16)
    w = XATTN_HEADS * HEAD_DIM
    outs = []
    for hd in range(XATTN_HEADS):
        mk = mkv_ref[0, :, hd * HEAD_DIM:(hd + 1) * HEAD_DIM]
        mv = mkv_ref[0, :, w + hd * HEAD_DIM:w + (hd + 1) * HEAD_DIM]
        s = lax.dot_general(q[:, hd * HEAD_DIM:(hd + 1) * HEAD_DIM], mk, _NT, preferred_element_type=F32)
        p = jnp.exp(s - jnp.max(s, axis=-1, keepdims=True))
        p = p / jnp.sum(p, axis=-1, keepdims=True)
        outs.append(jnp.dot(p.astype(BF16), mv, preferred_element_type=F32).astype(BF16))
    y = jnp.dot(jnp.concatenate(outs, axis=-1), wo_ref[...], preferred_element_type=F32)
    o_ref[0] = h + _rms(y, g3_ref[...])


def _xattn(h, g2, wq, mkv, wo, g3, *, tm=512):
    b, s, dm = h.shape
    tm = min(tm, s)
    fixed = lambda bi, i: (0, 0)
    return pl.pallas_call(
        functools.partial(_xattn_kernel, scale=float(HEAD_DIM ** -0.5)),
        grid=(b, s // tm),
        in_specs=[pl.BlockSpec((1, tm, dm), lambda bi, i: (bi, i, 0)),
                  pl.BlockSpec((1, dm), fixed),
                  pl.BlockSpec(wq.shape, fixed),
                  pl.BlockSpec((1,) + mkv.shape[1:], lambda bi, i: (bi, 0, 0)),
                  pl.BlockSpec(wo.shape, fixed),
                  pl.BlockSpec((1, dm), fixed)],
        out_specs=pl.BlockSpec((1, tm, dm), lambda bi, i: (bi, i, 0)),
        out_shape=jax.ShapeDtypeStruct((b, s, dm), F32),
        compiler_params=_params(("parallel", "parallel")),
        name="memory_xattn",
    )(h, g2.reshape(1, dm), wq, mkv, wo, g3.reshape(1, dm))


def _ffn_kernel(h_ref, g4_ref, w1_ref, w2_ref, g5_ref, o_ref, xn_ref, acc_ref):
    f = pl.program_id(1)

    @pl.when(f == 0)
    def _():
        xn_ref[...] = _rms(h_ref[...], g4_ref[...]).astype(BF16)
        acc_ref[...] = jnp.zeros_like(acc_ref)

    u = jnp.maximum(jnp.dot(xn_ref[...], w1_ref[...], preferred_element_type=F32), 0.0)
    acc_ref[...] += jnp.dot((u * u).astype(BF16), w2_ref[...], preferred_element_type=F32)

    @pl.when(f == pl.num_programs(1) - 1)
    def _():
        o_ref[...] = h_ref[...] + _rms(acc_ref[...], g5_ref[...])


def _ffn(h, g4, w1, w2, g5, *, tm=512, tf=512):
    t, dm = h.shape
    tm = min(tm, t)
    f = w1.shape[1]
    return pl.pallas_call(
        _ffn_kernel,
        grid=(t // tm, f // tf),
        in_specs=[pl.BlockSpec((tm, dm), lambda i, j: (i, 0)),
                  pl.BlockSpec((1, dm), lambda i, j: (0, 0)),
                  pl.BlockSpec((dm, tf), lambda i, j: (0, j)),
                  pl.BlockSpec((tf, dm), lambda i, j: (j, 0)),
                  pl.BlockSpec((1, dm), lambda i, j: (0, 0))],
        out_specs=pl.BlockSpec((tm, dm), lambda i, j: (i, 0)),
        out_shape=jax.ShapeDtypeStruct((t, dm), F32),
        scratch_shapes=[pltpu.VMEM((tm, dm), BF16), pltpu.VMEM((tm, dm), F32)],
        compiler_params=_params(("parallel", "arbitrary")),
        name="relu2_mlp",
    )(h, g4.reshape(1, dm), w1, w2, g5.reshape(1, dm))


def _alibi_key_bias(n_heads, seq):
    slopes = jnp.asarray([2.0 ** (-8.0 * (i + 1) / n_heads) for i in range(n_heads)], dtype=F32)
    return slopes[:, None] * jnp.arange(seq, dtype=F32)[None, :]


def _rope_tables(seq):
    half = MLA_ROPE // 2
    inv = ROPE_THETA ** (-jnp.arange(0, MLA_ROPE, 2, dtype=F32) / MLA_ROPE)
    ang = jnp.arange(seq, dtype=F32)[:, None] * inv[None, :]
    cos, sin = jnp.cos(ang), jnp.sin(ang)
    zero = jnp.zeros((seq, LANES - 2 * half), F32)
    return jnp.concatenate([cos, cos, zero], axis=-1), jnp.concatenate([-sin, sin, zero], axis=-1)


def _rope_pair_columns(w_t1, w_t2):
    zero = jnp.zeros((w_t1.shape[0], LANES - 2 * w_t1.shape[1]), w_t1.dtype)
    return jnp.concatenate([w_t1, w_t2, zero, w_t2, w_t1, zero], axis=-1)


def kernel(x, mem, mem_norm, mem_wkv, norms, xattn_wq, xattn_wo, ffn_w1, ffn_w2, ab_w_in, ab_w_out, fox_b_f, diff_lambda, diff_subln, cd_w_in, cd_w_out, mla_q_norm, mla_kv_norm, mla_w_uq, mla_w_ukv):
    b, s, dm = x.shape
    t = b * s
    depth = norms.shape[0]
    att_scale = HEAD_DIM ** -0.5
    half = MLA_ROPE // 2
    cos_t, sin_t = _rope_tables(s)

    mkv = _norm_matmul(mem.reshape(-1, dm), mem_norm, mem_wkv.astype(BF16)).reshape(b, mem.shape[1], -1)

    h = x.reshape(t, dm)
    for i in range(depth):
        n = norms[i]
        j = i // 2
        if i % 2 == 0:
            w = ab_w_in[j]
            fw = FOX_HEADS * HEAD_DIM
            dw = DIFF_HEADS * 2 * HEAD_DIM
            o_g = 3 * fw
            o_dq = o_g + FOX_HEADS
            w_main = jnp.concatenate([w[:, :fw] * att_scale, w[:, fw:o_g],
                                      w[:, o_dq:o_dq + dw] * att_scale, w[:, o_dq + dw:]], axis=-1).astype(BF16)
            w_gate = jnp.concatenate([w[:, o_g:o_dq].T, jnp.zeros((8, dm), F32)], axis=0).astype(BF16)
            proj, gate_t = _inproj(h, n[0], w_main, w_gate, "gate_t")
            proj = proj.reshape(b, s, -1)
            fox_kb = _fox_bias(gate_t, fox_b_f[j], b, s)
            a = _flash(proj, proj, proj, fox_kb, heads=FOX_HEADS, dk=HEAD_DIM, dv=HEAD_DIM,
                       q_blk=0, k_blk=FOX_HEADS, v_blk=2 * FOX_HEADS, t=512, name="fox_attention")
            lam_init = 0.8 - 0.6 * math.exp(-0.3 * i)
            d = _diff_attention(proj, _alibi_key_bias(DIFF_HEADS, s), diff_lambda[j], diff_subln[j],
                                q_blk=3 * FOX_HEADS, k_blk=3 * FOX_HEADS + 2 * DIFF_HEADS,
                                v_blk=(3 * fw + 2 * dw) // (2 * HEAD_DIM), t=512, lam_init=lam_init)
            wo = ab_w_out[j].astype(BF16)
            h = _outproj(a.reshape(t, -1), d.reshape(t, -1), wo[:fw], wo[fw:], n[1], h)
        else:
            w = cd_w_in[j]
            r2 = 2 * MLA_RANK
            o_m = r2 + MLA_ROPE
            mw = MOBA_HEADS * HEAD_DIM
            w_main = jnp.concatenate([w[:, :r2], w[:, o_m:o_m + mw] * att_scale, w[:, o_m + mw:]],
                                     axis=-1).astype(BF16)
            w_kpe = _rope_pair_columns(w[:, r2:r2 + half], w[:, r2 + half:o_m]).astype(BF16)
            proj, kpe_r = _inproj(h, n[0], w_main, w_kpe, "rope", (cos_t, sin_t))
            wq = mla_w_uq[j].reshape(MLA_RANK, MLA_HEADS, MLA_NOPE + MLA_ROPE)
            wq = jnp.concatenate(
                [jnp.concatenate([wq[:, hd, :MLA_NOPE],
                                  _rope_pair_columns(wq[:, hd, MLA_NOPE:MLA_NOPE + half], wq[:, hd, MLA_NOPE + half:])],
                                 axis=-1) for hd in range(MLA_HEADS)], axis=-1).astype(BF16)
            wkv = mla_w_ukv[j].reshape(MLA_RANK, MLA_HEADS, MLA_NOPE + MLA_V)
            wkv = jnp.concatenate([wkv[:, :, :MLA_NOPE].reshape(MLA_RANK, -1),
                                   wkv[:, :, MLA_NOPE:].reshape(MLA_RANK, -1)], axis=-1).astype(BF16)
            qf, kf, v = _mla_up(proj, kpe_r, mla_q_norm[j], mla_kv_norm[j], wq, wkv, cos_t, sin_t)
            c = _flash(qf.reshape(b, s, -1), kf.reshape(b, s, -1), v.reshape(b, s, -1),
                       jnp.zeros((1, MLA_HEADS, s), F32), heads=MLA_HEADS, dk=2 * LANES, dv=MLA_V,
                       q_blk=0, k_blk=0, v_blk=0, t=512, name="mla_attention")
            proj = proj.reshape(b, s, -1)
            kmean = _kmean(proj, k_blk_wide=(r2 + mw) // mw)
            dout = _moba(proj, kmean, _alibi_key_bias(MOBA_HEADS, s),
                         q_blk=r2 // HEAD_DIM, k_blk=(r2 + mw) // HEAD_DIM, v_blk=(r2 + 2 * mw) // HEAD_DIM)
            wo = cd_w_out[j].astype(BF16)
            cw = MLA_HEADS * MLA_V
            h = _outproj(c.reshape(t, -1), dout.reshape(t, -1), wo[:cw], wo[cw:], n[1], h)
        h = _xattn(h.reshape(b, s, dm), n[2], xattn_wq[i].astype(BF16), mkv, xattn_wo[i].astype(BF16), n[3])
        h = _ffn(h.reshape(t, dm), n[4], ffn_w1[i].astype(BF16), ffn_w2[i].astype(BF16), n[5])
    return h.reshape(b, s, dm)
```

```python
import functools
import math

import jax
import jax.numpy as jnp
from jax import lax
from jax.experimental import pallas as pl
from jax.experimental.pallas import tpu as pltpu

F32 = jnp.float32
BF16 = jnp.bfloat16

NORM_EPS = 1e-6
HEAD_DIM = 128
FOX_HEADS = 8
DIFF_HEADS = 4
MLA_HEADS = 8
MLA_NOPE = 128
MLA_ROPE = 64
MLA_V = 128
MLA_RANK = 512
ROPE_THETA = 10000.0
MOBA_HEADS = 8
MOBA_BLOCK = 256
MOBA_TOPK = 3
XATTN_HEADS = 4
LANES = 128
MASKED = -1e30
VMEM_LIMIT = 48 * 1024 * 1024
LOG2E = math.log2(math.e)
SKIP_LOG2 = 160.0
ATT_TILE = 512
ATT_ROWS = 256

_NT = (((1,), (1,)), ((), ()))


def _params(sem):
    return pltpu.CompilerParams(dimension_semantics=sem, vmem_limit_bytes=VMEM_LIMIT)


def _rms(x, g):
    ms = jnp.mean(x * x, axis=-1, keepdims=True)
    return x * lax.rsqrt(ms + NORM_EPS) * g


def _inproj_kernel(x_ref, g_ref, w_ref, aux_w_ref, *rest, aux_mode):
    if aux_mode == "rope":
        cos_ref, sin_ref, o_ref, aux_ref, xn_ref = rest
    else:
        o_ref, aux_ref, xn_ref = rest

    @pl.when(pl.program_id(1) == 0)
    def _():
        xn = _rms(x_ref[...], g_ref[...]).astype(BF16)
        xn_ref[...] = xn
        if aux_mode == "gate_t":
            r = lax.dot_general(aux_w_ref[...], xn, _NT, preferred_element_type=F32)
            aux_ref[...] = r[:8]
        else:
            ab = jnp.dot(xn, aux_w_ref[...], preferred_element_type=F32)
            aux_ref[...] = (ab[:, :LANES] * cos_ref[...] + ab[:, LANES:] * sin_ref[...]).astype(aux_ref.dtype)

    o_ref[...] = jnp.dot(xn_ref[...], w_ref[...], preferred_element_type=F32).astype(o_ref.dtype)


def _inproj(x, g, w, aux_w, aux_mode, tables=None, *, tm=512, tn=512):
    t, k = x.shape
    n = w.shape[1]
    tm = min(tm, t)
    grid = (t // tm, n // tn)
    in_specs = [
        pl.BlockSpec((tm, k), lambda i, j: (i, 0)),
        pl.BlockSpec((1, k), lambda i, j: (0, 0)),
        pl.BlockSpec((k, tn), lambda i, j: (0, j)),
        pl.BlockSpec(aux_w.shape, lambda i, j: (0, 0)),
    ]
    args = [x, g.reshape(1, k), w, aux_w]
    if aux_mode == "rope":
        cos_t, sin_t = tables
        nrep = cos_t.shape[0] // tm
        in_specs += [pl.BlockSpec((tm, LANES), lambda i, j: (i % nrep, 0))] * 2
        args += [cos_t, sin_t]
        aux_shape = jax.ShapeDtypeStruct((t, LANES), BF16)
        aux_spec = pl.BlockSpec((tm, LANES), lambda i, j: (i, 0))
    else:
        aux_shape = jax.ShapeDtypeStruct((8, t), F32)
        aux_spec = pl.BlockSpec((8, tm), lambda i, j: (0, i))
    return pl.pallas_call(
        functools.partial(_inproj_kernel, aux_mode=aux_mode),
        grid=grid,
        in_specs=in_specs,
        out_specs=[pl.BlockSpec((tm, tn), lambda i, j: (i, j)), aux_spec],
        out_shape=[jax.ShapeDtypeStruct((t, n), BF16), aux_shape],
        scratch_shapes=[pltpu.VMEM((tm, k), BF16)],
        compiler_params=_params(("parallel", "arbitrary")),
        name="inproj_" + aux_mode,
    )(*args)


def _norm_matmul_kernel(x_ref, g_ref, w_ref, o_ref):
    xn = _rms(x_ref[...], g_ref[...]).astype(BF16)
    o_ref[...] = jnp.dot(xn, w_ref[...], preferred_element_type=F32).astype(o_ref.dtype)


def _norm_matmul(x, g, w, *, tm=256):
    t, k = x.shape
    n = w.shape[1]
    return pl.pallas_call(
        _norm_matmul_kernel,
        grid=(t // tm,),
        in_specs=[pl.BlockSpec((tm, k), lambda i: (i, 0)),
                  pl.BlockSpec((1, k), lambda i: (0, 0)),
                  pl.BlockSpec((k, n), lambda i: (0, 0))],
        out_specs=pl.BlockSpec((tm, n), lambda i: (i, 0)),
        out_shape=jax.ShapeDtypeStruct((t, n), BF16),
        compiler_params=_params(("parallel",)),
        name="mem_kv_proj",
    )(x, g.reshape(1, k), w)


def _fox_bias_kernel(g_ref, b_ref, o_ref, carry_ref, *, tc):
    @pl.when(pl.program_id(1) == 0)
    def _():
        carry_ref[...] = jnp.zeros_like(carry_ref)

    z = g_ref[...] + b_ref[...]
    logf = jnp.minimum(z, 0.0) - jnp.log(1.0 + jnp.exp(-jnp.abs(z)))
    upper = (lax.broadcasted_iota(jnp.int32, (tc, tc), 0)
             <= lax.broadcasted_iota(jnp.int32, (tc, tc), 1)).astype(F32)
    cum = jnp.dot(logf, upper, preferred_element_type=F32, precision=lax.Precision.HIGHEST) + carry_ref[...]
    carry_ref[...] = cum[:, tc - 1:tc]
    o_ref[0] = cum * (-LOG2E)


def _fox_bias(gate_t, b_f, batch, seq, *, tc=256):
    ns = seq // tc
    return pl.pallas_call(
        functools.partial(_fox_bias_kernel, tc=tc),
        grid=(batch, ns),
        in_specs=[pl.BlockSpec((8, tc), lambda b, s: (0, b * ns + s)),
                  pl.BlockSpec((8, 1), lambda b, s: (0, 0))],
        out_specs=pl.BlockSpec((1, 8, tc), lambda b, s: (b, 0, s)),
        out_shape=jax.ShapeDtypeStruct((batch, 8, seq), F32),
        scratch_shapes=[pltpu.VMEM((8, 1), F32)],
        compiler_params=_params(("parallel", "arbitrary")),
        name="fox_gate_cumsum",
    )(gate_t, b_f.reshape(8, 1))


def _key_norm2(k_ref, kn2_ref, t, nq):
    def body(j, mx):
        kk = k_ref[0, pl.ds(pl.multiple_of(j * t, t), t), :].astype(F32)
        return jnp.maximum(mx, jnp.sum(kk * kk, axis=-1, keepdims=True))
    mx = lax.fori_loop(0, nq, body, jnp.zeros((t, 1), F32))
    kn2_ref[...] = jnp.max(mx, axis=0, keepdims=True)


def _attend(q_ref, k_ref, v_ref, bias, qi, *, t, m_ref, mb_ref, l_ref, acc_ref, rowneg=None):
    rs = min(ATT_ROWS, t)
    nsub, nch = t // rs, t // LANES
    q0 = pl.multiple_of(qi * t, t)

    def rows(r):
        return slice(r * rs, (r + 1) * rs)

    if bias is not None:
        kb_ref, kbmax_ref, kb_base, kn2_ref = bias
        shift = jnp.max(kb_ref[0, 0, qi], axis=-1, keepdims=True)

        def kbias(j):
            return kb_ref[0, 0, j] - shift
    else:
        def kbias(j):
            return None

    def chunk(s, kbv, c):
        sc = s[:, c * LANES:(c + 1) * LANES]
        return sc if kbv is None else sc + kbv[:, c * LANES:(c + 1) * LANES]

    def diag_chunks(r):
        width = (r + 1) * rs
        s = lax.dot_general(q_ref[0, rows(r), :], k_ref[0, pl.ds(q0, width), :], _NT, preferred_element_type=F32)
        kbv = kbias(qi)
        negs = rowneg(qi, r, True) if rowneg is not None else None
        lo = r * rs
        out = []
        for c in range(width // LANES):
            sc = chunk(s, kbv, c)
            if (c + 1) * LANES - 1 > lo:
                row = lo + lax.broadcasted_iota(jnp.int32, (rs, LANES), 0)
                col = c * LANES + lax.broadcasted_iota(jnp.int32, (rs, LANES), 1)
                sc = jnp.where(col <= row, sc, MASKED)
            elif negs is not None and negs[c] is not None:
                sc = sc + negs[c]
            out.append(sc)
        return out

    def past_chunks(j, r, kt, kbv):
        s = lax.dot_general(q_ref[0, rows(r), :], kt, _NT, preferred_element_type=F32)
        negs = rowneg(j, r, False) if rowneg is not None else None
        out = []
        for c in range(nch):
            sc = chunk(s, kbv, c)
            out.append(sc if negs is None else sc + negs[c])
        return out

    for r in range(nsub):
        m_ref[rows(r), :] = functools.reduce(jnp.maximum, diag_chunks(r))

    if bias is not None:
        m_min = jnp.min(jnp.max(m_ref[...], axis=-1, keepdims=True), axis=0, keepdims=True)
        qf = q_ref[0].astype(F32)
        qn2 = jnp.max(jnp.sum(qf * qf, axis=-1, keepdims=True), axis=0, keepdims=True)
        thresh = jnp.max(m_min - SKIP_LOG2 - jnp.sqrt(qn2 * kn2_ref[...]) + shift)
        j_start = lax.while_loop(lambda j: jnp.logical_and(j < qi, kbmax_ref[kb_base + j] < thresh),
                                 lambda j: j + 1, jnp.int32(0))
    else:
        j_start = 0

    def pass1(j, carry):
        kt = k_ref[0, pl.ds(pl.multiple_of(j * t, t), t), :]
        kbv = kbias(j)
        for r in range(nsub):
            m_ref[rows(r), :] = functools.reduce(jnp.maximum, past_chunks(j, r, kt, kbv), m_ref[rows(r), :])
        return carry

    lax.fori_loop(j_start, qi, pass1, 0)
    mb_ref[...] = jnp.broadcast_to(jnp.max(m_ref[...], axis=-1, keepdims=True), (t, LANES))

    def probs(chunks, r):
        mb = mb_ref[rows(r), :]
        ps = [jnp.exp2(sc - mb) for sc in chunks]
        p = ps[0] if len(ps) == 1 else jnp.concatenate(ps, axis=-1)
        return functools.reduce(jnp.add, ps), p.astype(BF16)

    for r in range(nsub):
        lsum, p = probs(diag_chunks(r), r)
        l_ref[rows(r), :] = lsum
        acc_ref[rows(r), :] = jnp.dot(p, v_ref[0, pl.ds(q0, (r + 1) * rs), :], preferred_element_type=F32)

    def pass2(j, carry):
        k0 = pl.multiple_of(j * t, t)
        kt = k_ref[0, pl.ds(k0, t), :]
        vt = v_ref[0, pl.ds(k0, t), :]
        kbv = kbias(j)
        for r in range(nsub):
            lsum, p = probs(past_chunks(j, r, kt, kbv), r)
            l_ref[rows(r), :] += lsum
            acc_ref[rows(r), :] += jnp.dot(p, vt, preferred_element_type=F32)
        return carry

    lax.fori_loop(j_start, qi, pass2, 0)
    return acc_ref[...] / jnp.sum(l_ref[...], axis=-1, keepdims=True)


def _att_scratch(t, dv, streams=1):
    acc = (t, dv) if streams == 1 else (streams, t, dv)
    return [pltpu.VMEM((t, LANES), F32), pltpu.VMEM((t, LANES), F32), pltpu.VMEM((t, LANES), F32),
            pltpu.VMEM(acc, F32)] + [pltpu.VMEM((1, 1), F32)] * streams


def _tile_bias(kbias, heads, nq, t):
    kb = kbias.reshape(kbias.shape[0], heads, nq, 1, t)
    return kb, jnp.max(kb, axis=(3, 4)).reshape(-1)


def _flash_kernel(*refs, t, nq, heads, kb_batched, has_bias):
    if not has_bias:
        q_ref, k_ref, v_ref, o_ref, m_ref, mb_ref, l_ref, acc_ref, _ = refs
        bias = None
    else:
        q_ref, k_ref, v_ref, kb_ref, kbmax_ref, o_ref, m_ref, mb_ref, l_ref, acc_ref, kn2_ref = refs
        bi, h, qi = pl.program_id(0), pl.program_id(1), pl.program_id(2)

        @pl.when(qi == 0)
        def _():
            _key_norm2(k_ref, kn2_ref, t, nq)

        bias = (kb_ref, kbmax_ref, ((bi * heads if kb_batched else 0) + h) * nq, kn2_ref)
    o = _attend(q_ref, k_ref, v_ref, bias, pl.program_id(2), t=t,
                m_ref=m_ref, mb_ref=mb_ref, l_ref=l_ref, acc_ref=acc_ref)
    o_ref[0] = o.astype(o_ref.dtype)


def _flash(q_arr, k_arr, v_arr, kbias, *, heads, dk, dv, q_blk, k_blk, v_blk, name):
    b, s, _ = q_arr.shape
    t = min(ATT_TILE, s)
    nq = s // t
    in_specs = [
        pl.BlockSpec((1, t, dk), lambda bi, h, i: (bi, i, q_blk + h)),
        pl.BlockSpec((1, s, dk), lambda bi, h, i: (bi, 0, k_blk + h)),
        pl.BlockSpec((1, s, dv), lambda bi, h, i: (bi, 0, v_blk + h)),
    ]
    args = [q_arr, k_arr, v_arr]
    kb_batched = kbias is not None and kbias.shape[0] == b
    if kbias is not None:
        kb, kbmax = _tile_bias(kbias, heads, nq, t)
        kb_b = (lambda bi: bi) if kb_batched else (lambda bi: 0)
        in_specs += [pl.BlockSpec((1, 1, nq, 1, t), lambda bi, h, i: (kb_b(bi), h, 0, 0, 0)),
                     pl.BlockSpec(memory_space=pltpu.SMEM)]
        args += [kb, kbmax]
    return pl.pallas_call(
        functools.partial(_flash_kernel, t=t, nq=nq, heads=heads, kb_batched=kb_batched,
                          has_bias=kbias is not None),
        grid=(b, heads, nq),
        in_specs=in_specs,
        out_specs=pl.BlockSpec((1, t, dv), lambda bi, h, i: (bi, i, h)),
        out_shape=jax.ShapeDtypeStruct((b, s, heads * dv), BF16),
        scratch_shapes=_att_scratch(t, dv),
        compiler_params=_params(("parallel", "parallel", "arbitrary")),
        name=name,
    )(*args)


def _diff_kernel(q1_ref, q2_ref, k1_ref, k2_ref, v_ref, kb_ref, kbmax_ref, lam_ref, g_ref, o_ref,
                 m_ref, mb_ref, l_ref, acc_ref, kn2a_ref, kn2b_ref, *, t, nq, lam_init):
    h, qi = pl.program_id(1), pl.program_id(2)

    @pl.when(qi == 0)
    def _():
        _key_norm2(k1_ref, kn2a_ref, t, nq)
        _key_norm2(k2_ref, kn2b_ref, t, nq)

    scratch = dict(m_ref=m_ref, mb_ref=mb_ref, l_ref=l_ref)
    o1 = _attend(q1_ref, k1_ref, v_ref, (kb_ref, kbmax_ref, h * nq, kn2a_ref), qi, t=t,
                 acc_ref=acc_ref.at[0], **scratch)
    o2 = _attend(q2_ref, k2_ref, v_ref, (kb_ref, kbmax_ref, h * nq, kn2b_ref), qi, t=t,
                 acc_ref=acc_ref.at[1], **scratch)
    lam = lam_ref[...]
    lam_full = (jnp.exp(jnp.sum(lam[0:1] * lam[1:2], axis=-1, keepdims=True))
                - jnp.exp(jnp.sum(lam[2:3] * lam[3:4], axis=-1, keepdims=True)) + lam_init)
    d = o1 - lam_full * o2
    o_ref[0] = (_rms(d, g_ref[...]) * (1.0 - lam_init)).astype(o_ref.dtype)


def _diff_attention(proj, kbias, lam, subln, *, q_blk, k_blk, v_blk, lam_init):
    b, s, _ = proj.shape
    t = min(ATT_TILE, s)
    nq = s // t
    dv = 2 * HEAD_DIM
    kb, kbmax = _tile_bias(kbias, DIFF_HEADS, nq, t)
    return pl.pallas_call(
        functools.partial(_diff_kernel, t=t, nq=nq, lam_init=lam_init),
        grid=(b, DIFF_HEADS, nq),
        in_specs=[
            pl.BlockSpec((1, t, HEAD_DIM), lambda bi, h, i: (bi, i, q_blk + 2 * h)),
            pl.BlockSpec((1, t, HEAD_DIM), lambda bi, h, i: (bi, i, q_blk + 2 * h + 1)),
            pl.BlockSpec((1, s, HEAD_DIM), lambda bi, h, i: (bi, 0, k_blk + 2 * h)),
            pl.BlockSpec((1, s, HEAD_DIM), lambda bi, h, i: (bi, 0, k_blk + 2 * h + 1)),
            pl.BlockSpec((1, s, dv), lambda bi, h, i: (bi, 0, v_blk + h)),
            pl.BlockSpec((1, 1, nq, 1, t), lambda bi, h, i: (0, h, 0, 0, 0)),
            pl.BlockSpec(memory_space=pltpu.SMEM),
            pl.BlockSpec((4, HEAD_DIM), lambda bi, h, i: (0, 0)),
            pl.BlockSpec((1, dv), lambda bi, h, i: (0, 0)),
        ],
        out_specs=pl.BlockSpec((1, t, dv), lambda bi, h, i: (bi, i, h)),
        out_shape=jax.ShapeDtypeStruct((b, s, DIFF_HEADS * dv), BF16),
        scratch_shapes=_att_scratch(t, dv, streams=2),
        compiler_params=_params(("parallel", "parallel", "arbitrary")),
        name="diff_attention",
    )(proj, proj, proj, proj, proj, kb, kbmax, lam, subln.reshape(1, dv))


def _mla_up_kernel(cq_ref, ckv_ref, kpe_ref, gq_ref, gkv_ref, wq_ref, wkv_ref, cos_ref, sin_ref,
                   qf_ref, kf_ref, v_ref, *, scale):
    cqn = _rms(cq_ref[...].astype(F32), gq_ref[...]).astype(BF16)
    ckvn = _rms(ckv_ref[...].astype(F32), gkv_ref[...]).astype(BF16)
    cos = cos_ref[...]
    sin = sin_ref[...]
    kpe = kpe_ref[...]
    kv = jnp.dot(ckvn, wkv_ref[...], preferred_element_type=F32)
    nk = MLA_HEADS * MLA_NOPE
    v_ref[...] = kv[:, nk:].astype(v_ref.dtype)
    for h in range(MLA_HEADS):
        q3 = jnp.dot(cqn, wq_ref[:, h * 3 * LANES:(h + 1) * 3 * LANES], preferred_element_type=F32)
        rot = q3[:, LANES:2 * LANES] * cos + q3[:, 2 * LANES:] * sin
        qf_ref[:, 2 * h * LANES:(2 * h + 1) * LANES] = (q3[:, :LANES] * scale).astype(qf_ref.dtype)
        qf_ref[:, (2 * h + 1) * LANES:(2 * h + 2) * LANES] = (rot * scale).astype(qf_ref.dtype)
        kf_ref[:, 2 * h * LANES:(2 * h + 1) * LANES] = kv[:, h * LANES:(h + 1) * LANES].astype(kf_ref.dtype)
        kf_ref[:, (2 * h + 1) * LANES:(2 * h + 2) * LANES] = kpe


def _mla_up(proj, kpe_r, gq, gkv, wq, wkv, cos_t, sin_t, *, tm=512):
    t = proj.shape[0]
    tm = min(tm, t)
    nrep = cos_t.shape[0] // tm
    scale = float((MLA_NOPE + MLA_ROPE) ** -0.5 * LOG2E)
    wide = 2 * LANES * MLA_HEADS
    row = lambda i: (i, 0)
    fixed = lambda i: (0, 0)
    return pl.pallas_call(
        functools.partial(_mla_up_kernel, scale=scale),
        grid=(t // tm,),
        in_specs=[
            pl.BlockSpec((tm, MLA_RANK), lambda i: (i, 0)),
            pl.BlockSpec((tm, MLA_RANK), lambda i: (i, 1)),
            pl.BlockSpec((tm, LANES), row),
            pl.BlockSpec((1, MLA_RANK), fixed),
            pl.BlockSpec((1, MLA_RANK), fixed),
            pl.BlockSpec(wq.shape, fixed),
            pl.BlockSpec(wkv.shape, fixed),
            pl.BlockSpec((tm, LANES), lambda i: (i % nrep, 0)),
            pl.BlockSpec((tm, LANES), lambda i: (i % nrep, 0)),
        ],
        out_specs=[pl.BlockSpec((tm, wide), row), pl.BlockSpec((tm, wide), row),
                   pl.BlockSpec((tm, MLA_HEADS * MLA_V), row)],
        out_shape=[jax.ShapeDtypeStruct((t, wide), BF16), jax.ShapeDtypeStruct((t, wide), BF16),
                   jax.ShapeDtypeStruct((t, MLA_HEADS * MLA_V), BF16)],
        compiler_params=_params(("parallel",)),
        name="mla_up",
    )(proj, proj, kpe_r, gq.reshape(1, -1), gkv.reshape(1, -1), wq, wkv, cos_t, sin_t)


def _kmean_kernel(k_ref, o_ref):
    j = pl.program_id(1)
    o_ref[0, pl.ds(j, 1), :] = jnp.mean(k_ref[0].astype(F32), axis=0, keepdims=True)


def _kmean(proj, *, k_blk_wide):
    b, s, _ = proj.shape
    nblk = s // MOBA_BLOCK
    w = MOBA_HEADS * HEAD_DIM
    return pl.pallas_call(
        _kmean_kernel,
        grid=(b, nblk),
        in_specs=[pl.BlockSpec((1, MOBA_BLOCK, w), lambda bi, j: (bi, j, k_blk_wide))],
        out_specs=pl.BlockSpec((1, nblk, w), lambda bi, j: (bi, 0, 0)),
        out_shape=jax.ShapeDtypeStruct((b, nblk, w), F32),
        compiler_params=_params(("parallel", "arbitrary")),
        name="moba_kmean",
    )(proj)


def _moba_kernel(q_ref, k_ref, v_ref, km_ref, kb_ref, kbmax_ref, o_ref,
                 m_ref, mb_ref, l_ref, acc_ref, kn2_ref, sel_ref, *, t, nq, nblk):
    h, qi = pl.program_id(1), pl.program_id(2)
    bpt = t // MOBA_BLOCK
    cpb = MOBA_BLOCK // LANES
    rs = min(ATT_ROWS, t)
    assert MOBA_BLOCK % rs == 0, "a row sub-tile must not straddle MoBA blocks"

    @pl.when(qi == 0)
    def _():
        _key_norm2(k_ref, kn2_ref, t, nq)

    km = km_ref[0]
    if nblk < LANES:
        km = jnp.concatenate([km, jnp.zeros((LANES - nblk, HEAD_DIM), F32)], axis=0)
    gate = lax.dot_general(q_ref[0].astype(F32), km, _NT, preferred_element_type=F32,
                           precision=lax.Precision.HIGHEST)
    lane = lax.broadcasted_iota(jnp.int32, (t, LANES), 1)
    row = lax.broadcasted_iota(jnp.int32, (t, LANES), 0)
    own = bpt * qi + jnp.right_shift(row, int(math.log2(MOBA_BLOCK)))
    past = lane < own
    neg_inf = -jnp.inf
    g0 = jnp.where(past, gate, neg_inf)
    g = g0
    kth = jnp.max(g, axis=-1, keepdims=True)
    for _ in range(MOBA_TOPK - 1):
        g = jnp.where(g >= kth, neg_inf, g)
        kth = jnp.max(g, axis=-1, keepdims=True)
    sel_ref[...] = jnp.where((g0 >= kth) & past, 1.0, 0.0).astype(BF16)

    def rowneg(j, r, diag):
        brow = lax.broadcasted_iota(jnp.int32, (LANES, bpt * LANES), 0)
        bcol = lax.broadcasted_iota(jnp.int32, (LANES, bpt * LANES), 1)
        pick = (brow == bpt * j + jnp.right_shift(bcol, int(math.log2(LANES)))).astype(BF16)
        keep = jnp.dot(sel_ref[r * rs:(r + 1) * rs, :], pick, preferred_element_type=F32)
        neg = (keep - 1.0) * (-MASKED)
        own = (r * rs) // MOBA_BLOCK
        return [None if diag and c // cpb >= own else neg[:, (c // cpb) * LANES:(c // cpb + 1) * LANES]
                for c in range(t // LANES)]

    o = _attend(q_ref, k_ref, v_ref, (kb_ref, kbmax_ref, h * nq, kn2_ref), qi, t=t,
                m_ref=m_ref, mb_ref=mb_ref, l_ref=l_ref, acc_ref=acc_ref, rowneg=rowneg)
    o_ref[0] = o.astype(o_ref.dtype)


def _moba(proj, kmean, kbias, *, q_blk, k_blk, v_blk):
    b, s, _ = proj.shape
    t = min(ATT_TILE, s)
    nq = s // t
    nblk = s // MOBA_BLOCK
    kb, kbmax = _tile_bias(kbias, MOBA_HEADS, nq, t)
    return pl.pallas_call(
        functools.partial(_moba_kernel, t=t, nq=nq, nblk=nblk),
        grid=(b, MOBA_HEADS, nq),
        in_specs=[
            pl.BlockSpec((1, t, HEAD_DIM), lambda bi, h, i: (bi, i, q_blk + h)),
            pl.BlockSpec((1, s, HEAD_DIM), lambda bi, h, i: (bi, 0, k_blk + h)),
            pl.BlockSpec((1, s, HEAD_DIM), lambda bi, h, i: (bi, 0, v_blk + h)),
            pl.BlockSpec((1, nblk, HEAD_DIM), lambda bi, h, i: (bi, 0, h)),
            pl.BlockSpec((1, 1, nq, 1, t), lambda bi, h, i: (0, h, 0, 0, 0)),
            pl.BlockSpec(memory_space=pltpu.SMEM),
        ],
        out_specs=pl.BlockSpec((1, t, HEAD_DIM), lambda bi, h, i: (bi, i, h)),
        out_shape=jax.ShapeDtypeStruct((b, s, MOBA_HEADS * HEAD_DIM), BF16),
        scratch_shapes=_att_scratch(t, HEAD_DIM) + [pltpu.VMEM((t, LANES), BF16)],
        compiler_params=_params(("parallel", "parallel", "arbitrary")),
        name="moba_attention",
    )(proj, proj, proj, kmean, kb, kbmax)


def _outproj_kernel(a_ref, d_ref, wa_ref, wd_ref, g_ref, h_ref, o_ref):
    y = jnp.dot(a_ref[...], wa_ref[...], preferred_element_type=F32)
    y = y + jnp.dot(d_ref[...], wd_ref[...], preferred_element_type=F32)
    o_ref[...] = h_ref[...] + _rms(y, g_ref[...])


def _outproj(a, d, wa, wd, g, h, *, tm=512):
    t, dm = h.shape
    tm = min(tm, t)
    ka, kd = a.shape[1], d.shape[1]
    return pl.pallas_call(
        _outproj_kernel,
        grid=(t // tm,),
        in_specs=[pl.BlockSpec((tm, ka), lambda i: (i, 0)),
                  pl.BlockSpec((tm, kd), lambda i: (i, 0)),
                  pl.BlockSpec((ka, dm), lambda i: (0, 0)),
                  pl.BlockSpec((kd, dm), lambda i: (0, 0)),
                  pl.BlockSpec((1, dm), lambda i: (0, 0)),
                  pl.BlockSpec((tm, dm), lambda i: (i, 0))],
        out_specs=pl.BlockSpec((tm, dm), lambda i: (i, 0)),
        out_shape=jax.ShapeDtypeStruct((t, dm), F32),
        compiler_params=_params(("parallel",)),
        name="mixer_outproj",
    )(a, d, wa, wd, g.reshape(1, dm), h)


def _xattn_kernel(h_ref, g2_ref, wq_ref, mkv_ref, wo_ref, g3_ref, o_ref, *, scale):
    h = h_ref[0]
    xn = _rms(h, g2_ref[...]).astype(BF16)
    q = (jnp.dot(xn, wq_ref[...], preferred_element_type=F32) * scale).astype(BF16)
    w = XATTN_HEADS * HEAD_DIM
    outs = []
    for hd in range(XATTN_HEADS):
        mk = mkv_ref[0, :, hd * HEAD_DIM:(hd + 1) * HEAD_DIM]
        mv = mkv_ref[0, :, w + hd * HEAD_DIM:w + (hd + 1) * HEAD_DIM]
        s = lax.dot_general(q[:, hd * HEAD_DIM:(hd + 1) * HEAD_DIM], mk, _NT, preferred_element_type=F32)
        p = jnp.exp(s - jnp.max(s, axis=-1, keepdims=True))
        p = p / jnp.sum(p, axis=-1, keepdims=True)
        outs.append(jnp.dot(p.astype(BF16), mv, preferred_element_type=F32).astype(BF16))
    y = jnp.dot(jnp.concatenate(outs, axis=-1), wo_ref[...], preferred_element_type=F32)
    o_ref[0] = h + _rms(y, g3_ref[...])


def _xattn(h, g2, wq, mkv, wo, g3, *, tm=512):
    b, s, dm = h.shape
    tm = min(tm, s)
    fixed = lambda bi, i: (0, 0)
    return pl.pallas_call(
        functools.partial(_xattn_kernel, scale=float(HEAD_DIM ** -0.5)),
        grid=(b, s // tm),
        in_specs=[pl.BlockSpec((1, tm, dm), lambda bi, i: (bi, i, 0)),
                  pl.BlockSpec((1, dm), fixed),
                  pl.BlockSpec(wq.shape, fixed),
                  pl.BlockSpec((1,) + mkv.shape[1:], lambda bi, i: (bi, 0, 0)),
                  pl.BlockSpec(wo.shape, fixed),
                  pl.BlockSpec((1, dm), fixed)],
        out_specs=pl.BlockSpec((1, tm, dm), lambda bi, i: (bi, i, 0)),
        out_shape=jax.ShapeDtypeStruct((b, s, dm), F32),
        compiler_params=_params(("parallel", "parallel")),
        name="memory_xattn",
    )(h, g2.reshape(1, dm), wq, mkv, wo, g3.reshape(1, dm))


def _ffn_kernel(h_ref, g4_ref, w1_ref, w2_ref, g5_ref, o_ref, xn_ref, acc_ref):
    f = pl.program_id(1)

    @pl.when(f == 0)
    def _():
        xn_ref[...] = _rms(h_ref[...], g4_ref[...]).astype(BF16)
        acc_ref[...] = jnp.zeros_like(acc_ref)

    u = jnp.maximum(jnp.dot(xn_ref[...], w1_ref[...], preferred_element_type=F32), 0.0)
    acc_ref[...] += jnp.dot((u * u).astype(BF16), w2_ref[...], preferred_element_type=F32)

    @pl.when(f == pl.num_programs(1) - 1)
    def _():
        o_ref[...] = h_ref[...] + _rms(acc_ref[...], g5_ref[...])


def _ffn(h, g4, w1, w2, g5, *, tm=512, tf=512):
    t, dm = h.shape
    tm = min(tm, t)
    f = w1.shape[1]
    return pl.pallas_call(
        _ffn_kernel,
        grid=(t // tm, f // tf),
        in_specs=[pl.BlockSpec((tm, dm), lambda i, j: (i, 0)),
                  pl.BlockSpec((1, dm), lambda i, j: (0, 0)),
                  pl.BlockSpec((dm, tf), lambda i, j: (0, j)),
                  pl.BlockSpec((tf, dm), lambda i, j: (j, 0)),
                  pl.BlockSpec((1, dm), lambda i, j: (0, 0))],
        out_specs=pl.BlockSpec((tm, dm), lambda i, j: (i, 0)),
        out_shape=jax.ShapeDtypeStruct((t, dm), F32),
        scratch_shapes=[pltpu.VMEM((tm, dm), BF16), pltpu.VMEM((tm, dm), F32)],
        compiler_params=_params(("parallel", "arbitrary")),
        name="relu2_mlp",
    )(h, g4.reshape(1, dm), w1, w2, g5.reshape(1, dm))


def _alibi_key_bias(n_heads, seq):
    slopes = jnp.asarray([2.0 ** (-8.0 * (i + 1) / n_heads) for i in range(n_heads)], dtype=F32)
    return (LOG2E * slopes[:, None] * jnp.arange(seq, dtype=F32)[None, :])[None]


def _rope_tables(seq):
    half = MLA_ROPE // 2
    inv = ROPE_THETA ** (-jnp.arange(0, MLA_ROPE, 2, dtype=F32) / MLA_ROPE)
    ang = jnp.arange(seq, dtype=F32)[:, None] * inv[None, :]
    cos, sin = jnp.cos(ang), jnp.sin(ang)
    zero = jnp.zeros((seq, LANES - 2 * half), F32)
    return jnp.concatenate([cos, cos, zero], axis=-1), jnp.concatenate([-sin, sin, zero], axis=-1)


def _rope_pair_columns(w_t1, w_t2):
    zero = jnp.zeros((w_t1.shape[0], LANES - 2 * w_t1.shape[1]), w_t1.dtype)
    return jnp.concatenate([w_t1, w_t2, zero, w_t2, w_t1, zero], axis=-1)


def kernel(x, mem, mem_norm, mem_wkv, norms, xattn_wq, xattn_wo, ffn_w1, ffn_w2, ab_w_in, ab_w_out, fox_b_f, diff_lambda, diff_subln, cd_w_in, cd_w_out, mla_q_norm, mla_kv_norm, mla_w_uq, mla_w_ukv):
    b, s, dm = x.shape
    t = b * s
    depth = norms.shape[0]
    q_scale = HEAD_DIM ** -0.5 * LOG2E
    half = MLA_ROPE // 2
    cos_t, sin_t = _rope_tables(s)

    mkv = _norm_matmul(mem.reshape(-1, dm), mem_norm, mem_wkv.astype(BF16)).reshape(b, mem.shape[1], -1)

    h = x.reshape(t, dm)
    for i in range(depth):
        n = norms[i]
        j = i // 2
        if i % 2 == 0:
            w = ab_w_in[j]
            fw = FOX_HEADS * HEAD_DIM
            dw = DIFF_HEADS * 2 * HEAD_DIM
            o_g = 3 * fw
            o_dq = o_g + FOX_HEADS
            w_main = jnp.concatenate([w[:, :fw] * q_scale, w[:, fw:o_g],
                                      w[:, o_dq:o_dq + dw] * q_scale, w[:, o_dq + dw:]], axis=-1).astype(BF16)
            w_gate = jnp.concatenate([w[:, o_g:o_dq].T, jnp.zeros((8, dm), F32)], axis=0).astype(BF16)
            proj, gate_t = _inproj(h, n[0], w_main, w_gate, "gate_t")
            proj = proj.reshape(b, s, -1)
            fox_kb = _fox_bias(gate_t, fox_b_f[j], b, s)
            a = _flash(proj, proj, proj, fox_kb, heads=FOX_HEADS, dk=HEAD_DIM, dv=HEAD_DIM,
                       q_blk=0, k_blk=FOX_HEADS, v_blk=2 * FOX_HEADS, name="fox_attention")
            lam_init = 0.8 - 0.6 * math.exp(-0.3 * i)
            d = _diff_attention(proj, _alibi_key_bias(DIFF_HEADS, s), diff_lambda[j], diff_subln[j],
                                q_blk=3 * FOX_HEADS, k_blk=3 * FOX_HEADS + 2 * DIFF_HEADS,
                                v_blk=(3 * fw + 2 * dw) // (2 * HEAD_DIM), lam_init=lam_init)
            wo = ab_w_out[j].astype(BF16)
            h = _outproj(a.reshape(t, -1), d.reshape(t, -1), wo[:fw], wo[fw:], n[1], h)
        else:
            w = cd_w_in[j]
            r2 = 2 * MLA_RANK
            o_m = r2 + MLA_ROPE
            mw = MOBA_HEADS * HEAD_DIM
            w_main = jnp.concatenate([w[:, :r2], w[:, o_m:o_m + mw] * q_scale, w[:, o_m + mw:]],
                                     axis=-1).astype(BF16)
            w_kpe = _rope_pair_columns(w[:, r2:r2 + half], w[:, r2 + half:o_m]).astype(BF16)
            proj, kpe_r = _inproj(h, n[0], w_main, w_kpe, "rope", (cos_t, sin_t))
            wq = mla_w_uq[j].reshape(MLA_RANK, MLA_HEADS, MLA_NOPE + MLA_ROPE)
            wq = jnp.concatenate(
                [jnp.concatenate([wq[:, hd, :MLA_NOPE],
                                  _rope_pair_columns(wq[:, hd, MLA_NOPE:MLA_NOPE + half], wq[:, hd, MLA_NOPE + half:])],
                                 axis=-1) for hd in range(MLA_HEADS)], axis=-1).astype(BF16)
            wkv = mla_w_ukv[j].reshape(MLA_RANK, MLA_HEADS, MLA_NOPE + MLA_V)
            wkv = jnp.concatenate([wkv[:, :, :MLA_NOPE].reshape(MLA_RANK, -1),
                                   wkv[:, :, MLA_NOPE:].reshape(MLA_RANK, -1)], axis=-1).astype(BF16)
            qf, kf, v = _mla_up(proj, kpe_r, mla_q_norm[j], mla_kv_norm[j], wq, wkv, cos_t, sin_t)
            c = _flash(qf.reshape(b, s, -1), kf.reshape(b, s, -1), v.reshape(b, s, -1), None,
                       heads=MLA_HEADS, dk=2 * LANES, dv=MLA_V, q_blk=0, k_blk=0, v_blk=0, name="mla_attention")
            proj = proj.reshape(b, s, -1)
            kmean = _kmean(proj, k_blk_wide=(r2 + mw) // mw)
            dout = _moba(proj, kmean, _alibi_key_bias(MOBA_HEADS, s),
                         q_blk=r2 // HEAD_DIM, k_blk=(r2 + mw) // HEAD_DIM, v_blk=(r2 + 2 * mw) // HEAD_DIM)
            wo = cd_w_out[j].astype(BF16)
            cw = MLA_HEADS * MLA_V
            h = _outproj(c.reshape(t, -1), dout.reshape(t, -1), wo[:cw], wo[cw:], n[1], h)
        h = _xattn(h.reshape(b, s, dm), n[2], xattn_wq[i].astype(BF16), mkv, xattn_wo[i].astype(BF16), n[3])
        h = _ffn(h.reshape(t, dm), n[4], ffn_w1[i].astype(BF16), ffn_w2[i].astype(BF16), n[5])
    return h.reshape(b, s, dm)
```

```python
import functools
import math

import jax
import jax.numpy as jnp
from jax import lax
from jax.experimental import pallas as pl
from jax.experimental.pallas import tpu as pltpu

F32 = jnp.float32
BF16 = jnp.bfloat16

NORM_EPS = 1e-6
HEAD_DIM = 128
FOX_HEADS = 8
DIFF_HEADS = 4
MLA_HEADS = 8
MLA_NOPE = 128
MLA_ROPE = 64
MLA_V = 128
MLA_RANK = 512
ROPE_THETA = 10000.0
MOBA_HEADS = 8
MOBA_BLOCK = 256
MOBA_TOPK = 3
XATTN_HEADS = 4
LANES = 128
MASKED = -1e30
VMEM_LIMIT = 48 * 1024 * 1024
LOG2E = math.log2(math.e)
SKIP_LOG2 = 160.0
MIN_ROW_SUM = 2.0 ** -64
ATT_TILE = 512
ATT_ROWS = 256

_NT = (((1,), (1,)), ((), ()))


def _params(sem):
    return pltpu.CompilerParams(dimension_semantics=sem, vmem_limit_bytes=VMEM_LIMIT)


def _rms(x, g):
    ms = jnp.mean(x * x, axis=-1, keepdims=True)
    return x * lax.rsqrt(ms + NORM_EPS) * g


def _inproj_kernel(x_ref, g_ref, w_ref, aux_w_ref, *rest, aux_mode):
    if aux_mode == "rope":
        cos_ref, sin_ref, o_ref, aux_ref, xn_ref = rest
    else:
        o_ref, aux_ref, xn_ref = rest

    @pl.when(pl.program_id(1) == 0)
    def _():
        xn = _rms(x_ref[...], g_ref[...]).astype(BF16)
        xn_ref[...] = xn
        if aux_mode == "gate_t":
            r = lax.dot_general(aux_w_ref[...], xn, _NT, preferred_element_type=F32)
            aux_ref[...] = r[:8]
        else:
            ab = jnp.dot(xn, aux_w_ref[...], preferred_element_type=F32)
            aux_ref[...] = (ab[:, :LANES] * cos_ref[...] + ab[:, LANES:] * sin_ref[...]).astype(aux_ref.dtype)

    o_ref[...] = jnp.dot(xn_ref[...], w_ref[...], preferred_element_type=F32).astype(o_ref.dtype)


def _inproj(x, g, w, aux_w, aux_mode, tables=None, *, tm=1024, tn=1024):
    t, k = x.shape
    n = w.shape[1]
    tm = min(tm, t)
    grid = (t // tm, n // tn)
    in_specs = [
        pl.BlockSpec((tm, k), lambda i, j: (i, 0)),
        pl.BlockSpec((1, k), lambda i, j: (0, 0)),
        pl.BlockSpec((k, tn), lambda i, j: (0, j)),
        pl.BlockSpec(aux_w.shape, lambda i, j: (0, 0)),
    ]
    args = [x, g.reshape(1, k), w, aux_w]
    if aux_mode == "rope":
        cos_t, sin_t = tables
        nrep = cos_t.shape[0] // tm
        in_specs += [pl.BlockSpec((tm, LANES), lambda i, j: (i % nrep, 0))] * 2
        args += [cos_t, sin_t]
        aux_shape = jax.ShapeDtypeStruct((t, LANES), BF16)
        aux_spec = pl.BlockSpec((tm, LANES), lambda i, j: (i, 0))
    else:
        aux_shape = jax.ShapeDtypeStruct((8, t), F32)
        aux_spec = pl.BlockSpec((8, tm), lambda i, j: (0, i))
    return pl.pallas_call(
        functools.partial(_inproj_kernel, aux_mode=aux_mode),
        grid=grid,
        in_specs=in_specs,
        out_specs=[pl.BlockSpec((tm, tn), lambda i, j: (i, j)), aux_spec],
        out_shape=[jax.ShapeDtypeStruct((t, n), BF16), aux_shape],
        scratch_shapes=[pltpu.VMEM((tm, k), BF16)],
        compiler_params=_params(("parallel", "arbitrary")),
        name="inproj_" + aux_mode,
    )(*args)


def _norm_matmul_kernel(x_ref, g_ref, w_ref, o_ref):
    xn = _rms(x_ref[...], g_ref[...]).astype(BF16)
    o_ref[...] = jnp.dot(xn, w_ref[...], preferred_element_type=F32).astype(o_ref.dtype)


def _norm_matmul(x, g, w, *, tm=256):
    t, k = x.shape
    n = w.shape[1]
    return pl.pallas_call(
        _norm_matmul_kernel,
        grid=(t // tm,),
        in_specs=[pl.BlockSpec((tm, k), lambda i: (i, 0)),
                  pl.BlockSpec((1, k), lambda i: (0, 0)),
                  pl.BlockSpec((k, n), lambda i: (0, 0))],
        out_specs=pl.BlockSpec((tm, n), lambda i: (i, 0)),
        out_shape=jax.ShapeDtypeStruct((t, n), BF16),
        compiler_params=_params(("parallel",)),
        name="mem_kv_proj",
    )(x, g.reshape(1, k), w)


def _fox_bias_kernel(g_ref, b_ref, o_ref, carry_ref, *, tc):
    @pl.when(pl.program_id(1) == 0)
    def _():
        carry_ref[...] = jnp.zeros_like(carry_ref)

    z = g_ref[...] + b_ref[...]
    logf = jnp.minimum(z, 0.0) - jnp.log(1.0 + jnp.exp(-jnp.abs(z)))
    upper = (lax.broadcasted_iota(jnp.int32, (tc, tc), 0)
             <= lax.broadcasted_iota(jnp.int32, (tc, tc), 1)).astype(F32)
    cum = jnp.dot(logf, upper, preferred_element_type=F32, precision=lax.Precision.HIGHEST) + carry_ref[...]
    carry_ref[...] = cum[:, tc - 1:tc]
    o_ref[0] = cum * (-LOG2E)


def _fox_bias(gate_t, b_f, batch, seq, *, tc=256):
    ns = seq // tc
    return pl.pallas_call(
        functools.partial(_fox_bias_kernel, tc=tc),
        grid=(batch, ns),
        in_specs=[pl.BlockSpec((8, tc), lambda b, s: (0, b * ns + s)),
                  pl.BlockSpec((8, 1), lambda b, s: (0, 0))],
        out_specs=pl.BlockSpec((1, 8, tc), lambda b, s: (b, 0, s)),
        out_shape=jax.ShapeDtypeStruct((batch, 8, seq), F32),
        scratch_shapes=[pltpu.VMEM((8, 1), F32)],
        compiler_params=_params(("parallel", "arbitrary")),
        name="fox_gate_cumsum",
    )(gate_t, b_f.reshape(8, 1))


def _key_norm2(k_ref, kn2_ref, t, nq):
    def body(j, mx):
        kk = k_ref[0, pl.ds(pl.multiple_of(j * t, t), t), :].astype(F32)
        return jnp.maximum(mx, jnp.sum(kk * kk, axis=-1, keepdims=True))
    mx = lax.fori_loop(0, nq, body, jnp.zeros((t, 1), F32))
    kn2_ref[...] = jnp.max(mx, axis=0, keepdims=True)


def _attend(q_ref, k_ref, v_ref, kn2_ref, bias, qi, *, t, m_ref, mb_ref, l_ref, s_ref, acc_ref, rowneg=None):
    rs = min(ATT_ROWS, t)
    nsub, nch = t // rs, t // LANES
    q0 = pl.multiple_of(qi * t, t)

    def rows(r):
        return slice(r * rs, (r + 1) * rs)

    if bias is not None:
        kb_ref, kbmax_ref, kb_base = bias
        shift = jnp.max(kb_ref[0, 0, qi], axis=-1, keepdims=True)

        def kbias(j):
            return kb_ref[0, 0, j] - shift
    else:
        def kbias(j):
            return None

    def chunk(s, kbv, c):
        sc = s[:, c * LANES:(c + 1) * LANES]
        return sc if kbv is None else sc + kbv[:, c * LANES:(c + 1) * LANES]

    def diag_chunks(r):
        width = (r + 1) * rs
        s = lax.dot_general(q_ref[0, rows(r), :], k_ref[0, pl.ds(q0, width), :], _NT, preferred_element_type=F32)
        kbv = kbias(qi)
        negs = rowneg(qi, r, True) if rowneg is not None else None
        lo = r * rs
        out = []
        for c in range(width // LANES):
            sc = chunk(s, kbv, c)
            if (c + 1) * LANES - 1 > lo:
                row = lo + lax.broadcasted_iota(jnp.int32, (rs, LANES), 0)
                col = c * LANES + lax.broadcasted_iota(jnp.int32, (rs, LANES), 1)
                sc = jnp.where(col <= row, sc, MASKED)
            elif negs is not None and negs[c] is not None:
                sc = sc + negs[c]
            out.append(sc)
        return out

    def raw_logits(r, kt):
        return lax.dot_general(q_ref[0, rows(r), :], kt, _NT, preferred_element_type=F32)

    def past_chunks(j, r, kt, kbv, s=None):
        s = raw_logits(r, kt) if s is None else s
        negs = rowneg(j, r, False) if rowneg is not None else None
        out = []
        for c in range(nch):
            sc = chunk(s, kbv, c)
            out.append(sc if negs is None else sc + negs[c])
        return out

    for r in range(nsub):
        m_ref[rows(r), :] = functools.reduce(jnp.maximum, diag_chunks(r))

    qf = q_ref[0].astype(F32)
    qk_bound = jnp.sqrt(jnp.sum(qf * qf, axis=-1, keepdims=True) * kn2_ref[...])
    if bias is not None:
        m_min = jnp.min(jnp.max(m_ref[...], axis=-1, keepdims=True), axis=0, keepdims=True)
        thresh = jnp.max(m_min - SKIP_LOG2 - jnp.max(qk_bound, axis=0, keepdims=True) + shift)
        j_start = lax.while_loop(lambda j: jnp.logical_and(j < qi, kbmax_ref[kb_base + j] < thresh),
                                 lambda j: j + 1, jnp.int32(0))
        kb_top = lax.fori_loop(1, qi + 1, lambda j, m: jnp.maximum(m, kbmax_ref[kb_base + j]),
                               kbmax_ref[kb_base])
        m_bound = qk_bound + (kb_top - shift)
    else:
        j_start = 0
        m_bound = qk_bound

    def exact_row_maxima():
        def tile(j, carry):
            kt = k_ref[0, pl.ds(pl.multiple_of(j * t, t), t), :]
            kbv = kbias(j)
            for r in range(nsub):
                m_ref[rows(r), :] = functools.reduce(jnp.maximum, past_chunks(j, r, kt, kbv), m_ref[rows(r), :])
            return carry

        lax.fori_loop(j_start, qi, tile, 0)
        mb_ref[...] = jnp.broadcast_to(jnp.max(m_ref[...], axis=-1, keepdims=True), (t, LANES))

    def probs(chunks, r):
        mb = mb_ref[rows(r), :]
        ps = [jnp.exp2(sc - mb) for sc in chunks]
        p = ps[0] if len(ps) == 1 else jnp.concatenate(ps, axis=-1)
        return functools.reduce(jnp.add, ps), p.astype(BF16)

    def accumulate():
        for r in range(nsub):
            lsum, p = probs(diag_chunks(r), r)
            l_ref[rows(r), :] = lsum
            acc_ref[rows(r), :] = jnp.dot(p, v_ref[0, pl.ds(q0, (r + 1) * rs), :], preferred_element_type=F32)

        def logits_into(slot, j):
            kt = k_ref[0, pl.ds(pl.multiple_of(j * t, t), t), :]
            for r in range(nsub):
                s_ref[slot, rows(r), :] = raw_logits(r, kt)

        def consume(slot, j):
            vt = v_ref[0, pl.ds(pl.multiple_of(j * t, t), t), :]
            kbv = kbias(j)
            for r in range(nsub):
                lsum, p = probs(past_chunks(j, r, None, kbv, s_ref[slot, rows(r), :]), r)
                l_ref[rows(r), :] += lsum
                acc_ref[rows(r), :] += jnp.dot(p, vt, preferred_element_type=F32)

        n_past = qi - j_start

        @pl.when(n_past > 0)
        def _():
            logits_into(0, j_start)

        def pair(i, carry):
            j = j_start + 2 * i
            logits_into(1, j + 1)
            consume(0, j)
            logits_into(0, jnp.minimum(j + 2, qi - 1))
            consume(1, j + 1)
            return carry

        lax.fori_loop(0, n_past // 2, pair, 0)

        @pl.when(n_past % 2 == 1)
        def _():
            consume(0, qi - 1)

        l = jnp.sum(l_ref[...], axis=-1, keepdims=True)
        acc_ref[...] = acc_ref[...] / l
        return jnp.min(l)

    mb_ref[...] = jnp.broadcast_to(m_bound, (t, LANES))
    l_min = accumulate()

    @pl.when(l_min < MIN_ROW_SUM)
    def _():
        exact_row_maxima()
        accumulate()


def _att_scratch(t, dv, streams=1):
    acc = (t, dv) if streams == 1 else (streams, t, dv)
    return [pltpu.VMEM((t, LANES), F32), pltpu.VMEM((t, LANES), F32), pltpu.VMEM((t, LANES), F32),
            pltpu.VMEM((2, t, t), F32), pltpu.VMEM(acc, F32)] + [pltpu.VMEM((1, 1), F32)] * streams


def _tile_bias(kbias, heads, nq, t):
    kb = kbias.reshape(kbias.shape[0], heads, nq, 1, t)
    return kb, jnp.max(kb, axis=(3, 4)).reshape(-1)


def _flash_kernel(*refs, t, nq, heads, kb_batched, has_bias):
    bi, h, qi = pl.program_id(0), pl.program_id(1), pl.program_id(2)
    if not has_bias:
        q_ref, k_ref, v_ref, o_ref, m_ref, mb_ref, l_ref, s_ref, acc_ref, kn2_ref = refs
        bias = None
    else:
        q_ref, k_ref, v_ref, kb_ref, kbmax_ref, o_ref, m_ref, mb_ref, l_ref, s_ref, acc_ref, kn2_ref = refs
        bias = (kb_ref, kbmax_ref, ((bi * heads if kb_batched else 0) + h) * nq)

    @pl.when(qi == 0)
    def _():
        _key_norm2(k_ref, kn2_ref, t, nq)

    _attend(q_ref, k_ref, v_ref, kn2_ref, bias, qi, t=t, m_ref=m_ref, mb_ref=mb_ref, l_ref=l_ref, s_ref=s_ref,
            acc_ref=acc_ref)
    o_ref[0] = acc_ref[...].astype(o_ref.dtype)


def _flash(q_arr, k_arr, v_arr, kbias, *, heads, dk, dv, q_blk, k_blk, v_blk, name):
    b, s, _ = q_arr.shape
    t = min(ATT_TILE, s)
    nq = s // t
    in_specs = [
        pl.BlockSpec((1, t, dk), lambda bi, h, i: (bi, i, q_blk + h)),
        pl.BlockSpec((1, s, dk), lambda bi, h, i: (bi, 0, k_blk + h)),
        pl.BlockSpec((1, s, dv), lambda bi, h, i: (bi, 0, v_blk + h)),
    ]
    args = [q_arr, k_arr, v_arr]
    kb_batched = kbias is not None and kbias.shape[0] == b
    if kbias is not None:
        kb, kbmax = _tile_bias(kbias, heads, nq, t)
        kb_b = (lambda bi: bi) if kb_batched else (lambda bi: 0)
        in_specs += [pl.BlockSpec((1, 1, nq, 1, t), lambda bi, h, i: (kb_b(bi), h, 0, 0, 0)),
                     pl.BlockSpec(memory_space=pltpu.SMEM)]
        args += [kb, kbmax]
    return pl.pallas_call(
        functools.partial(_flash_kernel, t=t, nq=nq, heads=heads, kb_batched=kb_batched,
                          has_bias=kbias is not None),
        grid=(b, heads, nq),
        in_specs=in_specs,
        out_specs=pl.BlockSpec((1, t, dv), lambda bi, h, i: (bi, i, h)),
        out_shape=jax.ShapeDtypeStruct((b, s, heads * dv), BF16),
        scratch_shapes=_att_scratch(t, dv),
        compiler_params=_params(("parallel", "parallel", "arbitrary")),
        name=name,
    )(*args)


def _diff_kernel(q1_ref, q2_ref, k1_ref, k2_ref, v_ref, kb_ref, kbmax_ref, lam_ref, g_ref, o_ref,
                 m_ref, mb_ref, l_ref, s_ref, acc_ref, kn2a_ref, kn2b_ref, *, t, nq, lam_init):
    h, qi = pl.program_id(1), pl.program_id(2)

    @pl.when(qi == 0)
    def _():
        _key_norm2(k1_ref, kn2a_ref, t, nq)
        _key_norm2(k2_ref, kn2b_ref, t, nq)

    scratch = dict(m_ref=m_ref, mb_ref=mb_ref, l_ref=l_ref, s_ref=s_ref)
    bias = (kb_ref, kbmax_ref, h * nq)
    _attend(q1_ref, k1_ref, v_ref, kn2a_ref, bias, qi, t=t, acc_ref=acc_ref.at[0], **scratch)
    _attend(q2_ref, k2_ref, v_ref, kn2b_ref, bias, qi, t=t, acc_ref=acc_ref.at[1], **scratch)
    lam = lam_ref[...]
    lam_full = (jnp.exp(jnp.sum(lam[0:1] * lam[1:2], axis=-1, keepdims=True))
                - jnp.exp(jnp.sum(lam[2:3] * lam[3:4], axis=-1, keepdims=True)) + lam_init)
    d = acc_ref[0] - lam_full * acc_ref[1]
    o_ref[0] = (_rms(d, g_ref[...]) * (1.0 - lam_init)).astype(o_ref.dtype)


def _diff_attention(proj, kbias, lam, subln, *, q_blk, k_blk, v_blk, lam_init):
    b, s, _ = proj.shape
    t = min(ATT_TILE, s)
    nq = s // t
    dv = 2 * HEAD_DIM
    kb, kbmax = _tile_bias(kbias, DIFF_HEADS, nq, t)
    return pl.pallas_call(
        functools.partial(_diff_kernel, t=t, nq=nq, lam_init=lam_init),
        grid=(b, DIFF_HEADS, nq),
        in_specs=[
            pl.BlockSpec((1, t, HEAD_DIM), lambda bi, h, i: (bi, i, q_blk + 2 * h)),
            pl.BlockSpec((1, t, HEAD_DIM), lambda bi, h, i: (bi, i, q_blk + 2 * h + 1)),
            pl.BlockSpec((1, s, HEAD_DIM), lambda bi, h, i: (bi, 0, k_blk + 2 * h)),
            pl.BlockSpec((1, s, HEAD_DIM), lambda bi, h, i: (bi, 0, k_blk + 2 * h + 1)),
            pl.BlockSpec((1, s, dv), lambda bi, h, i: (bi, 0, v_blk + h)),
            pl.BlockSpec((1, 1, nq, 1, t), lambda bi, h, i: (0, h, 0, 0, 0)),
            pl.BlockSpec(memory_space=pltpu.SMEM),
            pl.BlockSpec((4, HEAD_DIM), lambda bi, h, i: (0, 0)),
            pl.BlockSpec((1, dv), lambda bi, h, i: (0, 0)),
        ],
        out_specs=pl.BlockSpec((1, t, dv), lambda bi, h, i: (bi, i, h)),
        out_shape=jax.ShapeDtypeStruct((b, s, DIFF_HEADS * dv), BF16),
        scratch_shapes=_att_scratch(t, dv, streams=2),
        compiler_params=_params(("parallel", "parallel", "arbitrary")),
        name="diff_attention",
    )(proj, proj, proj, proj, proj, kb, kbmax, lam, subln.reshape(1, dv))


def _mla_up_kernel(cq_ref, ckv_ref, kpe_ref, gq_ref, gkv_ref, wq_ref, wkv_ref, cos_ref, sin_ref,
                   qf_ref, kf_ref, v_ref, *, scale):
    cqn = _rms(cq_ref[...].astype(F32), gq_ref[...]).astype(BF16)
    ckvn = _rms(ckv_ref[...].astype(F32), gkv_ref[...]).astype(BF16)
    cos = cos_ref[...]
    sin = sin_ref[...]
    kpe = kpe_ref[...]
    kv = jnp.dot(ckvn, wkv_ref[...], preferred_element_type=F32)
    nk = MLA_HEADS * MLA_NOPE
    v_ref[...] = kv[:, nk:].astype(v_ref.dtype)
    for h in range(MLA_HEADS):
        q3 = jnp.dot(cqn, wq_ref[:, h * 3 * LANES:(h + 1) * 3 * LANES], preferred_element_type=F32)
        rot = q3[:, LANES:2 * LANES] * cos + q3[:, 2 * LANES:] * sin
        qf_ref[:, 2 * h * LANES:(2 * h + 1) * LANES] = (q3[:, :LANES] * scale).astype(qf_ref.dtype)
        qf_ref[:, (2 * h + 1) * LANES:(2 * h + 2) * LANES] = (rot * scale).astype(qf_ref.dtype)
        kf_ref[:, 2 * h * LANES:(2 * h + 1) * LANES] = kv[:, h * LANES:(h + 1) * LANES].astype(kf_ref.dtype)
        kf_ref[:, (2 * h + 1) * LANES:(2 * h + 2) * LANES] = kpe


def _mla_up(proj, kpe_r, gq, gkv, wq, wkv, cos_t, sin_t, *, tm=512):
    t = proj.shape[0]
    tm = min(tm, t)
    nrep = cos_t.shape[0] // tm
    scale = float((MLA_NOPE + MLA_ROPE) ** -0.5 * LOG2E)
    wide = 2 * LANES * MLA_HEADS
    row = lambda i: (i, 0)
    fixed = lambda i: (0, 0)
    return pl.pallas_call(
        functools.partial(_mla_up_kernel, scale=scale),
        grid=(t // tm,),
        in_specs=[
            pl.BlockSpec((tm, MLA_RANK), lambda i: (i, 0)),
            pl.BlockSpec((tm, MLA_RANK), lambda i: (i, 1)),
            pl.BlockSpec((tm, LANES), row),
            pl.BlockSpec((1, MLA_RANK), fixed),
            pl.BlockSpec((1, MLA_RANK), fixed),
            pl.BlockSpec(wq.shape, fixed),
            pl.BlockSpec(wkv.shape, fixed),
            pl.BlockSpec((tm, LANES), lambda i: (i % nrep, 0)),
            pl.BlockSpec((tm, LANES), lambda i: (i % nrep, 0)),
        ],
        out_specs=[pl.BlockSpec((tm, wide), row), pl.BlockSpec((tm, wide), row),
                   pl.BlockSpec((tm, MLA_HEADS * MLA_V), row)],
        out_shape=[jax.ShapeDtypeStruct((t, wide), BF16), jax.ShapeDtypeStruct((t, wide), BF16),
                   jax.ShapeDtypeStruct((t, MLA_HEADS * MLA_V), BF16)],
        compiler_params=_params(("parallel",)),
        name="mla_up",
    )(proj, proj, kpe_r, gq.reshape(1, -1), gkv.reshape(1, -1), wq, wkv, cos_t, sin_t)


def _kmean_kernel(k_ref, o_ref):
    j = pl.program_id(1)
    o_ref[0, pl.ds(j, 1), :] = jnp.mean(k_ref[0].astype(F32), axis=0, keepdims=True)


def _kmean(proj, *, k_blk_wide):
    b, s, _ = proj.shape
    nblk = s // MOBA_BLOCK
    w = MOBA_HEADS * HEAD_DIM
    return pl.pallas_call(
        _kmean_kernel,
        grid=(b, nblk),
        in_specs=[pl.BlockSpec((1, MOBA_BLOCK, w), lambda bi, j: (bi, j, k_blk_wide))],
        out_specs=pl.BlockSpec((1, nblk, w), lambda bi, j: (bi, 0, 0)),
        out_shape=jax.ShapeDtypeStruct((b, nblk, w), F32),
        compiler_params=_params(("parallel", "arbitrary")),
        name="moba_kmean",
    )(proj)


def _moba_kernel(q_ref, k_ref, v_ref, km_ref, kb_ref, kbmax_ref, o_ref,
                 m_ref, mb_ref, l_ref, s_ref, acc_ref, kn2_ref, sel_ref, *, t, nq, nblk):
    h, qi = pl.program_id(1), pl.program_id(2)
    bpt = t // MOBA_BLOCK
    cpb = MOBA_BLOCK // LANES
    rs = min(ATT_ROWS, t)
    assert MOBA_BLOCK % rs == 0, "a row sub-tile must not straddle MoBA blocks"

    @pl.when(qi == 0)
    def _():
        _key_norm2(k_ref, kn2_ref, t, nq)

    km = km_ref[0]
    if nblk < LANES:
        km = jnp.concatenate([km, jnp.zeros((LANES - nblk, HEAD_DIM), F32)], axis=0)
    gate = lax.dot_general(q_ref[0].astype(F32), km, _NT, preferred_element_type=F32,
                           precision=lax.Precision.HIGHEST)
    lane = lax.broadcasted_iota(jnp.int32, (t, LANES), 1)
    row = lax.broadcasted_iota(jnp.int32, (t, LANES), 0)
    own = bpt * qi + jnp.right_shift(row, int(math.log2(MOBA_BLOCK)))
    past = lane < own
    neg_inf = -jnp.inf
    g0 = jnp.where(past, gate, neg_inf)
    g = g0
    kth = jnp.max(g, axis=-1, keepdims=True)
    for _ in range(MOBA_TOPK - 1):
        g = jnp.where(g >= kth, neg_inf, g)
        kth = jnp.max(g, axis=-1, keepdims=True)
    sel_ref[...] = jnp.where((g0 >= kth) & past, 1.0, 0.0).astype(BF16)

    def rowneg(j, r, diag):
        brow = lax.broadcasted_iota(jnp.int32, (LANES, bpt * LANES), 0)
        bcol = lax.broadcasted_iota(jnp.int32, (LANES, bpt * LANES), 1)
        pick = (brow == bpt * j + jnp.right_shift(bcol, int(math.log2(LANES)))).astype(BF16)
        keep = jnp.dot(sel_ref[r * rs:(r + 1) * rs, :], pick, preferred_element_type=F32)
        neg = (keep - 1.0) * (-MASKED)
        own = (r * rs) // MOBA_BLOCK
        return [None if diag and c // cpb >= own else neg[:, (c // cpb) * LANES:(c // cpb + 1) * LANES]
                for c in range(t // LANES)]

    _attend(q_ref, k_ref, v_ref, kn2_ref, (kb_ref, kbmax_ref, h * nq), qi, t=t,
            m_ref=m_ref, mb_ref=mb_ref, l_ref=l_ref, s_ref=s_ref, acc_ref=acc_ref, rowneg=rowneg)
    o_ref[0] = acc_ref[...].astype(o_ref.dtype)


def _moba(proj, kmean, kbias, *, q_blk, k_blk, v_blk):
    b, s, _ = proj.shape
    t = min(ATT_TILE, s)
    nq = s // t
    nblk = s // MOBA_BLOCK
    kb, kbmax = _tile_bias(kbias, MOBA_HEADS, nq, t)
    return pl.pallas_call(
        functools.partial(_moba_kernel, t=t, nq=nq, nblk=nblk),
        grid=(b, MOBA_HEADS, nq),
        in_specs=[
            pl.BlockSpec((1, t, HEAD_DIM), lambda bi, h, i: (bi, i, q_blk + h)),
            pl.BlockSpec((1, s, HEAD_DIM), lambda bi, h, i: (bi, 0, k_blk + h)),
            pl.BlockSpec((1, s, HEAD_DIM), lambda bi, h, i: (bi, 0, v_blk + h)),
            pl.BlockSpec((1, nblk, HEAD_DIM), lambda bi, h, i: (bi, 0, h)),
            pl.BlockSpec((1, 1, nq, 1, t), lambda bi, h, i: (0, h, 0, 0, 0)),
            pl.BlockSpec(memory_space=pltpu.SMEM),
        ],
        out_specs=pl.BlockSpec((1, t, HEAD_DIM), lambda bi, h, i: (bi, i, h)),
        out_shape=jax.ShapeDtypeStruct((b, s, MOBA_HEADS * HEAD_DIM), BF16),
        scratch_shapes=_att_scratch(t, HEAD_DIM) + [pltpu.VMEM((t, LANES), BF16)],
        compiler_params=_params(("parallel", "parallel", "arbitrary")),
        name="moba_attention",
    )(proj, proj, proj, kmean, kb, kbmax)


def _outproj_kernel(a_ref, d_ref, wa_ref, wd_ref, g_ref, h_ref, o_ref):
    y = jnp.dot(a_ref[...], wa_ref[...], preferred_element_type=F32)
    y = y + jnp.dot(d_ref[...], wd_ref[...], preferred_element_type=F32)
    o_ref[...] = h_ref[...] + _rms(y, g_ref[...])


def _outproj(a, d, wa, wd, g, h, *, tm=512):
    t, dm = h.shape
    tm = min(tm, t)
    ka, kd = a.shape[1], d.shape[1]
    return pl.pallas_call(
        _outproj_kernel,
        grid=(t // tm,),
        in_specs=[pl.BlockSpec((tm, ka), lambda i: (i, 0)),
                  pl.BlockSpec((tm, kd), lambda i: (i, 0)),
                  pl.BlockSpec((ka, dm), lambda i: (0, 0)),
                  pl.BlockSpec((kd, dm), lambda i: (0, 0)),
                  pl.BlockSpec((1, dm), lambda i: (0, 0)),
                  pl.BlockSpec((tm, dm), lambda i: (i, 0))],
        out_specs=pl.BlockSpec((tm, dm), lambda i: (i, 0)),
        out_shape=jax.ShapeDtypeStruct((t, dm), F32),
        compiler_params=_params(("parallel",)),
        name="mixer_outproj",
    )(a, d, wa, wd, g.reshape(1, dm), h)


def _xattn_kernel(h_ref, g2_ref, wq_ref, mkv_ref, wo_ref, g3_ref, o_ref, *, scale):
    h = h_ref[0]
    xn = _rms(h, g2_ref[...]).astype(BF16)
    q = (jnp.dot(xn, wq_ref[...], preferred_element_type=F32) * scale).astype(BF16)
    w = XATTN_HEADS * HEAD_DIM
    outs = []
    for hd in range(XATTN_HEADS):
        mk = mkv_ref[0, :, hd * HEAD_DIM:(hd + 1) * HEAD_DIM]
        mv = mkv_ref[0, :, w + hd * HEAD_DIM:w + (hd + 1) * HEAD_DIM]
        s = lax.dot_general(q[:, hd * HEAD_DIM:(hd + 1) * HEAD_DIM], mk, _NT, preferred_element_type=F32)
        p = jnp.exp(s - jnp.max(s, axis=-1, keepdims=True))
        p = p / jnp.sum(p, axis=-1, keepdims=True)
        outs.append(jnp.dot(p.astype(BF16), mv, preferred_element_type=F32).astype(BF16))
    y = jnp.dot(jnp.concatenate(outs, axis=-1), wo_ref[...], preferred_element_type=F32)
    o_ref[0] = h + _rms(y, g3_ref[...])


def _xattn(h, g2, wq, mkv, wo, g3, *, tm=512):
    b, s, dm = h.shape
    tm = min(tm, s)
    fixed = lambda bi, i: (0, 0)
    return pl.pallas_call(
        functools.partial(_xattn_kernel, scale=float(HEAD_DIM ** -0.5)),
        grid=(b, s // tm),
        in_specs=[pl.BlockSpec((1, tm, dm), lambda bi, i: (bi, i, 0)),
                  pl.BlockSpec((1, dm), fixed),
                  pl.BlockSpec(wq.shape, fixed),
                  pl.BlockSpec((1,) + mkv.shape[1:], lambda bi, i: (bi, 0, 0)),
                  pl.BlockSpec(wo.shape, fixed),
                  pl.BlockSpec((1, dm), fixed)],
        out_specs=pl.BlockSpec((1, tm, dm), lambda bi, i: (bi, i, 0)),
        out_shape=jax.ShapeDtypeStruct((b, s, dm), F32),
        compiler_params=_params(("parallel", "parallel")),
        name="memory_xattn",
    )(h, g2.reshape(1, dm), wq, mkv, wo, g3.reshape(1, dm))


def _ffn_kernel(h_ref, g4_ref, w1_ref, w2_ref, g5_ref, o_ref, xn_ref, acc_ref):
    f = pl.program_id(1)

    @pl.when(f == 0)
    def _():
        xn_ref[...] = _rms(h_ref[...], g4_ref[...]).astype(BF16)
        acc_ref[...] = jnp.zeros_like(acc_ref)

    u = jnp.maximum(jnp.dot(xn_ref[...], w1_ref[...], preferred_element_type=F32), 0.0)
    acc_ref[...] += jnp.dot((u * u).astype(BF16), w2_ref[...], preferred_element_type=F32)

    @pl.when(f == pl.num_programs(1) - 1)
    def _():
        o_ref[...] = h_ref[...] + _rms(acc_ref[...], g5_ref[...])


def _ffn(h, g4, w1, w2, g5, *, tm=512, tf=1024):
    t, dm = h.shape
    tm = min(tm, t)
    f = w1.shape[1]
    return pl.pallas_call(
        _ffn_kernel,
        grid=(t // tm, f // tf),
        in_specs=[pl.BlockSpec((tm, dm), lambda i, j: (i, 0)),
                  pl.BlockSpec((1, dm), lambda i, j: (0, 0)),
                  pl.BlockSpec((dm, tf), lambda i, j: (0, j)),
                  pl.BlockSpec((tf, dm), lambda i, j: (j, 0)),
                  pl.BlockSpec((1, dm), lambda i, j: (0, 0))],
        out_specs=pl.BlockSpec((tm, dm), lambda i, j: (i, 0)),
        out_shape=jax.ShapeDtypeStruct((t, dm), F32),
        scratch_shapes=[pltpu.VMEM((tm, dm), BF16), pltpu.VMEM((tm, dm), F32)],
        compiler_params=_params(("parallel", "arbitrary")),
        name="relu2_mlp",
    )(h, g4.reshape(1, dm), w1, w2, g5.reshape(1, dm))


def _alibi_key_bias(n_heads, seq):
    slopes = jnp.asarray([2.0 ** (-8.0 * (i + 1) / n_heads) for i in range(n_heads)], dtype=F32)
    return (LOG2E * slopes[:, None] * jnp.arange(seq, dtype=F32)[None, :])[None]


def _rope_tables(seq):
    half = MLA_ROPE // 2
    inv = ROPE_THETA ** (-jnp.arange(0, MLA_ROPE, 2, dtype=F32) / MLA_ROPE)
    ang = jnp.arange(seq, dtype=F32)[:, None] * inv[None, :]
    cos, sin = jnp.cos(ang), jnp.sin(ang)
    zero = jnp.zeros((seq, LANES - 2 * half), F32)
    return jnp.concatenate([cos, cos, zero], axis=-1), jnp.concatenate([-sin, sin, zero], axis=-1)


def _rope_pair_columns(w_t1, w_t2):
    zero = jnp.zeros((w_t1.shape[0], LANES - 2 * w_t1.shape[1]), w_t1.dtype)
    return jnp.concatenate([w_t1, w_t2, zero, w_t2, w_t1, zero], axis=-1)


def kernel(x, mem, mem_norm, mem_wkv, norms, xattn_wq, xattn_wo, ffn_w1, ffn_w2, ab_w_in, ab_w_out, fox_b_f, diff_lambda, diff_subln, cd_w_in, cd_w_out, mla_q_norm, mla_kv_norm, mla_w_uq, mla_w_ukv):
    b, s, dm = x.shape
    t = b * s
    depth = norms.shape[0]
    q_scale = HEAD_DIM ** -0.5 * LOG2E
    half = MLA_ROPE // 2
    cos_t, sin_t = _rope_tables(s)

    mkv = _norm_matmul(mem.reshape(-1, dm), mem_norm, mem_wkv.astype(BF16)).reshape(b, mem.shape[1], -1)

    h = x.reshape(t, dm)
    for i in range(depth):
        n = norms[i]
        j = i // 2
        if i % 2 == 0:
            w = ab_w_in[j]
            fw = FOX_HEADS * HEAD_DIM
            dw = DIFF_HEADS * 2 * HEAD_DIM
            o_g = 3 * fw
            o_dq = o_g + FOX_HEADS
            w_main = jnp.concatenate([w[:, :fw] * q_scale, w[:, fw:o_g],
                                      w[:, o_dq:o_dq + dw] * q_scale, w[:, o_dq + dw:]], axis=-1).astype(BF16)
            w_gate = jnp.concatenate([w[:, o_g:o_dq].T, jnp.zeros((8, dm), F32)], axis=0).astype(BF16)
            proj, gate_t = _inproj(h, n[0], w_main, w_gate, "gate_t")
            proj = proj.reshape(b, s, -1)
            fox_kb = _fox_bias(gate_t, fox_b_f[j], b, s)
            a = _flash(proj, proj, proj, fox_kb, heads=FOX_HEADS, dk=HEAD_DIM, dv=HEAD_DIM,
                       q_blk=0, k_blk=FOX_HEADS, v_blk=2 * FOX_HEADS, name="fox_attention")
            lam_init = 0.8 - 0.6 * math.exp(-0.3 * i)
            d = _diff_attention(proj, _alibi_key_bias(DIFF_HEADS, s), diff_lambda[j], diff_subln[j],
                                q_blk=3 * FOX_HEADS, k_blk=3 * FOX_HEADS + 2 * DIFF_HEADS,
                                v_blk=(3 * fw + 2 * dw) // (2 * HEAD_DIM), lam_init=lam_init)
            wo = ab_w_out[j].astype(BF16)
            h = _outproj(a.reshape(t, -1), d.reshape(t, -1), wo[:fw], wo[fw:], n[1], h)
        else:
            w = cd_w_in[j]
            r2 = 2 * MLA_RANK
            o_m = r2 + MLA_ROPE
            mw = MOBA_HEADS * HEAD_DIM
            w_main = jnp.concatenate([w[:, :r2], w[:, o_m:o_m + mw] * q_scale, w[:, o_m + mw:]],
                                     axis=-1).astype(BF16)
            w_kpe = _rope_pair_columns(w[:, r2:r2 + half], w[:, r2 + half:o_m]).astype(BF16)
            proj, kpe_r = _inproj(h, n[0], w_main, w_kpe, "rope", (cos_t, sin_t))
            wq = mla_w_uq[j].reshape(MLA_RANK, MLA_HEADS, MLA_NOPE + MLA_ROPE)
            wq = jnp.concatenate(
                [jnp.concatenate([wq[:, hd, :MLA_NOPE],
                                  _rope_pair_columns(wq[:, hd, MLA_NOPE:MLA_NOPE + half], wq[:, hd, MLA_NOPE + half:])],
                                 axis=-1) for hd in range(MLA_HEADS)], axis=-1).astype(BF16)
            wkv = mla_w_ukv[j].reshape(MLA_RANK, MLA_HEADS, MLA_NOPE + MLA_V)
            wkv = jnp.concatenate([wkv[:, :, :MLA_NOPE].reshape(MLA_RANK, -1),
                                   wkv[:, :, MLA_NOPE:].reshape(MLA_RANK, -1)], axis=-1).astype(BF16)
            qf, kf, v = _mla_up(proj, kpe_r, mla_q_norm[j], mla_kv_norm[j], wq, wkv, cos_t, sin_t)
            c = _flash(qf.reshape(b, s, -1), kf.reshape(b, s, -1), v.reshape(b, s, -1), None,
                       heads=MLA_HEADS, dk=2 * LANES, dv=MLA_V, q_blk=0, k_blk=0, v_blk=0, name="mla_attention")
            proj = proj.reshape(b, s, -1)
            kmean = _kmean(proj, k_blk_wide=(r2 + mw) // mw)
            dout = _moba(proj, kmean, _alibi_key_bias(MOBA_HEADS, s),
                         q_blk=r2 // HEAD_DIM, k_blk=(r2 + mw) // HEAD_DIM, v_blk=(r2 + 2 * mw) // HEAD_DIM)
            wo = cd_w_out[j].astype(BF16)
            cw = MLA_HEADS * MLA_V
            h = _outproj(c.reshape(t, -1), dout.reshape(t, -1), wo[:cw], wo[cw:], n[1], h)
        h = _xattn(h.reshape(b, s, dm), n[2], xattn_wq[i].astype(BF16), mkv, xattn_wo[i].astype(BF16), n[3])
        h = _ffn(h.reshape(t, dm), n[4], ffn_w1[i].astype(BF16), ffn_w2[i].astype(BF16), n[5])
    return h.reshape(b, s, dm)
```

```python
import functools
import math

import jax
import jax.numpy as jnp
from jax import lax
from jax.experimental import pallas as pl
from jax.experimental.pallas import tpu as pltpu

F32 = jnp.float32
BF16 = jnp.bfloat16

NORM_EPS = 1e-6
HEAD_DIM = 128
FOX_HEADS = 8
DIFF_HEADS = 4
MLA_HEADS = 8
MLA_NOPE = 128
MLA_ROPE = 64
MLA_V = 128
MLA_RANK = 512
ROPE_THETA = 10000.0
MOBA_HEADS = 8
MOBA_BLOCK = 256
MOBA_TOPK = 3
XATTN_HEADS = 4
LANES = 128
MASKED = -1e30
VMEM_LIMIT = 48 * 1024 * 1024
LOG2E = math.log2(math.e)
SKIP_LOG2 = 160.0
MIN_ROW_SUM = 2.0 ** -64
BLOCK_OFF = -2.0 ** 100
ATT_TILE = 512
ATT_ROWS = 256

_NT = (((1,), (1,)), ((), ()))


def _params(sem):
    return pltpu.CompilerParams(dimension_semantics=sem, vmem_limit_bytes=VMEM_LIMIT)


def _rms(x, g):
    ms = jnp.mean(x * x, axis=-1, keepdims=True)
    return x * lax.rsqrt(ms + NORM_EPS) * g


def _inproj_kernel(x_ref, g_ref, w_ref, aux_w_ref, *rest, aux_mode):
    if aux_mode == "rope":
        cos_ref, sin_ref, o_ref, aux_ref, xn_ref = rest
    else:
        o_ref, aux_ref, xn_ref = rest

    @pl.when(pl.program_id(1) == 0)
    def _():
        xn = _rms(x_ref[...], g_ref[...]).astype(BF16)
        xn_ref[...] = xn
        if aux_mode == "gate_t":
            r = lax.dot_general(aux_w_ref[...], xn, _NT, preferred_element_type=F32)
            aux_ref[...] = r[:8]
        else:
            ab = jnp.dot(xn, aux_w_ref[...], preferred_element_type=F32)
            aux_ref[...] = (ab[:, :LANES] * cos_ref[...] + ab[:, LANES:] * sin_ref[...]).astype(aux_ref.dtype)

    o_ref[...] = jnp.dot(xn_ref[...], w_ref[...], preferred_element_type=F32).astype(o_ref.dtype)


def _inproj(x, g, w, aux_w, aux_mode, tables=None, *, tm=1024, tn=1024):
    t, k = x.shape
    n = w.shape[1]
    tm = min(tm, t)
    grid = (t // tm, n // tn)
    in_specs = [
        pl.BlockSpec((tm, k), lambda i, j: (i, 0)),
        pl.BlockSpec((1, k), lambda i, j: (0, 0)),
        pl.BlockSpec((k, tn), lambda i, j: (0, j)),
        pl.BlockSpec(aux_w.shape, lambda i, j: (0, 0)),
    ]
    args = [x, g.reshape(1, k), w, aux_w]
    if aux_mode == "rope":
        cos_t, sin_t = tables
        nrep = cos_t.shape[0] // tm
        in_specs += [pl.BlockSpec((tm, LANES), lambda i, j: (i % nrep, 0))] * 2
        args += [cos_t, sin_t]
        aux_shape = jax.ShapeDtypeStruct((t, LANES), BF16)
        aux_spec = pl.BlockSpec((tm, LANES), lambda i, j: (i, 0))
    else:
        aux_shape = jax.ShapeDtypeStruct((8, t), F32)
        aux_spec = pl.BlockSpec((8, tm), lambda i, j: (0, i))
    return pl.pallas_call(
        functools.partial(_inproj_kernel, aux_mode=aux_mode),
        grid=grid,
        in_specs=in_specs,
        out_specs=[pl.BlockSpec((tm, tn), lambda i, j: (i, j)), aux_spec],
        out_shape=[jax.ShapeDtypeStruct((t, n), BF16), aux_shape],
        scratch_shapes=[pltpu.VMEM((tm, k), BF16)],
        compiler_params=_params(("parallel", "arbitrary")),
        name="inproj_" + aux_mode,
    )(*args)


def _norm_matmul_kernel(x_ref, g_ref, w_ref, o_ref):
    xn = _rms(x_ref[...], g_ref[...]).astype(BF16)
    o_ref[...] = jnp.dot(xn, w_ref[...], preferred_element_type=F32).astype(o_ref.dtype)


def _norm_matmul(x, g, w, *, tm=256):
    t, k = x.shape
    n = w.shape[1]
    return pl.pallas_call(
        _norm_matmul_kernel,
        grid=(t // tm,),
        in_specs=[pl.BlockSpec((tm, k), lambda i: (i, 0)),
                  pl.BlockSpec((1, k), lambda i: (0, 0)),
                  pl.BlockSpec((k, n), lambda i: (0, 0))],
        out_specs=pl.BlockSpec((tm, n), lambda i: (i, 0)),
        out_shape=jax.ShapeDtypeStruct((t, n), BF16),
        compiler_params=_params(("parallel",)),
        name="mem_kv_proj",
    )(x, g.reshape(1, k), w)


def _fox_bias_kernel(g_ref, b_ref, o_ref, carry_ref, *, tc):
    @pl.when(pl.program_id(1) == 0)
    def _():
        carry_ref[...] = jnp.zeros_like(carry_ref)

    z = g_ref[...] + b_ref[...]
    logf = jnp.minimum(z, 0.0) - jnp.log(1.0 + jnp.exp(-jnp.abs(z)))
    upper = (lax.broadcasted_iota(jnp.int32, (tc, tc), 0)
             <= lax.broadcasted_iota(jnp.int32, (tc, tc), 1)).astype(F32)
    cum = jnp.dot(logf, upper, preferred_element_type=F32, precision=lax.Precision.HIGHEST) + carry_ref[...]
    carry_ref[...] = cum[:, tc - 1:tc]
    o_ref[0] = cum * (-LOG2E)


def _fox_bias(gate_t, b_f, batch, seq, *, tc=256):
    ns = seq // tc
    return pl.pallas_call(
        functools.partial(_fox_bias_kernel, tc=tc),
        grid=(batch, ns),
        in_specs=[pl.BlockSpec((8, tc), lambda b, s: (0, b * ns + s)),
                  pl.BlockSpec((8, 1), lambda b, s: (0, 0))],
        out_specs=pl.BlockSpec((1, 8, tc), lambda b, s: (b, 0, s)),
        out_shape=jax.ShapeDtypeStruct((batch, 8, seq), F32),
        scratch_shapes=[pltpu.VMEM((8, 1), F32)],
        compiler_params=_params(("parallel", "arbitrary")),
        name="fox_gate_cumsum",
    )(gate_t, b_f.reshape(8, 1))


def _key_norm2(k_ref, kn2_ref, t, nq):
    def body(j, mx):
        kk = k_ref[0, pl.ds(pl.multiple_of(j * t, t), t), :].astype(F32)
        return jnp.maximum(mx, jnp.sum(kk * kk, axis=-1, keepdims=True))
    mx = lax.fori_loop(0, nq, body, jnp.zeros((t, 1), F32))
    kn2_ref[...] = jnp.max(mx, axis=0, keepdims=True)


def _attend(q_ref, k_ref, v_ref, kn2_ref, bias, qi, *, t, m_ref, mb_ref, l_ref, s_ref, acc_ref, extra=None):
    rs = min(ATT_ROWS, t)
    nsub, nch = t // rs, t // LANES
    q0 = pl.multiple_of(qi * t, t)

    def rows(r):
        return slice(r * rs, (r + 1) * rs)

    if bias is not None:
        kb_ref, kbmax_ref, kb_base = bias
        shift = jnp.max(kb_ref[0, 0, qi], axis=-1, keepdims=True)

        def kbias(j):
            return kb_ref[0, 0, j] - shift
    else:
        def kbias(j):
            return None

    def chunk(s, kbv, c):
        sc = s[:, c * LANES:(c + 1) * LANES]
        return sc if kbv is None else sc + kbv[:, c * LANES:(c + 1) * LANES]

    def raw_logits(r, k0, width):
        lhs = q_ref[0, rows(r), :]
        rhs = k_ref[0, pl.ds(k0, width), :]
        if extra is not None:
            qx_ref, kx_ref = extra
            lhs = jnp.concatenate([lhs, qx_ref[rows(r), :]], axis=-1)
            rhs = jnp.concatenate([rhs, kx_ref[pl.ds(k0, width), :]], axis=-1)
        return lax.dot_general(lhs, rhs, _NT, preferred_element_type=F32)

    def diag_chunks(r):
        width = (r + 1) * rs
        s = raw_logits(r, q0, width)
        kbv = kbias(qi)
        lo = r * rs
        out = []
        for c in range(width // LANES):
            sc = chunk(s, kbv, c)
            if (c + 1) * LANES - 1 > lo:
                row = lo + lax.broadcasted_iota(jnp.int32, (rs, LANES), 0)
                col = c * LANES + lax.broadcasted_iota(jnp.int32, (rs, LANES), 1)
                sc = jnp.where(col <= row, sc, MASKED)
            out.append(sc)
        return out

    def past_chunks(s, kbv):
        return [chunk(s, kbv, c) for c in range(nch)]

    qf = q_ref[0].astype(F32)
    qk_bound = jnp.sqrt(jnp.sum(qf * qf, axis=-1, keepdims=True) * kn2_ref[...])
    if bias is not None:
        kb_d = kb_ref[0, 0, qi]
        row = lax.broadcasted_iota(jnp.int32, (t, t), 0)
        col = lax.broadcasted_iota(jnp.int32, (t, t), 1)
        kb_self = jnp.sum(jnp.where(col == row, kb_d, 0.0), axis=-1, keepdims=True)
        self_logit = jnp.sum(qf * k_ref[0, pl.ds(q0, t), :].astype(F32), axis=-1, keepdims=True) + (kb_self - shift)
        m_min = jnp.min(self_logit, axis=0, keepdims=True)
        thresh = jnp.max(m_min - SKIP_LOG2 - jnp.max(qk_bound, axis=0, keepdims=True) + shift)
        j_start = lax.while_loop(lambda j: jnp.logical_and(j < qi, kbmax_ref[kb_base + j] < thresh),
                                 lambda j: j + 1, jnp.int32(0))
        kb_own = jnp.max(jnp.where(col <= row, kb_d, -jnp.inf), axis=-1, keepdims=True)
        kb_prev = lax.fori_loop(0, qi, lambda j, m: jnp.maximum(m, kbmax_ref[kb_base + j]), jnp.float32(-jnp.inf))
        m_bound = qk_bound + (jnp.maximum(kb_own, kb_prev) - shift)
    else:
        j_start = 0
        m_bound = qk_bound

    def exact_row_maxima():
        for r in range(nsub):
            m_ref[rows(r), :] = functools.reduce(jnp.maximum, diag_chunks(r))

        def tile(j, carry):
            k0 = pl.multiple_of(j * t, t)
            kbv = kbias(j)
            for r in range(nsub):
                m_ref[rows(r), :] = functools.reduce(jnp.maximum, past_chunks(raw_logits(r, k0, t), kbv),
                                                     m_ref[rows(r), :])
            return carry

        lax.fori_loop(j_start, qi, tile, 0)
        mb_ref[...] = jnp.broadcast_to(jnp.max(m_ref[...], axis=-1, keepdims=True), (t, LANES))

    def probs(chunks, r):
        mb = mb_ref[rows(r), :]
        ps = [jnp.exp2(sc - mb) for sc in chunks]
        p = ps[0] if len(ps) == 1 else jnp.concatenate(ps, axis=-1)
        return functools.reduce(jnp.add, ps), p.astype(BF16)

    def accumulate():
        for r in range(nsub):
            lsum, p = probs(diag_chunks(r), r)
            l_ref[rows(r), :] = lsum
            acc_ref[rows(r), :] = jnp.dot(p, v_ref[0, pl.ds(q0, (r + 1) * rs), :], preferred_element_type=F32)

        def logits_into(slot, j):
            k0 = pl.multiple_of(j * t, t)
            for r in range(nsub):
                s_ref[slot, rows(r), :] = raw_logits(r, k0, t)

        def consume(slot, j):
            vt = v_ref[0, pl.ds(pl.multiple_of(j * t, t), t), :]
            kbv = kbias(j)
            for r in range(nsub):
                lsum, p = probs(past_chunks(s_ref[slot, rows(r), :], kbv), r)
                l_ref[rows(r), :] += lsum
                acc_ref[rows(r), :] += jnp.dot(p, vt, preferred_element_type=F32)

        n_past = qi - j_start

        @pl.when(n_past > 0)
        def _():
            logits_into(0, j_start)

        def pair(i, carry):
            j = j_start + 2 * i
            logits_into(1, j + 1)
            consume(0, j)
            logits_into(0, jnp.minimum(j + 2, qi - 1))
            consume(1, j + 1)
            return carry

        lax.fori_loop(0, n_past // 2, pair, 0)

        @pl.when(n_past % 2 == 1)
        def _():
            consume(0, qi - 1)

        l = jnp.sum(l_ref[...], axis=-1, keepdims=True)
        acc_ref[...] = acc_ref[...] / l
        return jnp.min(l)

    mb_ref[...] = jnp.broadcast_to(m_bound, (t, LANES))
    l_min = accumulate()

    @pl.when(l_min < MIN_ROW_SUM)
    def _():
        exact_row_maxima()
        accumulate()


def _att_scratch(t, dv, streams=1):
    acc = (t, dv) if streams == 1 else (streams, t, dv)
    return [pltpu.VMEM((t, LANES), F32), pltpu.VMEM((t, LANES), F32), pltpu.VMEM((t, LANES), F32),
            pltpu.VMEM((2, t, t), F32), pltpu.VMEM(acc, F32)] + [pltpu.VMEM((1, 1), F32)] * streams


def _tile_bias(kbias, heads, nq, t):
    kb = kbias.reshape(kbias.shape[0], heads, nq, 1, t)
    return kb, jnp.max(kb, axis=(3, 4)).reshape(-1)


def _flash_kernel(*refs, t, nq, heads, kb_batched, has_bias):
    bi, h, qi = pl.program_id(0), pl.program_id(1), pl.program_id(2)
    if not has_bias:
        q_ref, k_ref, v_ref, o_ref, m_ref, mb_ref, l_ref, s_ref, acc_ref, kn2_ref = refs
        bias = None
    else:
        q_ref, k_ref, v_ref, kb_ref, kbmax_ref, o_ref, m_ref, mb_ref, l_ref, s_ref, acc_ref, kn2_ref = refs
        bias = (kb_ref, kbmax_ref, ((bi * heads if kb_batched else 0) + h) * nq)

    @pl.when(qi == 0)
    def _():
        _key_norm2(k_ref, kn2_ref, t, nq)

    _attend(q_ref, k_ref, v_ref, kn2_ref, bias, qi, t=t, m_ref=m_ref, mb_ref=mb_ref, l_ref=l_ref, s_ref=s_ref,
            acc_ref=acc_ref)
    o_ref[0] = acc_ref[...].astype(o_ref.dtype)


def _flash(q_arr, k_arr, v_arr, kbias, *, heads, dk, dv, q_blk, k_blk, v_blk, name):
    b, s, _ = q_arr.shape
    t = min(ATT_TILE, s)
    nq = s // t
    in_specs = [
        pl.BlockSpec((1, t, dk), lambda bi, h, i: (bi, i, q_blk + h)),
        pl.BlockSpec((1, s, dk), lambda bi, h, i: (bi, 0, k_blk + h)),
        pl.BlockSpec((1, s, dv), lambda bi, h, i: (bi, 0, v_blk + h)),
    ]
    args = [q_arr, k_arr, v_arr]
    kb_batched = kbias is not None and kbias.shape[0] == b
    if kbias is not None:
        kb, kbmax = _tile_bias(kbias, heads, nq, t)
        kb_b = (lambda bi: bi) if kb_batched else (lambda bi: 0)
        in_specs += [pl.BlockSpec((1, 1, nq, 1, t), lambda bi, h, i: (kb_b(bi), h, 0, 0, 0)),
                     pl.BlockSpec(memory_space=pltpu.SMEM)]
        args += [kb, kbmax]
    return pl.pallas_call(
        functools.partial(_flash_kernel, t=t, nq=nq, heads=heads, kb_batched=kb_batched,
                          has_bias=kbias is not None),
        grid=(b, heads, nq),
        in_specs=in_specs,
        out_specs=pl.BlockSpec((1, t, dv), lambda bi, h, i: (bi, i, h)),
        out_shape=jax.ShapeDtypeStruct((b, s, heads * dv), BF16),
        scratch_shapes=_att_scratch(t, dv),
        compiler_params=_params(("parallel", "parallel", "arbitrary")),
        name=name,
    )(*args)


def _diff_kernel(q1_ref, q2_ref, k1_ref, k2_ref, v_ref, kb_ref, kbmax_ref, lam_ref, g_ref, o_ref,
                 m_ref, mb_ref, l_ref, s_ref, acc_ref, kn2a_ref, kn2b_ref, *, t, nq, lam_init):
    h, qi = pl.program_id(1), pl.program_id(2)

    @pl.when(qi == 0)
    def _():
        _key_norm2(k1_ref, kn2a_ref, t, nq)
        _key_norm2(k2_ref, kn2b_ref, t, nq)

    scratch = dict(m_ref=m_ref, mb_ref=mb_ref, l_ref=l_ref, s_ref=s_ref)
    bias = (kb_ref, kbmax_ref, h * nq)
    _attend(q1_ref, k1_ref, v_ref, kn2a_ref, bias, qi, t=t, acc_ref=acc_ref.at[0], **scratch)
    _attend(q2_ref, k2_ref, v_ref, kn2b_ref, bias, qi, t=t, acc_ref=acc_ref.at[1], **scratch)
    lam = lam_ref[...]
    lam_full = (jnp.exp(jnp.sum(lam[0:1] * lam[1:2], axis=-1, keepdims=True))
                - jnp.exp(jnp.sum(lam[2:3] * lam[3:4], axis=-1, keepdims=True)) + lam_init)
    d = acc_ref[0] - lam_full * acc_ref[1]
    o_ref[0] = (_rms(d, g_ref[...]) * (1.0 - lam_init)).astype(o_ref.dtype)


def _diff_attention(proj, kbias, lam, subln, *, q_blk, k_blk, v_blk, lam_init):
    b, s, _ = proj.shape
    t = min(ATT_TILE, s)
    nq = s // t
    dv = 2 * HEAD_DIM
    kb, kbmax = _tile_bias(kbias, DIFF_HEADS, nq, t)
    return pl.pallas_call(
        functools.partial(_diff_kernel, t=t, nq=nq, lam_init=lam_init),
        grid=(b, DIFF_HEADS, nq),
        in_specs=[
            pl.BlockSpec((1, t, HEAD_DIM), lambda bi, h, i: (bi, i, q_blk + 2 * h)),
            pl.BlockSpec((1, t, HEAD_DIM), lambda bi, h, i: (bi, i, q_blk + 2 * h + 1)),
            pl.BlockSpec((1, s, HEAD_DIM), lambda bi, h, i: (bi, 0, k_blk + 2 * h)),
            pl.BlockSpec((1, s, HEAD_DIM), lambda bi, h, i: (bi, 0, k_blk + 2 * h + 1)),
            pl.BlockSpec((1, s, dv), lambda bi, h, i: (bi, 0, v_blk + h)),
            pl.BlockSpec((1, 1, nq, 1, t), lambda bi, h, i: (0, h, 0, 0, 0)),
            pl.BlockSpec(memory_space=pltpu.SMEM),
            pl.BlockSpec((4, HEAD_DIM), lambda bi, h, i: (0, 0)),
            pl.BlockSpec((1, dv), lambda bi, h, i: (0, 0)),
        ],
        out_specs=pl.BlockSpec((1, t, dv), lambda bi, h, i: (bi, i, h)),
        out_shape=jax.ShapeDtypeStruct((b, s, DIFF_HEADS * dv), BF16),
        scratch_shapes=_att_scratch(t, dv, streams=2),
        compiler_params=_params(("parallel", "parallel", "arbitrary")),
        name="diff_attention",
    )(proj, proj, proj, proj, proj, kb, kbmax, lam, subln.reshape(1, dv))


def _mla_up_kernel(cq_ref, ckv_ref, kpe_ref, gq_ref, gkv_ref, wq_ref, wkv_ref, cos_ref, sin_ref,
                   qf_ref, kf_ref, v_ref, *, scale):
    cqn = _rms(cq_ref[...].astype(F32), gq_ref[...]).astype(BF16)
    ckvn = _rms(ckv_ref[...].astype(F32), gkv_ref[...]).astype(BF16)
    cos = cos_ref[...]
    sin = sin_ref[...]
    kpe = kpe_ref[...]
    kv = jnp.dot(ckvn, wkv_ref[...], preferred_element_type=F32)
    nk = MLA_HEADS * MLA_NOPE
    v_ref[...] = kv[:, nk:].astype(v_ref.dtype)
    for h in range(MLA_HEADS):
        q3 = jnp.dot(cqn, wq_ref[:, h * 3 * LANES:(h + 1) * 3 * LANES], preferred_element_type=F32)
        rot = q3[:, LANES:2 * LANES] * cos + q3[:, 2 * LANES:] * sin
        qf_ref[:, 2 * h * LANES:(2 * h + 1) * LANES] = (q3[:, :LANES] * scale).astype(qf_ref.dtype)
        qf_ref[:, (2 * h + 1) * LANES:(2 * h + 2) * LANES] = (rot * scale).astype(qf_ref.dtype)
        kf_ref[:, 2 * h * LANES:(2 * h + 1) * LANES] = kv[:, h * LANES:(h + 1) * LANES].astype(kf_ref.dtype)
        kf_ref[:, (2 * h + 1) * LANES:(2 * h + 2) * LANES] = kpe


def _mla_up(proj, kpe_r, gq, gkv, wq, wkv, cos_t, sin_t, *, tm=512):
    t = proj.shape[0]
    tm = min(tm, t)
    nrep = cos_t.shape[0] // tm
    scale = float((MLA_NOPE + MLA_ROPE) ** -0.5 * LOG2E)
    wide = 2 * LANES * MLA_HEADS
    row = lambda i: (i, 0)
    fixed = lambda i: (0, 0)
    return pl.pallas_call(
        functools.partial(_mla_up_kernel, scale=scale),
        grid=(t // tm,),
        in_specs=[
            pl.BlockSpec((tm, MLA_RANK), lambda i: (i, 0)),
            pl.BlockSpec((tm, MLA_RANK), lambda i: (i, 1)),
            pl.BlockSpec((tm, LANES), row),
            pl.BlockSpec((1, MLA_RANK), fixed),
            pl.BlockSpec((1, MLA_RANK), fixed),
            pl.BlockSpec(wq.shape, fixed),
            pl.BlockSpec(wkv.shape, fixed),
            pl.BlockSpec((tm, LANES), lambda i: (i % nrep, 0)),
            pl.BlockSpec((tm, LANES), lambda i: (i % nrep, 0)),
        ],
        out_specs=[pl.BlockSpec((tm, wide), row), pl.BlockSpec((tm, wide), row),
                   pl.BlockSpec((tm, MLA_HEADS * MLA_V), row)],
        out_shape=[jax.ShapeDtypeStruct((t, wide), BF16), jax.ShapeDtypeStruct((t, wide), BF16),
                   jax.ShapeDtypeStruct((t, MLA_HEADS * MLA_V), BF16)],
        compiler_params=_params(("parallel",)),
        name="mla_up",
    )(proj, proj, kpe_r, gq.reshape(1, -1), gkv.reshape(1, -1), wq, wkv, cos_t, sin_t)


def _kmean_kernel(k_ref, o_ref):
    j = pl.program_id(1)
    o_ref[0, pl.ds(j, 1), :] = jnp.mean(k_ref[0].astype(F32), axis=0, keepdims=True)


def _kmean(proj, *, k_blk_wide):
    b, s, _ = proj.shape
    nblk = s // MOBA_BLOCK
    w = MOBA_HEADS * HEAD_DIM
    return pl.pallas_call(
        _kmean_kernel,
        grid=(b, nblk),
        in_specs=[pl.BlockSpec((1, MOBA_BLOCK, w), lambda bi, j: (bi, j, k_blk_wide))],
        out_specs=pl.BlockSpec((1, nblk, w), lambda bi, j: (bi, 0, 0)),
        out_shape=jax.ShapeDtypeStruct((b, nblk, w), F32),
        compiler_params=_params(("parallel", "arbitrary")),
        name="moba_kmean",
    )(proj)


def _moba_kernel(q_ref, k_ref, v_ref, km_ref, kx_ref, kb_ref, kbmax_ref, o_ref,
                 m_ref, mb_ref, l_ref, s_ref, acc_ref, kn2_ref, qx_ref, *, t, nq, nblk):
    h, qi = pl.program_id(1), pl.program_id(2)
    bpt = t // MOBA_BLOCK

    @pl.when(qi == 0)
    def _():
        _key_norm2(k_ref, kn2_ref, t, nq)

    km = km_ref[0]
    if nblk < LANES:
        km = jnp.concatenate([km, jnp.zeros((LANES - nblk, HEAD_DIM), F32)], axis=0)
    gate = lax.dot_general(q_ref[0].astype(F32), km, _NT, preferred_element_type=F32,
                           precision=lax.Precision.HIGHEST)
    lane = lax.broadcasted_iota(jnp.int32, (t, LANES), 1)
    row = lax.broadcasted_iota(jnp.int32, (t, LANES), 0)
    own = bpt * qi + jnp.right_shift(row, int(math.log2(MOBA_BLOCK)))
    past = lane < own
    neg_inf = -jnp.inf
    g0 = jnp.where(past, gate, neg_inf)
    g = g0
    kth = jnp.max(g, axis=-1, keepdims=True)
    for _ in range(MOBA_TOPK - 1):
        g = jnp.where(g >= kth, neg_inf, g)
        kth = jnp.max(g, axis=-1, keepdims=True)
    qx_ref[...] = jnp.where(past & (g0 < kth), BLOCK_OFF, 0.0).astype(BF16)

    _attend(q_ref, k_ref, v_ref, kn2_ref, (kb_ref, kbmax_ref, h * nq), qi, t=t,
            m_ref=m_ref, mb_ref=mb_ref, l_ref=l_ref, s_ref=s_ref, acc_ref=acc_ref, extra=(qx_ref, kx_ref))
    o_ref[0] = acc_ref[...].astype(o_ref.dtype)


def _moba(proj, kmean, kbias, *, q_blk, k_blk, v_blk):
    b, s, _ = proj.shape
    t = min(ATT_TILE, s)
    nq = s // t
    nblk = s // MOBA_BLOCK
    kb, kbmax = _tile_bias(kbias, MOBA_HEADS, nq, t)
    key_block = (jnp.arange(s)[:, None] // MOBA_BLOCK == jnp.arange(LANES)[None, :]).astype(BF16)
    return pl.pallas_call(
        functools.partial(_moba_kernel, t=t, nq=nq, nblk=nblk),
        grid=(b, MOBA_HEADS, nq),
        in_specs=[
            pl.BlockSpec((1, t, HEAD_DIM), lambda bi, h, i: (bi, i, q_blk + h)),
            pl.BlockSpec((1, s, HEAD_DIM), lambda bi, h, i: (bi, 0, k_blk + h)),
            pl.BlockSpec((1, s, HEAD_DIM), lambda bi, h, i: (bi, 0, v_blk + h)),
            pl.BlockSpec((1, nblk, HEAD_DIM), lambda bi, h, i: (bi, 0, h)),
            pl.BlockSpec((s, LANES), lambda bi, h, i: (0, 0)),
            pl.BlockSpec((1, 1, nq, 1, t), lambda bi, h, i: (0, h, 0, 0, 0)),
            pl.BlockSpec(memory_space=pltpu.SMEM),
        ],
        out_specs=pl.BlockSpec((1, t, HEAD_DIM), lambda bi, h, i: (bi, i, h)),
        out_shape=jax.ShapeDtypeStruct((b, s, MOBA_HEADS * HEAD_DIM), BF16),
        scratch_shapes=_att_scratch(t, HEAD_DIM) + [pltpu.VMEM((t, LANES), BF16)],
        compiler_params=_params(("parallel", "parallel", "arbitrary")),
        name="moba_attention",
    )(proj, proj, proj, kmean, key_block, kb, kbmax)


def _outproj_kernel(a_ref, d_ref, wa_ref, wd_ref, g_ref, h_ref, o_ref):
    y = jnp.dot(a_ref[...], wa_ref[...], preferred_element_type=F32)
    y = y + jnp.dot(d_ref[...], wd_ref[...], preferred_element_type=F32)
    o_ref[...] = h_ref[...] + _rms(y, g_ref[...])


def _outproj(a, d, wa, wd, g, h, *, tm=512):
    t, dm = h.shape
    tm = min(tm, t)
    ka, kd = a.shape[1], d.shape[1]
    return pl.pallas_call(
        _outproj_kernel,
        grid=(t // tm,),
        in_specs=[pl.BlockSpec((tm, ka), lambda i: (i, 0)),
                  pl.BlockSpec((tm, kd), lambda i: (i, 0)),
                  pl.BlockSpec((ka, dm), lambda i: (0, 0)),
                  pl.BlockSpec((kd, dm), lambda i: (0, 0)),
                  pl.BlockSpec((1, dm), lambda i: (0, 0)),
                  pl.BlockSpec((tm, dm), lambda i: (i, 0))],
        out_specs=pl.BlockSpec((tm, dm), lambda i: (i, 0)),
        out_shape=jax.ShapeDtypeStruct((t, dm), F32),
        compiler_params=_params(("parallel",)),
        name="mixer_outproj",
    )(a, d, wa, wd, g.reshape(1, dm), h)


def _xattn_kernel(h_ref, g2_ref, wq_ref, mkv_ref, wo_ref, g3_ref, o_ref, *, scale):
    h = h_ref[0]
    xn = _rms(h, g2_ref[...]).astype(BF16)
    q = (jnp.dot(xn, wq_ref[...], preferred_element_type=F32) * scale).astype(BF16)
    w = XATTN_HEADS * HEAD_DIM
    outs = []
    for hd in range(XATTN_HEADS):
        mk = mkv_ref[0, :, hd * HEAD_DIM:(hd + 1) * HEAD_DIM]
        mv = mkv_ref[0, :, w + hd * HEAD_DIM:w + (hd + 1) * HEAD_DIM]
        s = lax.dot_general(q[:, hd * HEAD_DIM:(hd + 1) * HEAD_DIM], mk, _NT, preferred_element_type=F32)
        p = jnp.exp(s - jnp.max(s, axis=-1, keepdims=True))
        p = p / jnp.sum(p, axis=-1, keepdims=True)
        outs.append(jnp.dot(p.astype(BF16), mv, preferred_element_type=F32).astype(BF16))
    y = jnp.dot(jnp.concatenate(outs, axis=-1), wo_ref[...], preferred_element_type=F32)
    o_ref[0] = h + _rms(y, g3_ref[...])


def _xattn(h, g2, wq, mkv, wo, g3, *, tm=512):
    b, s, dm = h.shape
    tm = min(tm, s)
    fixed = lambda bi, i: (0, 0)
    return pl.pallas_call(
        functools.partial(_xattn_kernel, scale=float(HEAD_DIM ** -0.5)),
        grid=(b, s // tm),
        in_specs=[pl.BlockSpec((1, tm, dm), lambda bi, i: (bi, i, 0)),
                  pl.BlockSpec((1, dm), fixed),
                  pl.BlockSpec(wq.shape, fixed),
                  pl.BlockSpec((1,) + mkv.shape[1:], lambda bi, i: (bi, 0, 0)),
                  pl.BlockSpec(wo.shape, fixed),
                  pl.BlockSpec((1, dm), fixed)],
        out_specs=pl.BlockSpec((1, tm, dm), lambda bi, i: (bi, i, 0)),
        out_shape=jax.ShapeDtypeStruct((b, s, dm), F32),
        compiler_params=_params(("parallel", "parallel")),
        name="memory_xattn",
    )(h, g2.reshape(1, dm), wq, mkv, wo, g3.reshape(1, dm))


def _ffn_kernel(h_ref, g4_ref, w1_ref, w2_ref, g5_ref, o_ref, xn_ref, acc_ref):
    f = pl.program_id(1)

    @pl.when(f == 0)
    def _():
        xn_ref[...] = _rms(h_ref[...], g4_ref[...]).astype(BF16)
        acc_ref[...] = jnp.zeros_like(acc_ref)

    u = jnp.maximum(jnp.dot(xn_ref[...], w1_ref[...], preferred_element_type=F32), 0.0)
    acc_ref[...] += jnp.dot((u * u).astype(BF16), w2_ref[...], preferred_element_type=F32)

    @pl.when(f == pl.num_programs(1) - 1)
    def _():
        o_ref[...] = h_ref[...] + _rms(acc_ref[...], g5_ref[...])


def _ffn(h, g4, w1, w2, g5, *, tm=512, tf=1024):
    t, dm = h.shape
    tm = min(tm, t)
    f = w1.shape[1]
    return pl.pallas_call(
        _ffn_kernel,
        grid=(t // tm, f // tf),
        in_specs=[pl.BlockSpec((tm, dm), lambda i, j: (i, 0)),
                  pl.BlockSpec((1, dm), lambda i, j: (0, 0)),
                  pl.BlockSpec((dm, tf), lambda i, j: (0, j)),
                  pl.BlockSpec((tf, dm), lambda i, j: (j, 0)),
                  pl.BlockSpec((1, dm), lambda i, j: (0, 0))],
        out_specs=pl.BlockSpec((tm, dm), lambda i, j: (i, 0)),
        out_shape=jax.ShapeDtypeStruct((t, dm), F32),
        scratch_shapes=[pltpu.VMEM((tm, dm), BF16), pltpu.VMEM((tm, dm), F32)],
        compiler_params=_params(("parallel", "arbitrary")),
        name="relu2_mlp",
    )(h, g4.reshape(1, dm), w1, w2, g5.reshape(1, dm))


def _alibi_key_bias(n_heads, seq):
    slopes = jnp.asarray([2.0 ** (-8.0 * (i + 1) / n_heads) for i in range(n_heads)], dtype=F32)
    return (LOG2E * slopes[:, None] * jnp.arange(seq, dtype=F32)[None, :])[None]


def _rope_tables(seq):
    half = MLA_ROPE // 2
    inv = ROPE_THETA ** (-jnp.arange(0, MLA_ROPE, 2, dtype=F32) / MLA_ROPE)
    ang = jnp.arange(seq, dtype=F32)[:, None] * inv[None, :]
    cos, sin = jnp.cos(ang), jnp.sin(ang)
    zero = jnp.zeros((seq, LANES - 2 * half), F32)
    return jnp.concatenate([cos, cos, zero], axis=-1), jnp.concatenate([-sin, sin, zero], axis=-1)


def _rope_pair_columns(w_t1, w_t2):
    zero = jnp.zeros((w_t1.shape[0], LANES - 2 * w_t1.shape[1]), w_t1.dtype)
    return jnp.concatenate([w_t1, w_t2, zero, w_t2, w_t1, zero], axis=-1)


def kernel(x, mem, mem_norm, mem_wkv, norms, xattn_wq, xattn_wo, ffn_w1, ffn_w2, ab_w_in, ab_w_out, fox_b_f, diff_lambda, diff_subln, cd_w_in, cd_w_out, mla_q_norm, mla_kv_norm, mla_w_uq, mla_w_ukv):
    b, s, dm = x.shape
    t = b * s
    depth = norms.shape[0]
    q_scale = HEAD_DIM ** -0.5 * LOG2E
    half = MLA_ROPE // 2
    cos_t, sin_t = _rope_tables(s)

    mkv = _norm_matmul(mem.reshape(-1, dm), mem_norm, mem_wkv.astype(BF16)).reshape(b, mem.shape[1], -1)

    h = x.reshape(t, dm)
    for i in range(depth):
        n = norms[i]
        j = i // 2
        if i % 2 == 0:
            w = ab_w_in[j]
            fw = FOX_HEADS * HEAD_DIM
            dw = DIFF_HEADS * 2 * HEAD_DIM
            o_g = 3 * fw
            o_dq = o_g + FOX_HEADS
            w_main = jnp.concatenate([w[:, :fw] * q_scale, w[:, fw:o_g],
                                      w[:, o_dq:o_dq + dw] * q_scale, w[:, o_dq + dw:]], axis=-1).astype(BF16)
            w_gate = jnp.concatenate([w[:, o_g:o_dq].T, jnp.zeros((8, dm), F32)], axis=0).astype(BF16)
            proj, gate_t = _inproj(h, n[0], w_main, w_gate, "gate_t")
            proj = proj.reshape(b, s, -1)
            fox_kb = _fox_bias(gate_t, fox_b_f[j], b, s)
            a = _flash(proj, proj, proj, fox_kb, heads=FOX_HEADS, dk=HEAD_DIM, dv=HEAD_DIM,
                       q_blk=0, k_blk=FOX_HEADS, v_blk=2 * FOX_HEADS, name="fox_attention")
            lam_init = 0.8 - 0.6 * math.exp(-0.3 * i)
            d = _diff_attention(proj, _alibi_key_bias(DIFF_HEADS, s), diff_lambda[j], diff_subln[j],
                                q_blk=3 * FOX_HEADS, k_blk=3 * FOX_HEADS + 2 * DIFF_HEADS,
                                v_blk=(3 * fw + 2 * dw) // (2 * HEAD_DIM), lam_init=lam_init)
            wo = ab_w_out[j].astype(BF16)
            h = _outproj(a.reshape(t, -1), d.reshape(t, -1), wo[:fw], wo[fw:], n[1], h)
        else:
            w = cd_w_in[j]
            r2 = 2 * MLA_RANK
            o_m = r2 + MLA_ROPE
            mw = MOBA_HEADS * HEAD_DIM
            w_main = jnp.concatenate([w[:, :r2], w[:, o_m:o_m + mw] * q_scale, w[:, o_m + mw:]],
                                     axis=-1).astype(BF16)
            w_kpe = _rope_pair_columns(w[:, r2:r2 + half], w[:, r2 + half:o_m]).astype(BF16)
            proj, kpe_r = _inproj(h, n[0], w_main, w_kpe, "rope", (cos_t, sin_t))
            wq = mla_w_uq[j].reshape(MLA_RANK, MLA_HEADS, MLA_NOPE + MLA_ROPE)
            wq = jnp.concatenate(
                [jnp.concatenate([wq[:, hd, :MLA_NOPE],
                                  _rope_pair_columns(wq[:, hd, MLA_NOPE:MLA_NOPE + half], wq[:, hd, MLA_NOPE + half:])],
                                 axis=-1) for hd in range(MLA_HEADS)], axis=-1).astype(BF16)
            wkv = mla_w_ukv[j].reshape(MLA_RANK, MLA_HEADS, MLA_NOPE + MLA_V)
            wkv = jnp.concatenate([wkv[:, :, :MLA_NOPE].reshape(MLA_RANK, -1),
                                   wkv[:, :, MLA_NOPE:].reshape(MLA_RANK, -1)], axis=-1).astype(BF16)
            qf, kf, v = _mla_up(proj, kpe_r, mla_q_norm[j], mla_kv_norm[j], wq, wkv, cos_t, sin_t)
            c = _flash(qf.reshape(b, s, -1), kf.reshape(b, s, -1), v.reshape(b, s, -1), None,
                       heads=MLA_HEADS, dk=2 * LANES, dv=MLA_V, q_blk=0, k_blk=0, v_blk=0, name="mla_attention")
            proj = proj.reshape(b, s, -1)
            kmean = _kmean(proj, k_blk_wide=(r2 + mw) // mw)
            dout = _moba(proj, kmean, _alibi_key_bias(MOBA_HEADS, s),
                         q_blk=r2 // HEAD_DIM, k_blk=(r2 + mw) // HEAD_DIM, v_blk=(r2 + 2 * mw) // HEAD_DIM)
            wo = cd_w_out[j].astype(BF16)
            cw = MLA_HEADS * MLA_V
            h = _outproj(c.reshape(t, -1), dout.reshape(t, -1), wo[:cw], wo[cw:], n[1], h)
        h = _xattn(h.reshape(b, s, dm), n[2], xattn_wq[i].astype(BF16), mkv, xattn_wo[i].astype(BF16), n[3])
        h = _ffn(h.reshape(t, dm), n[4], ffn_w1[i].astype(BF16), ffn_w2[i].astype(BF16), n[5])
    return h.reshape(b, s, dm)
```

```python
import functools
import math

import jax
import jax.numpy as jnp
from jax import lax
from jax.experimental import pallas as pl
from jax.experimental.pallas import tpu as pltpu

F32 = jnp.float32
BF16 = jnp.bfloat16

NORM_EPS = 1e-6
HEAD_DIM = 128
FOX_HEADS = 8
DIFF_HEADS = 4
MLA_HEADS = 8
MLA_NOPE = 128
MLA_ROPE = 64
MLA_V = 128
MLA_RANK = 512
ROPE_THETA = 10000.0
MOBA_HEADS = 8
MOBA_BLOCK = 256
MOBA_TOPK = 3
XATTN_HEADS = 4
LANES = 128
MASKED = -1e30
VMEM_LIMIT = 48 * 1024 * 1024
LOG2E = math.log2(math.e)
SKIP_LOG2 = 160.0
MIN_ROW_SUM = 2.0 ** -64
BLOCK_OFF = -2.0 ** 100
ATT_TILE = 512
ATT_ROWS = 256

_NT = (((1,), (1,)), ((), ()))


def _params(sem):
    return pltpu.CompilerParams(dimension_semantics=sem, vmem_limit_bytes=VMEM_LIMIT)


def _rms(x, g):
    ms = jnp.mean(x * x, axis=-1, keepdims=True)
    return x * lax.rsqrt(ms + NORM_EPS) * g


def _inproj_kernel(x_ref, g_ref, w_ref, aux_w_ref, *rest, aux_mode):
    if aux_mode == "rope":
        cos_ref, sin_ref, o_ref, aux_ref, xn_ref = rest
    else:
        o_ref, aux_ref, xn_ref = rest

    @pl.when(pl.program_id(1) == 0)
    def _():
        xn = _rms(x_ref[...], g_ref[...]).astype(BF16)
        xn_ref[...] = xn
        if aux_mode == "gate_t":
            r = lax.dot_general(aux_w_ref[...], xn, _NT, preferred_element_type=F32)
            aux_ref[...] = r[:8]
        else:
            ab = jnp.dot(xn, aux_w_ref[...], preferred_element_type=F32)
            aux_ref[...] = (ab[:, :LANES] * cos_ref[...] + ab[:, LANES:] * sin_ref[...]).astype(aux_ref.dtype)

    o_ref[...] = jnp.dot(xn_ref[...], w_ref[...], preferred_element_type=F32).astype(o_ref.dtype)


def _inproj(x, g, w, aux_w, aux_mode, tables=None, *, tm=1024, tn=1024):
    t, k = x.shape
    n = w.shape[1]
    tm = min(tm, t)
    grid = (t // tm, n // tn)
    in_specs = [
        pl.BlockSpec((tm, k), lambda i, j: (i, 0)),
        pl.BlockSpec((1, k), lambda i, j: (0, 0)),
        pl.BlockSpec((k, tn), lambda i, j: (0, j)),
        pl.BlockSpec(aux_w.shape, lambda i, j: (0, 0)),
    ]
    args = [x, g.reshape(1, k), w, aux_w]
    if aux_mode == "rope":
        cos_t, sin_t = tables
        nrep = cos_t.shape[0] // tm
        in_specs += [pl.BlockSpec((tm, LANES), lambda i, j: (i % nrep, 0))] * 2
        args += [cos_t, sin_t]
        aux_shape = jax.ShapeDtypeStruct((t, LANES), BF16)
        aux_spec = pl.BlockSpec((tm, LANES), lambda i, j: (i, 0))
    else:
        aux_shape = jax.ShapeDtypeStruct((8, t), F32)
        aux_spec = pl.BlockSpec((8, tm), lambda i, j: (0, i))
    return pl.pallas_call(
        functools.partial(_inproj_kernel, aux_mode=aux_mode),
        grid=grid,
        in_specs=in_specs,
        out_specs=[pl.BlockSpec((tm, tn), lambda i, j: (i, j)), aux_spec],
        out_shape=[jax.ShapeDtypeStruct((t, n), BF16), aux_shape],
        scratch_shapes=[pltpu.VMEM((tm, k), BF16)],
        compiler_params=_params(("parallel", "arbitrary")),
        name="inproj_" + aux_mode,
    )(*args)


def _norm_matmul_kernel(x_ref, g_ref, w_ref, o_ref):
    xn = _rms(x_ref[...], g_ref[...]).astype(BF16)
    o_ref[...] = jnp.dot(xn, w_ref[...], preferred_element_type=F32).astype(o_ref.dtype)


def _norm_matmul(x, g, w, *, tm=256):
    t, k = x.shape
    n = w.shape[1]
    return pl.pallas_call(
        _norm_matmul_kernel,
        grid=(t // tm,),
        in_specs=[pl.BlockSpec((tm, k), lambda i: (i, 0)),
                  pl.BlockSpec((1, k), lambda i: (0, 0)),
                  pl.BlockSpec((k, n), lambda i: (0, 0))],
        out_specs=pl.BlockSpec((tm, n), lambda i: (i, 0)),
        out_shape=jax.ShapeDtypeStruct((t, n), BF16),
        compiler_params=_params(("parallel",)),
        name="mem_kv_proj",
    )(x, g.reshape(1, k), w)


def _fox_bias_kernel(g_ref, b_ref, o_ref, carry_ref, *, tc):
    @pl.when(pl.program_id(1) == 0)
    def _():
        carry_ref[...] = jnp.zeros_like(carry_ref)

    z = g_ref[...] + b_ref[...]
    logf = jnp.minimum(z, 0.0) - jnp.log(1.0 + jnp.exp(-jnp.abs(z)))
    upper = (lax.broadcasted_iota(jnp.int32, (tc, tc), 0)
             <= lax.broadcasted_iota(jnp.int32, (tc, tc), 1)).astype(F32)
    cum = jnp.dot(logf, upper, preferred_element_type=F32, precision=lax.Precision.HIGHEST) + carry_ref[...]
    carry_ref[...] = cum[:, tc - 1:tc]
    o_ref[0] = cum * (-LOG2E)


def _fox_bias(gate_t, b_f, batch, seq, *, tc=256):
    ns = seq // tc
    return pl.pallas_call(
        functools.partial(_fox_bias_kernel, tc=tc),
        grid=(batch, ns),
        in_specs=[pl.BlockSpec((8, tc), lambda b, s: (0, b * ns + s)),
                  pl.BlockSpec((8, 1), lambda b, s: (0, 0))],
        out_specs=pl.BlockSpec((1, 8, tc), lambda b, s: (b, 0, s)),
        out_shape=jax.ShapeDtypeStruct((batch, 8, seq), F32),
        scratch_shapes=[pltpu.VMEM((8, 1), F32)],
        compiler_params=_params(("parallel", "arbitrary")),
        name="fox_gate_cumsum",
    )(gate_t, b_f.reshape(8, 1))


def _key_norm2(k_ref, kn2_ref, t, nq):
    def body(j, mx):
        kk = k_ref[0, pl.ds(pl.multiple_of(j * t, t), t), :].astype(F32)
        return jnp.maximum(mx, jnp.sum(kk * kk, axis=-1, keepdims=True))
    mx = lax.fori_loop(0, nq, body, jnp.zeros((t, 1), F32))
    kn2_ref[...] = jnp.max(mx, axis=0, keepdims=True)


def _attend(q_ref, k_ref, v_ref, kn2_ref, bias, qi, *, t, m_ref, mb_ref, l_ref, s_ref, acc_ref, extra=None):
    rs = min(ATT_ROWS, t)
    nsub, nch = t // rs, t // LANES
    q0 = pl.multiple_of(qi * t, t)

    def rows(r):
        return slice(r * rs, (r + 1) * rs)

    if bias is not None:
        kb_ref, kbmax_ref, kb_base = bias
        shift = jnp.max(kb_ref[0, 0, qi], axis=-1, keepdims=True)

        def kbias(j):
            return kb_ref[0, 0, j] - shift
    else:
        def kbias(j):
            return None

    def chunk(s, kbv, c):
        sc = s[:, c * LANES:(c + 1) * LANES]
        return sc if kbv is None else sc + kbv[:, c * LANES:(c + 1) * LANES]

    def raw_logits(r, k0, width):
        lhs = q_ref[0, rows(r), :]
        rhs = k_ref[0, pl.ds(k0, width), :]
        if extra is not None:
            qx_ref, kx_ref = extra
            lhs = jnp.concatenate([lhs, qx_ref[rows(r), :]], axis=-1)
            rhs = jnp.concatenate([rhs, kx_ref[pl.ds(k0, width), :]], axis=-1)
        return lax.dot_general(lhs, rhs, _NT, preferred_element_type=F32)

    def diag_chunks(r):
        width = (r + 1) * rs
        s = raw_logits(r, q0, width)
        kbv = kbias(qi)
        lo = r * rs
        out = []
        for c in range(width // LANES):
            sc = chunk(s, kbv, c)
            if (c + 1) * LANES - 1 > lo:
                row = lo + lax.broadcasted_iota(jnp.int32, (rs, LANES), 0)
                col = c * LANES + lax.broadcasted_iota(jnp.int32, (rs, LANES), 1)
                sc = jnp.where(col <= row, sc, MASKED)
            out.append(sc)
        return out

    def past_chunks(s, kbv):
        return [chunk(s, kbv, c) for c in range(nch)]

    qf = q_ref[0].astype(F32)
    qk_bound = jnp.sqrt(jnp.sum(qf * qf, axis=-1, keepdims=True) * kn2_ref[...])
    if bias is not None:
        kb_d = kb_ref[0, 0, qi]
        row = lax.broadcasted_iota(jnp.int32, (LANES, LANES), 0)
        col = lax.broadcasted_iota(jnp.int32, (LANES, LANES), 1)
        selfs, owns = [], []
        before = jnp.full((1, 1), -jnp.inf, F32)
        for c in range(nch):
            kbc = kb_d[:, c * LANES:(c + 1) * LANES]
            selfs.append(jnp.sum(jnp.where(col == row, kbc, 0.0), axis=-1, keepdims=True))
            owns.append(jnp.maximum(jnp.max(jnp.where(col <= row, kbc, -jnp.inf), axis=-1, keepdims=True), before))
            before = jnp.maximum(before, jnp.max(kbc, axis=-1, keepdims=True))
        kb_self = jnp.concatenate(selfs, axis=0)
        kb_own = jnp.concatenate(owns, axis=0)
        self_logit = jnp.sum(qf * k_ref[0, pl.ds(q0, t), :].astype(F32), axis=-1, keepdims=True) + (kb_self - shift)
        m_min = jnp.min(self_logit, axis=0, keepdims=True)
        thresh = jnp.max(m_min - SKIP_LOG2 - jnp.max(qk_bound, axis=0, keepdims=True) + shift)
        j_start = lax.while_loop(lambda j: jnp.logical_and(j < qi, kbmax_ref[kb_base + j] < thresh),
                                 lambda j: j + 1, jnp.int32(0))
        kb_prev = lax.fori_loop(0, qi, lambda j, m: jnp.maximum(m, kbmax_ref[kb_base + j]), jnp.float32(-jnp.inf))
        m_bound = qk_bound + (jnp.maximum(kb_own, kb_prev) - shift)
    else:
        j_start = 0
        m_bound = qk_bound

    def exact_row_maxima():
        for r in range(nsub):
            m_ref[rows(r), :] = functools.reduce(jnp.maximum, diag_chunks(r))

        def tile(j, carry):
            k0 = pl.multiple_of(j * t, t)
            kbv = kbias(j)
            for r in range(nsub):
                m_ref[rows(r), :] = functools.reduce(jnp.maximum, past_chunks(raw_logits(r, k0, t), kbv),
                                                     m_ref[rows(r), :])
            return carry

        lax.fori_loop(j_start, qi, tile, 0)
        mb_ref[...] = jnp.broadcast_to(jnp.max(m_ref[...], axis=-1, keepdims=True), (t, LANES))

    def probs(chunks, r):
        mb = mb_ref[rows(r), :]
        ps = [jnp.exp2(sc - mb) for sc in chunks]
        p = ps[0] if len(ps) == 1 else jnp.concatenate(ps, axis=-1)
        return functools.reduce(jnp.add, ps), p.astype(BF16)

    def accumulate():
        def logits_into(slot, j):
            k0 = pl.multiple_of(j * t, t)
            for r in range(nsub):
                s_ref[slot, rows(r), :] = raw_logits(r, k0, t)

        def consume(slot, j):
            vt = v_ref[0, pl.ds(pl.multiple_of(j * t, t), t), :]
            kbv = kbias(j)
            for r in range(nsub):
                lsum, p = probs(past_chunks(s_ref[slot, rows(r), :], kbv), r)
                l_ref[rows(r), :] += lsum
                acc_ref[rows(r), :] += jnp.dot(p, vt, preferred_element_type=F32)

        def last_past(j):
            return jnp.minimum(j, jnp.maximum(qi - 1, 0))

        logits_into(0, last_past(j_start))
        for r in range(nsub):
            lsum, p = probs(diag_chunks(r), r)
            l_ref[rows(r), :] = lsum
            acc_ref[rows(r), :] = jnp.dot(p, v_ref[0, pl.ds(q0, (r + 1) * rs), :], preferred_element_type=F32)

        n_past = qi - j_start

        def trip(j, tiles):
            for i in range(tiles):
                logits_into((i + 1) % 2, last_past(j + i + 1))
                consume(i % 2, j + i)

        def quad(i, carry):
            trip(j_start + 4 * i, 4)
            return carry

        lax.fori_loop(0, n_past // 4, quad, 0)

        @pl.when(n_past % 4 >= 2)
        def _():
            trip(j_start + (n_past // 4) * 4, 2)

        @pl.when(n_past % 2 == 1)
        def _():
            consume(0, qi - 1)

        l = jnp.sum(l_ref[...], axis=-1, keepdims=True)
        acc_ref[...] = acc_ref[...] / l
        return jnp.min(l)

    mb_ref[...] = jnp.broadcast_to(m_bound, (t, LANES))
    l_min = accumulate()

    @pl.when(l_min < MIN_ROW_SUM)
    def _():
        exact_row_maxima()
        accumulate()


def _att_scratch(t, dv, streams=1):
    acc = (t, dv) if streams == 1 else (streams, t, dv)
    return [pltpu.VMEM((t, LANES), F32), pltpu.VMEM((t, LANES), F32), pltpu.VMEM((t, LANES), F32),
            pltpu.VMEM((2, t, t), F32), pltpu.VMEM(acc, F32)] + [pltpu.VMEM((1, 1), F32)] * streams


def _tile_bias(kbias, heads, nq, t):
    kb = kbias.reshape(kbias.shape[0], heads, nq, 1, t)
    return kb, jnp.max(kb, axis=(3, 4)).reshape(-1)


def _flash_kernel(*refs, t, nq, heads, kb_batched, has_bias):
    bi, h, qi = pl.program_id(0), pl.program_id(1), pl.program_id(2)
    if not has_bias:
        q_ref, k_ref, v_ref, o_ref, m_ref, mb_ref, l_ref, s_ref, acc_ref, kn2_ref = refs
        bias = None
    else:
        q_ref, k_ref, v_ref, kb_ref, kbmax_ref, o_ref, m_ref, mb_ref, l_ref, s_ref, acc_ref, kn2_ref = refs
        bias = (kb_ref, kbmax_ref, ((bi * heads if kb_batched else 0) + h) * nq)

    @pl.when(qi == 0)
    def _():
        _key_norm2(k_ref, kn2_ref, t, nq)

    _attend(q_ref, k_ref, v_ref, kn2_ref, bias, qi, t=t, m_ref=m_ref, mb_ref=mb_ref, l_ref=l_ref, s_ref=s_ref,
            acc_ref=acc_ref)
    o_ref[0] = acc_ref[...].astype(o_ref.dtype)


def _flash(q_arr, k_arr, v_arr, kbias, *, heads, dk, dv, q_blk, k_blk, v_blk, name):
    b, s, _ = q_arr.shape
    t = min(ATT_TILE, s)
    nq = s // t
    in_specs = [
        pl.BlockSpec((1, t, dk), lambda bi, h, i: (bi, i, q_blk + h)),
        pl.BlockSpec((1, s, dk), lambda bi, h, i: (bi, 0, k_blk + h)),
        pl.BlockSpec((1, s, dv), lambda bi, h, i: (bi, 0, v_blk + h)),
    ]
    args = [q_arr, k_arr, v_arr]
    kb_batched = kbias is not None and kbias.shape[0] == b
    if kbias is not None:
        kb, kbmax = _tile_bias(kbias, heads, nq, t)
        kb_b = (lambda bi: bi) if kb_batched else (lambda bi: 0)
        in_specs += [pl.BlockSpec((1, 1, nq, 1, t), lambda bi, h, i: (kb_b(bi), h, 0, 0, 0)),
                     pl.BlockSpec(memory_space=pltpu.SMEM)]
        args += [kb, kbmax]
    return pl.pallas_call(
        functools.partial(_flash_kernel, t=t, nq=nq, heads=heads, kb_batched=kb_batched,
                          has_bias=kbias is not None),
        grid=(b, heads, nq),
        in_specs=in_specs,
        out_specs=pl.BlockSpec((1, t, dv), lambda bi, h, i: (bi, i, h)),
        out_shape=jax.ShapeDtypeStruct((b, s, heads * dv), BF16),
        scratch_shapes=_att_scratch(t, dv),
        compiler_params=_params(("parallel", "parallel", "arbitrary")),
        name=name,
    )(*args)


def _diff_kernel(q1_ref, q2_ref, k1_ref, k2_ref, v_ref, kb_ref, kbmax_ref, lam_ref, g_ref, o_ref,
                 m_ref, mb_ref, l_ref, s_ref, acc_ref, kn2a_ref, kn2b_ref, *, t, nq, lam_init):
    h, qi = pl.program_id(1), pl.program_id(2)

    @pl.when(qi == 0)
    def _():
        _key_norm2(k1_ref, kn2a_ref, t, nq)
        _key_norm2(k2_ref, kn2b_ref, t, nq)

    scratch = dict(m_ref=m_ref, mb_ref=mb_ref, l_ref=l_ref, s_ref=s_ref)
    bias = (kb_ref, kbmax_ref, h * nq)
    _attend(q1_ref, k1_ref, v_ref, kn2a_ref, bias, qi, t=t, acc_ref=acc_ref.at[0], **scratch)
    _attend(q2_ref, k2_ref, v_ref, kn2b_ref, bias, qi, t=t, acc_ref=acc_ref.at[1], **scratch)
    lam = lam_ref[...]
    lam_full = (jnp.exp(jnp.sum(lam[0:1] * lam[1:2], axis=-1, keepdims=True))
                - jnp.exp(jnp.sum(lam[2:3] * lam[3:4], axis=-1, keepdims=True)) + lam_init)
    d = acc_ref[0] - lam_full * acc_ref[1]
    o_ref[0] = (_rms(d, g_ref[...]) * (1.0 - lam_init)).astype(o_ref.dtype)


def _diff_attention(proj, kbias, lam, subln, *, q_blk, k_blk, v_blk, lam_init):
    b, s, _ = proj.shape
    t = min(ATT_TILE, s)
    nq = s // t
    dv = 2 * HEAD_DIM
    kb, kbmax = _tile_bias(kbias, DIFF_HEADS, nq, t)
    return pl.pallas_call(
        functools.partial(_diff_kernel, t=t, nq=nq, lam_init=lam_init),
        grid=(b, DIFF_HEADS, nq),
        in_specs=[
            pl.BlockSpec((1, t, HEAD_DIM), lambda bi, h, i: (bi, i, q_blk + 2 * h)),
            pl.BlockSpec((1, t, HEAD_DIM), lambda bi, h, i: (bi, i, q_blk + 2 * h + 1)),
            pl.BlockSpec((1, s, HEAD_DIM), lambda bi, h, i: (bi, 0, k_blk + 2 * h)),
            pl.BlockSpec((1, s, HEAD_DIM), lambda bi, h, i: (bi, 0, k_blk + 2 * h + 1)),
            pl.BlockSpec((1, s, dv), lambda bi, h, i: (bi, 0, v_blk + h)),
            pl.BlockSpec((1, 1, nq, 1, t), lambda bi, h, i: (0, h, 0, 0, 0)),
            pl.BlockSpec(memory_space=pltpu.SMEM),
            pl.BlockSpec((4, HEAD_DIM), lambda bi, h, i: (0, 0)),
            pl.BlockSpec((1, dv), lambda bi, h, i: (0, 0)),
        ],
        out_specs=pl.BlockSpec((1, t, dv), lambda bi, h, i: (bi, i, h)),
        out_shape=jax.ShapeDtypeStruct((b, s, DIFF_HEADS * dv), BF16),
        scratch_shapes=_att_scratch(t, dv, streams=2),
        compiler_params=_params(("parallel", "parallel", "arbitrary")),
        name="diff_attention",
    )(proj, proj, proj, proj, proj, kb, kbmax, lam, subln.reshape(1, dv))


def _mla_up_kernel(cq_ref, ckv_ref, kpe_ref, gq_ref, gkv_ref, wq_ref, wkv_ref, cos_ref, sin_ref,
                   qf_ref, kf_ref, v_ref, *, scale):
    cqn = _rms(cq_ref[...].astype(F32), gq_ref[...]).astype(BF16)
    ckvn = _rms(ckv_ref[...].astype(F32), gkv_ref[...]).astype(BF16)
    cos = cos_ref[...]
    sin = sin_ref[...]
    kpe = kpe_ref[...]
    kv = jnp.dot(ckvn, wkv_ref[...], preferred_element_type=F32)
    nk = MLA_HEADS * MLA_NOPE
    v_ref[...] = kv[:, nk:].astype(v_ref.dtype)
    for h in range(MLA_HEADS):
        q3 = jnp.dot(cqn, wq_ref[:, h * 3 * LANES:(h + 1) * 3 * LANES], preferred_element_type=F32)
        rot = q3[:, LANES:2 * LANES] * cos + q3[:, 2 * LANES:] * sin
        qf_ref[:, 2 * h * LANES:(2 * h + 1) * LANES] = (q3[:, :LANES] * scale).astype(qf_ref.dtype)
        qf_ref[:, (2 * h + 1) * LANES:(2 * h + 2) * LANES] = (rot * scale).astype(qf_ref.dtype)
        kf_ref[:, 2 * h * LANES:(2 * h + 1) * LANES] = kv[:, h * LANES:(h + 1) * LANES].astype(kf_ref.dtype)
        kf_ref[:, (2 * h + 1) * LANES:(2 * h + 2) * LANES] = kpe


def _mla_up(proj, kpe_r, gq, gkv, wq, wkv, cos_t, sin_t, *, tm=512):
    t = proj.shape[0]
    tm = min(tm, t)
    nrep = cos_t.shape[0] // tm
    scale = float((MLA_NOPE + MLA_ROPE) ** -0.5 * LOG2E)
    wide = 2 * LANES * MLA_HEADS
    row = lambda i: (i, 0)
    fixed = lambda i: (0, 0)
    return pl.pallas_call(
        functools.partial(_mla_up_kernel, scale=scale),
        grid=(t // tm,),
        in_specs=[
            pl.BlockSpec((tm, MLA_RANK), lambda i: (i, 0)),
            pl.BlockSpec((tm, MLA_RANK), lambda i: (i, 1)),
            pl.BlockSpec((tm, LANES), row),
            pl.BlockSpec((1, MLA_RANK), fixed),
            pl.BlockSpec((1, MLA_RANK), fixed),
            pl.BlockSpec(wq.shape, fixed),
            pl.BlockSpec(wkv.shape, fixed),
            pl.BlockSpec((tm, LANES), lambda i: (i % nrep, 0)),
            pl.BlockSpec((tm, LANES), lambda i: (i % nrep, 0)),
        ],
        out_specs=[pl.BlockSpec((tm, wide), row), pl.BlockSpec((tm, wide), row),
                   pl.BlockSpec((tm, MLA_HEADS * MLA_V), row)],
        out_shape=[jax.ShapeDtypeStruct((t, wide), BF16), jax.ShapeDtypeStruct((t, wide), BF16),
                   jax.ShapeDtypeStruct((t, MLA_HEADS * MLA_V), BF16)],
        compiler_params=_params(("parallel",)),
        name="mla_up",
    )(proj, proj, kpe_r, gq.reshape(1, -1), gkv.reshape(1, -1), wq, wkv, cos_t, sin_t)


def _kmean_kernel(k_ref, o_ref):
    j = pl.program_id(1)
    o_ref[0, pl.ds(j, 1), :] = jnp.mean(k_ref[0].astype(F32), axis=0, keepdims=True)


def _kmean(proj, *, k_blk_wide):
    b, s, _ = proj.shape
    nblk = s // MOBA_BLOCK
    w = MOBA_HEADS * HEAD_DIM
    return pl.pallas_call(
        _kmean_kernel,
        grid=(b, nblk),
        in_specs=[pl.BlockSpec((1, MOBA_BLOCK, w), lambda bi, j: (bi, j, k_blk_wide))],
        out_specs=pl.BlockSpec((1, nblk, w), lambda bi, j: (bi, 0, 0)),
        out_shape=jax.ShapeDtypeStruct((b, nblk, w), F32),
        compiler_params=_params(("parallel", "arbitrary")),
        name="moba_kmean",
    )(proj)


def _moba_kernel(q_ref, k_ref, v_ref, km_ref, kx_ref, kb_ref, kbmax_ref, o_ref,
                 m_ref, mb_ref, l_ref, s_ref, acc_ref, kn2_ref, qx_ref, *, t, nq, nblk):
    h, qi = pl.program_id(1), pl.program_id(2)
    bpt = t // MOBA_BLOCK

    @pl.when(qi == 0)
    def _():
        _key_norm2(k_ref, kn2_ref, t, nq)

    km = km_ref[0]
    if nblk < LANES:
        km = jnp.concatenate([km, jnp.zeros((LANES - nblk, HEAD_DIM), F32)], axis=0)
    gate = lax.dot_general(q_ref[0].astype(F32), km, _NT, preferred_element_type=F32,
                           precision=lax.Precision.HIGHEST)
    lane = lax.broadcasted_iota(jnp.int32, (t, LANES), 1)
    row = lax.broadcasted_iota(jnp.int32, (t, LANES), 0)
    own = bpt * qi + jnp.right_shift(row, int(math.log2(MOBA_BLOCK)))
    past = lane < own
    neg_inf = -jnp.inf
    g0 = jnp.where(past, gate, neg_inf)
    g = g0
    kth = jnp.max(g, axis=-1, keepdims=True)
    for _ in range(MOBA_TOPK - 1):
        g = jnp.where(g >= kth, neg_inf, g)
        kth = jnp.max(g, axis=-1, keepdims=True)
    qx_ref[...] = jnp.where(past & (g0 < kth), BLOCK_OFF, 0.0).astype(BF16)

    _attend(q_ref, k_ref, v_ref, kn2_ref, (kb_ref, kbmax_ref, h * nq), qi, t=t,
            m_ref=m_ref, mb_ref=mb_ref, l_ref=l_ref, s_ref=s_ref, acc_ref=acc_ref, extra=(qx_ref, kx_ref))
    o_ref[0] = acc_ref[...].astype(o_ref.dtype)


def _moba(proj, kmean, kbias, *, q_blk, k_blk, v_blk):
    b, s, _ = proj.shape
    t = min(ATT_TILE, s)
    nq = s // t
    nblk = s // MOBA_BLOCK
    kb, kbmax = _tile_bias(kbias, MOBA_HEADS, nq, t)
    key_block = (jnp.arange(s)[:, None] // MOBA_BLOCK == jnp.arange(LANES)[None, :]).astype(BF16)
    return pl.pallas_call(
        functools.partial(_moba_kernel, t=t, nq=nq, nblk=nblk),
        grid=(b, MOBA_HEADS, nq),
        in_specs=[
            pl.BlockSpec((1, t, HEAD_DIM), lambda bi, h, i: (bi, i, q_blk + h)),
            pl.BlockSpec((1, s, HEAD_DIM), lambda bi, h, i: (bi, 0, k_blk + h)),
            pl.BlockSpec((1, s, HEAD_DIM), lambda bi, h, i: (bi, 0, v_blk + h)),
            pl.BlockSpec((1, nblk, HEAD_DIM), lambda bi, h, i: (bi, 0, h)),
            pl.BlockSpec((s, LANES), lambda bi, h, i: (0, 0)),
            pl.BlockSpec((1, 1, nq, 1, t), lambda bi, h, i: (0, h, 0, 0, 0)),
            pl.BlockSpec(memory_space=pltpu.SMEM),
        ],
        out_specs=pl.BlockSpec((1, t, HEAD_DIM), lambda bi, h, i: (bi, i, h)),
        out_shape=jax.ShapeDtypeStruct((b, s, MOBA_HEADS * HEAD_DIM), BF16),
        scratch_shapes=_att_scratch(t, HEAD_DIM) + [pltpu.VMEM((t, LANES), BF16)],
        compiler_params=_params(("parallel", "parallel", "arbitrary")),
        name="moba_attention",
    )(proj, proj, proj, kmean, key_block, kb, kbmax)


def _outproj_kernel(a_ref, d_ref, wa_ref, wd_ref, g_ref, h_ref, o_ref):
    y = jnp.dot(a_ref[...], wa_ref[...], preferred_element_type=F32)
    y = y + jnp.dot(d_ref[...], wd_ref[...], preferred_element_type=F32)
    o_ref[...] = h_ref[...] + _rms(y, g_ref[...])


def _outproj(a, d, wa, wd, g, h, *, tm=512):
    t, dm = h.shape
    tm = min(tm, t)
    ka, kd = a.shape[1], d.shape[1]
    return pl.pallas_call(
        _outproj_kernel,
        grid=(t // tm,),
        in_specs=[pl.BlockSpec((tm, ka), lambda i: (i, 0)),
                  pl.BlockSpec((tm, kd), lambda i: (i, 0)),
                  pl.BlockSpec((ka, dm), lambda i: (0, 0)),
                  pl.BlockSpec((kd, dm), lambda i: (0, 0)),
                  pl.BlockSpec((1, dm), lambda i: (0, 0)),
                  pl.BlockSpec((tm, dm), lambda i: (i, 0))],
        out_specs=pl.BlockSpec((tm, dm), lambda i: (i, 0)),
        out_shape=jax.ShapeDtypeStruct((t, dm), F32),
        compiler_params=_params(("parallel",)),
        name="mixer_outproj",
    )(a, d, wa, wd, g.reshape(1, dm), h)


def _xattn_kernel(h_ref, g2_ref, wq_ref, mkv_ref, wo_ref, g3_ref, o_ref, *, scale):
    h = h_ref[0]
    xn = _rms(h, g2_ref[...]).astype(BF16)
    q = (jnp.dot(xn, wq_ref[...], preferred_element_type=F32) * scale).astype(BF16)
    w = XATTN_HEADS * HEAD_DIM
    outs = []
    for hd in range(XATTN_HEADS):
        mk = mkv_ref[0, :, hd * HEAD_DIM:(hd + 1) * HEAD_DIM]
        mv = mkv_ref[0, :, w + hd * HEAD_DIM:w + (hd + 1) * HEAD_DIM]
        s = lax.dot_general(q[:, hd * HEAD_DIM:(hd + 1) * HEAD_DIM], mk, _NT, preferred_element_type=F32)
        p = jnp.exp(s - jnp.max(s, axis=-1, keepdims=True))
        p = p / jnp.sum(p, axis=-1, keepdims=True)
        outs.append(jnp.dot(p.astype(BF16), mv, preferred_element_type=F32).astype(BF16))
    y = jnp.dot(jnp.concatenate(outs, axis=-1), wo_ref[...], preferred_element_type=F32)
    o_ref[0] = h + _rms(y, g3_ref[...])


def _xattn(h, g2, wq, mkv, wo, g3, *, tm=512):
    b, s, dm = h.shape
    tm = min(tm, s)
    fixed = lambda bi, i: (0, 0)
    return pl.pallas_call(
        functools.partial(_xattn_kernel, scale=float(HEAD_DIM ** -0.5)),
        grid=(b, s // tm),
        in_specs=[pl.BlockSpec((1, tm, dm), lambda bi, i: (bi, i, 0)),
                  pl.BlockSpec((1, dm), fixed),
                  pl.BlockSpec(wq.shape, fixed),
                  pl.BlockSpec((1,) + mkv.shape[1:], lambda bi, i: (bi, 0, 0)),
                  pl.BlockSpec(wo.shape, fixed),
                  pl.BlockSpec((1, dm), fixed)],
        out_specs=pl.BlockSpec((1, tm, dm), lambda bi, i: (bi, i, 0)),
        out_shape=jax.ShapeDtypeStruct((b, s, dm), F32),
        compiler_params=_params(("parallel", "parallel")),
        name="memory_xattn",
    )(h, g2.reshape(1, dm), wq, mkv, wo, g3.reshape(1, dm))


def _ffn_kernel(h_ref, g4_ref, w1_ref, w2_ref, g5_ref, o_ref, xn_ref, acc_ref):
    f = pl.program_id(1)

    @pl.when(f == 0)
    def _():
        xn_ref[...] = _rms(h_ref[...], g4_ref[...]).astype(BF16)
        acc_ref[...] = jnp.zeros_like(acc_ref)

    u = jnp.maximum(jnp.dot(xn_ref[...], w1_ref[...], preferred_element_type=F32), 0.0)
    acc_ref[...] += jnp.dot((u * u).astype(BF16), w2_ref[...], preferred_element_type=F32)

    @pl.when(f == pl.num_programs(1) - 1)
    def _():
        o_ref[...] = h_ref[...] + _rms(acc_ref[...], g5_ref[...])


def _ffn(h, g4, w1, w2, g5, *, tm=512, tf=1024):
    t, dm = h.shape
    tm = min(tm, t)
    f = w1.shape[1]
    return pl.pallas_call(
        _ffn_kernel,
        grid=(t // tm, f // tf),
        in_specs=[pl.BlockSpec((tm, dm), lambda i, j: (i, 0)),
                  pl.BlockSpec((1, dm), lambda i, j: (0, 0)),
                  pl.BlockSpec((dm, tf), lambda i, j: (0, j)),
                  pl.BlockSpec((tf, dm), lambda i, j: (j, 0)),
                  pl.BlockSpec((1, dm), lambda i, j: (0, 0))],
        out_specs=pl.BlockSpec((tm, dm), lambda i, j: (i, 0)),
        out_shape=jax.ShapeDtypeStruct((t, dm), F32),
        scratch_shapes=[pltpu.VMEM((tm, dm), BF16), pltpu.VMEM((tm, dm), F32)],
        compiler_params=_params(("parallel", "arbitrary")),
        name="relu2_mlp",
    )(h, g4.reshape(1, dm), w1, w2, g5.reshape(1, dm))


def _alibi_key_bias(n_heads, seq):
    slopes = jnp.asarray([2.0 ** (-8.0 * (i + 1) / n_heads) for i in range(n_heads)], dtype=F32)
    return (LOG2E * slopes[:, None] * jnp.arange(seq, dtype=F32)[None, :])[None]


def _rope_tables(seq):
    half = MLA_ROPE // 2
    inv = ROPE_THETA ** (-jnp.arange(0, MLA_ROPE, 2, dtype=F32) / MLA_ROPE)
    ang = jnp.arange(seq, dtype=F32)[:, None] * inv[None, :]
    cos, sin = jnp.cos(ang), jnp.sin(ang)
    zero = jnp.zeros((seq, LANES - 2 * half), F32)
    return jnp.concatenate([cos, cos, zero], axis=-1), jnp.concatenate([-sin, sin, zero], axis=-1)


def _rope_pair_columns(w_t1, w_t2):
    zero = jnp.zeros((w_t1.shape[0], LANES - 2 * w_t1.shape[1]), w_t1.dtype)
    return jnp.concatenate([w_t1, w_t2, zero, w_t2, w_t1, zero], axis=-1)


def kernel(x, mem, mem_norm, mem_wkv, norms, xattn_wq, xattn_wo, ffn_w1, ffn_w2, ab_w_in, ab_w_out, fox_b_f, diff_lambda, diff_subln, cd_w_in, cd_w_out, mla_q_norm, mla_kv_norm, mla_w_uq, mla_w_ukv):
    b, s, dm = x.shape
    t = b * s
    depth = norms.shape[0]
    q_scale = HEAD_DIM ** -0.5 * LOG2E
    half = MLA_ROPE // 2
    cos_t, sin_t = _rope_tables(s)

    mkv = _norm_matmul(mem.reshape(-1, dm), mem_norm, mem_wkv.astype(BF16)).reshape(b, mem.shape[1], -1)

    h = x.reshape(t, dm)
    for i in range(depth):
        n = norms[i]
        j = i // 2
        if i % 2 == 0:
            w = ab_w_in[j]
            fw = FOX_HEADS * HEAD_DIM
            dw = DIFF_HEADS * 2 * HEAD_DIM
            o_g = 3 * fw
            o_dq = o_g + FOX_HEADS
            w_main = jnp.concatenate([w[:, :fw] * q_scale, w[:, fw:o_g],
                                      w[:, o_dq:o_dq + dw] * q_scale, w[:, o_dq + dw:]], axis=-1).astype(BF16)
            w_gate = jnp.concatenate([w[:, o_g:o_dq].T, jnp.zeros((8, dm), F32)], axis=0).astype(BF16)
            proj, gate_t = _inproj(h, n[0], w_main, w_gate, "gate_t")
            proj = proj.reshape(b, s, -1)
            fox_kb = _fox_bias(gate_t, fox_b_f[j], b, s)
            a = _flash(proj, proj, proj, fox_kb, heads=FOX_HEADS, dk=HEAD_DIM, dv=HEAD_DIM,
                       q_blk=0, k_blk=FOX_HEADS, v_blk=2 * FOX_HEADS, name="fox_attention")
            lam_init = 0.8 - 0.6 * math.exp(-0.3 * i)
            d = _diff_attention(proj, _alibi_key_bias(DIFF_HEADS, s), diff_lambda[j], diff_subln[j],
                                q_blk=3 * FOX_HEADS, k_blk=3 * FOX_HEADS + 2 * DIFF_HEADS,
                                v_blk=(3 * fw + 2 * dw) // (2 * HEAD_DIM), lam_init=lam_init)
            wo = ab_w_out[j].astype(BF16)
            h = _outproj(a.reshape(t, -1), d.reshape(t, -1), wo[:fw], wo[fw:], n[1], h)
        else:
            w = cd_w_in[j]
            r2 = 2 * MLA_RANK
            o_m = r2 + MLA_ROPE
            mw = MOBA_HEADS * HEAD_DIM
            w_main = jnp.concatenate([w[:, :r2], w[:, o_m:o_m + mw] * q_scale, w[:, o_m + mw:]],
                                     axis=-1).astype(BF16)
            w_kpe = _rope_pair_columns(w[:, r2:r2 + half], w[:, r2 + half:o_m]).astype(BF16)
            proj, kpe_r = _inproj(h, n[0], w_main, w_kpe, "rope", (cos_t, sin_t))
            wq = mla_w_uq[j].reshape(MLA_RANK, MLA_HEADS, MLA_NOPE + MLA_ROPE)
            wq = jnp.concatenate(
                [jnp.concatenate([wq[:, hd, :MLA_NOPE],
                                  _rope_pair_columns(wq[:, hd, MLA_NOPE:MLA_NOPE + half], wq[:, hd, MLA_NOPE + half:])],
                                 axis=-1) for hd in range(MLA_HEADS)], axis=-1).astype(BF16)
            wkv = mla_w_ukv[j].reshape(MLA_RANK, MLA_HEADS, MLA_NOPE + MLA_V)
            wkv = jnp.concatenate([wkv[:, :, :MLA_NOPE].reshape(MLA_RANK, -1),
                                   wkv[:, :, MLA_NOPE:].reshape(MLA_RANK, -1)], axis=-1).astype(BF16)
            qf, kf, v = _mla_up(proj, kpe_r, mla_q_norm[j], mla_kv_norm[j], wq, wkv, cos_t, sin_t)
            c = _flash(qf.reshape(b, s, -1), kf.reshape(b, s, -1), v.reshape(b, s, -1), None,
                       heads=MLA_HEADS, dk=2 * LANES, dv=MLA_V, q_blk=0, k_blk=0, v_blk=0, name="mla_attention")
            proj = proj.reshape(b, s, -1)
            kmean = _kmean(proj, k_blk_wide=(r2 + mw) // mw)
            dout = _moba(proj, kmean, _alibi_key_bias(MOBA_HEADS, s),
                         q_blk=r2 // HEAD_DIM, k_blk=(r2 + mw) // HEAD_DIM, v_blk=(r2 + 2 * mw) // HEAD_DIM)
            wo = cd_w_out[j].astype(BF16)
            cw = MLA_HEADS * MLA_V
            h = _outproj(c.reshape(t, -1), dout.reshape(t, -1), wo[:cw], wo[cw:], n[1], h)
        h = _xattn(h.reshape(b, s, dm), n[2], xattn_wq[i].astype(BF16), mkv, xattn_wo[i].astype(BF16), n[3])
        h = _ffn(h.reshape(t, dm), n[4], ffn_w1[i].astype(BF16), ffn_w2[i].astype(BF16), n[5])
    return h.reshape(b, s, dm)
```

```python
import functools
import math

import jax
import jax.numpy as jnp
from jax import lax
from jax.experimental import pallas as pl
from jax.experimental.pallas import tpu as pltpu

F32 = jnp.float32
BF16 = jnp.bfloat16

NORM_EPS = 1e-6
HEAD_DIM = 128
FOX_HEADS = 8
DIFF_HEADS = 4
MLA_HEADS = 8
MLA_NOPE = 128
MLA_ROPE = 64
MLA_V = 128
MLA_RANK = 512
ROPE_THETA = 10000.0
MOBA_HEADS = 8
MOBA_BLOCK = 256
MOBA_TOPK = 3
XATTN_HEADS = 4
LANES = 128
MASKED = -1e30
VMEM_LIMIT = 48 * 1024 * 1024
LOG2E = math.log2(math.e)
SKIP_LOG2 = 160.0
MIN_ROW_SUM = 2.0 ** -64
BLOCK_OFF = -2.0 ** 100
ATT_TILE = 512
ATT_ROWS = 256

_NT = (((1,), (1,)), ((), ()))


def _params(sem):
    return pltpu.CompilerParams(dimension_semantics=sem, vmem_limit_bytes=VMEM_LIMIT)


def _rms(x, g):
    ms = jnp.mean(x * x, axis=-1, keepdims=True)
    return x * lax.rsqrt(ms + NORM_EPS) * g


def _inproj_kernel(x_ref, g_ref, w_ref, aux_w_ref, *rest, aux_mode):
    if aux_mode == "rope":
        cos_ref, sin_ref, o_ref, aux_ref, xn_ref = rest
    else:
        o_ref, aux_ref, xn_ref = rest

    @pl.when(pl.program_id(1) == 0)
    def _():
        xn = _rms(x_ref[...], g_ref[...]).astype(BF16)
        xn_ref[...] = xn
        if aux_mode == "gate_t":
            r = lax.dot_general(aux_w_ref[...], xn, _NT, preferred_element_type=F32)
            aux_ref[...] = r[:8]
        else:
            ab = jnp.dot(xn, aux_w_ref[...], preferred_element_type=F32)
            aux_ref[...] = (ab[:, :LANES] * cos_ref[...] + ab[:, LANES:] * sin_ref[...]).astype(aux_ref.dtype)

    o_ref[...] = jnp.dot(xn_ref[...], w_ref[...], preferred_element_type=F32).astype(o_ref.dtype)


def _inproj(x, g, w, aux_w, aux_mode, tables=None, *, tm=1024, tn=1024):
    t, k = x.shape
    n = w.shape[1]
    tm = min(tm, t)
    grid = (t // tm, n // tn)
    in_specs = [
        pl.BlockSpec((tm, k), lambda i, j: (i, 0)),
        pl.BlockSpec((1, k), lambda i, j: (0, 0)),
        pl.BlockSpec((k, tn), lambda i, j: (0, j)),
        pl.BlockSpec(aux_w.shape, lambda i, j: (0, 0)),
    ]
    args = [x, g.reshape(1, k), w, aux_w]
    if aux_mode == "rope":
        cos_t, sin_t = tables
        nrep = cos_t.shape[0] // tm
        in_specs += [pl.BlockSpec((tm, LANES), lambda i, j: (i % nrep, 0))] * 2
        args += [cos_t, sin_t]
        aux_shape = jax.ShapeDtypeStruct((t, LANES), BF16)
        aux_spec = pl.BlockSpec((tm, LANES), lambda i, j: (i, 0))
    else:
        aux_shape = jax.ShapeDtypeStruct((8, t), F32)
        aux_spec = pl.BlockSpec((8, tm), lambda i, j: (0, i))
    return pl.pallas_call(
        functools.partial(_inproj_kernel, aux_mode=aux_mode),
        grid=grid,
        in_specs=in_specs,
        out_specs=[pl.BlockSpec((tm, tn), lambda i, j: (i, j)), aux_spec],
        out_shape=[jax.ShapeDtypeStruct((t, n), BF16), aux_shape],
        scratch_shapes=[pltpu.VMEM((tm, k), BF16)],
        compiler_params=_params(("parallel", "arbitrary")),
        name="inproj_" + aux_mode,
    )(*args)


def _norm_matmul_kernel(x_ref, g_ref, w_ref, o_ref):
    xn = _rms(x_ref[...], g_ref[...]).astype(BF16)
    o_ref[...] = jnp.dot(xn, w_ref[...], preferred_element_type=F32).astype(o_ref.dtype)


def _norm_matmul(x, g, w, *, tm=256):
    t, k = x.shape
    n = w.shape[1]
    return pl.pallas_call(
        _norm_matmul_kernel,
        grid=(t // tm,),
        in_specs=[pl.BlockSpec((tm, k), lambda i: (i, 0)),
                  pl.BlockSpec((1, k), lambda i: (0, 0)),
                  pl.BlockSpec((k, n), lambda i: (0, 0))],
        out_specs=pl.BlockSpec((tm, n), lambda i: (i, 0)),
        out_shape=jax.ShapeDtypeStruct((t, n), BF16),
        compiler_params=_params(("parallel",)),
        name="mem_kv_proj",
    )(x, g.reshape(1, k), w)


def _fox_bias_kernel(g_ref, b_ref, o_ref, carry_ref, *, tc):
    @pl.when(pl.program_id(1) == 0)
    def _():
        carry_ref[...] = jnp.zeros_like(carry_ref)

    z = g_ref[...] + b_ref[...]
    logf = jnp.minimum(z, 0.0) - jnp.log(1.0 + jnp.exp(-jnp.abs(z)))
    upper = (lax.broadcasted_iota(jnp.int32, (tc, tc), 0)
             <= lax.broadcasted_iota(jnp.int32, (tc, tc), 1)).astype(F32)
    cum = jnp.dot(logf, upper, preferred_element_type=F32, precision=lax.Precision.HIGHEST) + carry_ref[...]
    carry_ref[...] = cum[:, tc - 1:tc]
    o_ref[0] = cum * (-LOG2E)


def _fox_bias(gate_t, b_f, batch, seq, *, tc=256):
    ns = seq // tc
    return pl.pallas_call(
        functools.partial(_fox_bias_kernel, tc=tc),
        grid=(batch, ns),
        in_specs=[pl.BlockSpec((8, tc), lambda b, s: (0, b * ns + s)),
                  pl.BlockSpec((8, 1), lambda b, s: (0, 0))],
        out_specs=pl.BlockSpec((1, 8, tc), lambda b, s: (b, 0, s)),
        out_shape=jax.ShapeDtypeStruct((batch, 8, seq), F32),
        scratch_shapes=[pltpu.VMEM((8, 1), F32)],
        compiler_params=_params(("parallel", "arbitrary")),
        name="fox_gate_cumsum",
    )(gate_t, b_f.reshape(8, 1))


def _key_norm2(k_ref, kn2_ref, t, nq):
    def body(j, mx):
        kk = k_ref[0, pl.ds(pl.multiple_of(j * t, t), t), :].astype(F32)
        return jnp.maximum(mx, jnp.sum(kk * kk, axis=-1, keepdims=True))
    mx = lax.fori_loop(0, nq, body, jnp.zeros((t, 1), F32))
    kn2_ref[...] = jnp.max(mx, axis=0, keepdims=True)


def _attend(q_ref, k_ref, v_ref, kn2_ref, bias, qi, *, t, m_ref, mb_ref, l_ref, s_ref, acc_ref, extra=None):
    rs = min(ATT_ROWS, t)
    nsub, nch = t // rs, t // LANES
    q0 = pl.multiple_of(qi * t, t)

    def rows(r):
        return slice(r * rs, (r + 1) * rs)

    if bias is not None:
        kb_ref, kbmax_ref, kb_base = bias
        shift = jnp.max(kb_ref[0, 0, qi], axis=-1, keepdims=True)

        def kbias(j):
            return kb_ref[0, 0, j] - shift
    else:
        def kbias(j):
            return None

    def chunk(s, kbv, c):
        sc = s[:, c * LANES:(c + 1) * LANES]
        return sc if kbv is None else sc + kbv[:, c * LANES:(c + 1) * LANES]

    def raw_logits(r, k0, width):
        lhs = q_ref[0, pl.ds(pl.multiple_of(q0 + r * rs, rs), rs), :]
        rhs = k_ref[0, pl.ds(k0, width), :]
        if extra is not None:
            qx_ref, kx_ref = extra
            lhs = jnp.concatenate([lhs, qx_ref[rows(r), :]], axis=-1)
            rhs = jnp.concatenate([rhs, kx_ref[pl.ds(k0, width), :]], axis=-1)
        return lax.dot_general(lhs, rhs, _NT, preferred_element_type=F32)

    def diag_chunks(r):
        width = (r + 1) * rs
        s = raw_logits(r, q0, width)
        kbv = kbias(qi)
        lo = r * rs
        out = []
        for c in range(width // LANES):
            sc = chunk(s, kbv, c)
            if (c + 1) * LANES - 1 > lo:
                row = lo + lax.broadcasted_iota(jnp.int32, (rs, LANES), 0)
                col = c * LANES + lax.broadcasted_iota(jnp.int32, (rs, LANES), 1)
                sc = jnp.where(col <= row, sc, MASKED)
            out.append(sc)
        return out

    def past_chunks(s, kbv):
        return [chunk(s, kbv, c) for c in range(nch)]

    qf = q_ref[0, pl.ds(q0, t), :].astype(F32)
    qk_bound = jnp.sqrt(jnp.sum(qf * qf, axis=-1, keepdims=True) * kn2_ref[...])
    if bias is not None:
        kb_d = kb_ref[0, 0, qi]
        row = lax.broadcasted_iota(jnp.int32, (LANES, LANES), 0)
        col = lax.broadcasted_iota(jnp.int32, (LANES, LANES), 1)
        selfs, owns = [], []
        before = jnp.full((1, 1), -jnp.inf, F32)
        for c in range(nch):
            kbc = kb_d[:, c * LANES:(c + 1) * LANES]
            selfs.append(jnp.sum(jnp.where(col == row, kbc, 0.0), axis=-1, keepdims=True))
            owns.append(jnp.maximum(jnp.max(jnp.where(col <= row, kbc, -jnp.inf), axis=-1, keepdims=True), before))
            before = jnp.maximum(before, jnp.max(kbc, axis=-1, keepdims=True))
        kb_self = jnp.concatenate(selfs, axis=0)
        kb_own = jnp.concatenate(owns, axis=0)
        self_logit = jnp.sum(qf * k_ref[0, pl.ds(q0, t), :].astype(F32), axis=-1, keepdims=True) + (kb_self - shift)
        m_min = jnp.min(self_logit, axis=0, keepdims=True)
        thresh = jnp.max(m_min - SKIP_LOG2 - jnp.max(qk_bound, axis=0, keepdims=True) + shift)
        j_start = lax.while_loop(lambda j: jnp.logical_and(j < qi, kbmax_ref[kb_base + j] < thresh),
                                 lambda j: j + 1, jnp.int32(0))
        kb_prev = lax.fori_loop(0, qi, lambda j, m: jnp.maximum(m, kbmax_ref[kb_base + j]), jnp.float32(-jnp.inf))
        m_bound = qk_bound + (jnp.maximum(kb_own, kb_prev) - shift)
    else:
        j_start = 0
        m_bound = qk_bound

    def exact_row_maxima():
        for r in range(nsub):
            m_ref[rows(r), :] = functools.reduce(jnp.maximum, diag_chunks(r))

        def tile(j, carry):
            k0 = pl.multiple_of(j * t, t)
            kbv = kbias(j)
            for r in range(nsub):
                m_ref[rows(r), :] = functools.reduce(jnp.maximum, past_chunks(raw_logits(r, k0, t), kbv),
                                                     m_ref[rows(r), :])
            return carry

        lax.fori_loop(j_start, qi, tile, 0)
        mb_ref[...] = jnp.broadcast_to(jnp.max(m_ref[...], axis=-1, keepdims=True), (t, LANES))

    def probs(chunks, r):
        mb = mb_ref[rows(r), :]
        ps = [jnp.exp2(sc - mb) for sc in chunks]
        p = ps[0] if len(ps) == 1 else jnp.concatenate(ps, axis=-1)
        return functools.reduce(jnp.add, ps), p.astype(BF16)

    def accumulate():
        def logits_into(slot, j):
            k0 = pl.multiple_of(j * t, t)
            for r in range(nsub):
                s_ref[slot, rows(r), :] = raw_logits(r, k0, t)

        def consume(slot, j):
            vt = v_ref[0, pl.ds(pl.multiple_of(j * t, t), t), :]
            kbv = kbias(j)
            for r in range(nsub):
                lsum, p = probs(past_chunks(s_ref[slot, rows(r), :], kbv), r)
                l_ref[rows(r), :] += lsum
                acc_ref[rows(r), :] += jnp.dot(p, vt, preferred_element_type=F32)

        def last_past(j):
            return jnp.minimum(j, jnp.maximum(qi - 1, 0))

        logits_into(0, last_past(j_start))
        for r in range(nsub):
            lsum, p = probs(diag_chunks(r), r)
            l_ref[rows(r), :] = lsum
            acc_ref[rows(r), :] = jnp.dot(p, v_ref[0, pl.ds(q0, (r + 1) * rs), :], preferred_element_type=F32)

        n_past = qi - j_start

        def trip(j, tiles):
            for i in range(tiles):
                logits_into((i + 1) % 2, last_past(j + i + 1))
                consume(i % 2, j + i)

        def quad(i, carry):
            trip(j_start + 4 * i, 4)
            return carry

        lax.fori_loop(0, n_past // 4, quad, 0)

        @pl.when(n_past % 4 >= 2)
        def _():
            trip(j_start + (n_past // 4) * 4, 2)

        @pl.when(n_past % 2 == 1)
        def _():
            consume(0, qi - 1)

        l = jnp.sum(l_ref[...], axis=-1, keepdims=True)
        acc_ref[...] = acc_ref[...] / l
        return jnp.min(l)

    mb_ref[...] = jnp.broadcast_to(m_bound, (t, LANES))
    l_min = accumulate()

    @pl.when(l_min < MIN_ROW_SUM)
    def _():
        exact_row_maxima()
        accumulate()


def _att_scratch(t, dv, streams=1):
    acc = (t, dv) if streams == 1 else (streams, t, dv)
    return [pltpu.VMEM((t, LANES), F32), pltpu.VMEM((t, LANES), F32), pltpu.VMEM((t, LANES), F32),
            pltpu.VMEM((2, t, t), F32), pltpu.VMEM(acc, F32)] + [pltpu.VMEM((1, 1), F32)] * streams


def _tile_bias(kbias, heads, nq, t):
    kb = kbias.reshape(kbias.shape[0], heads, nq, 1, t)
    return kb, jnp.max(kb, axis=(3, 4)).reshape(-1)


def _flash_kernel(*refs, t, nq, heads, kb_batched, has_bias):
    bi, h = pl.program_id(0), pl.program_id(1)
    if not has_bias:
        q_ref, k_ref, v_ref, o_ref, m_ref, mb_ref, l_ref, s_ref, acc_ref, kn2_ref = refs
        bias = None
    else:
        q_ref, k_ref, v_ref, kb_ref, kbmax_ref, o_ref, m_ref, mb_ref, l_ref, s_ref, acc_ref, kn2_ref = refs
        bias = (kb_ref, kbmax_ref, ((bi * heads if kb_batched else 0) + h) * nq)

    _key_norm2(k_ref, kn2_ref, t, nq)

    def q_tile(qi, carry):
        _attend(q_ref, k_ref, v_ref, kn2_ref, bias, qi, t=t, m_ref=m_ref, mb_ref=mb_ref, l_ref=l_ref,
                s_ref=s_ref, acc_ref=acc_ref)
        o_ref[0, pl.ds(pl.multiple_of(qi * t, t), t), :] = acc_ref[...].astype(o_ref.dtype)
        return carry

    lax.fori_loop(0, nq, q_tile, 0)


def _flash(q_arr, k_arr, v_arr, kbias, *, heads, dk, dv, q_blk, k_blk, v_blk, name):
    b, s, _ = q_arr.shape
    t = min(ATT_TILE, s)
    nq = s // t
    in_specs = [
        pl.BlockSpec((1, s, dk), lambda bi, h: (bi, 0, q_blk + h)),
        pl.BlockSpec((1, s, dk), lambda bi, h: (bi, 0, k_blk + h)),
        pl.BlockSpec((1, s, dv), lambda bi, h: (bi, 0, v_blk + h)),
    ]
    args = [q_arr, k_arr, v_arr]
    kb_batched = kbias is not None and kbias.shape[0] == b
    if kbias is not None:
        kb, kbmax = _tile_bias(kbias, heads, nq, t)
        kb_b = (lambda bi: bi) if kb_batched else (lambda bi: 0)
        in_specs += [pl.BlockSpec((1, 1, nq, 1, t), lambda bi, h: (kb_b(bi), h, 0, 0, 0)),
                     pl.BlockSpec(memory_space=pltpu.SMEM)]
        args += [kb, kbmax]
    return pl.pallas_call(
        functools.partial(_flash_kernel, t=t, nq=nq, heads=heads, kb_batched=kb_batched,
                          has_bias=kbias is not None),
        grid=(b, heads),
        in_specs=in_specs,
        out_specs=pl.BlockSpec((1, s, dv), lambda bi, h: (bi, 0, h)),
        out_shape=jax.ShapeDtypeStruct((b, s, heads * dv), BF16),
        scratch_shapes=_att_scratch(t, dv),
        compiler_params=_params(("parallel", "parallel")),
        name=name,
    )(*args)


def _diff_kernel(q1_ref, q2_ref, k1_ref, k2_ref, v_ref, kb_ref, kbmax_ref, lam_ref, g_ref, o_ref,
                 m_ref, mb_ref, l_ref, s_ref, acc_ref, kn2a_ref, kn2b_ref, *, t, nq, lam_init):
    h = pl.program_id(1)
    _key_norm2(k1_ref, kn2a_ref, t, nq)
    _key_norm2(k2_ref, kn2b_ref, t, nq)
    scratch = dict(m_ref=m_ref, mb_ref=mb_ref, l_ref=l_ref, s_ref=s_ref)
    bias = (kb_ref, kbmax_ref, h * nq)
    lam = lam_ref[...]
    lam_full = (jnp.exp(jnp.sum(lam[0:1] * lam[1:2], axis=-1, keepdims=True))
                - jnp.exp(jnp.sum(lam[2:3] * lam[3:4], axis=-1, keepdims=True)) + lam_init)

    def q_tile(qi, carry):
        _attend(q1_ref, k1_ref, v_ref, kn2a_ref, bias, qi, t=t, acc_ref=acc_ref.at[0], **scratch)
        _attend(q2_ref, k2_ref, v_ref, kn2b_ref, bias, qi, t=t, acc_ref=acc_ref.at[1], **scratch)
        d = acc_ref[0] - lam_full * acc_ref[1]
        o_ref[0, pl.ds(pl.multiple_of(qi * t, t), t), :] = (
            _rms(d, g_ref[...]) * (1.0 - lam_init)).astype(o_ref.dtype)
        return carry

    lax.fori_loop(0, nq, q_tile, 0)


def _diff_attention(proj, kbias, lam, subln, *, q_blk, k_blk, v_blk, lam_init):
    b, s, _ = proj.shape
    t = min(ATT_TILE, s)
    nq = s // t
    dv = 2 * HEAD_DIM
    kb, kbmax = _tile_bias(kbias, DIFF_HEADS, nq, t)
    return pl.pallas_call(
        functools.partial(_diff_kernel, t=t, nq=nq, lam_init=lam_init),
        grid=(b, DIFF_HEADS),
        in_specs=[
            pl.BlockSpec((1, s, HEAD_DIM), lambda bi, h: (bi, 0, q_blk + 2 * h)),
            pl.BlockSpec((1, s, HEAD_DIM), lambda bi, h: (bi, 0, q_blk + 2 * h + 1)),
            pl.BlockSpec((1, s, HEAD_DIM), lambda bi, h: (bi, 0, k_blk + 2 * h)),
            pl.BlockSpec((1, s, HEAD_DIM), lambda bi, h: (bi, 0, k_blk + 2 * h + 1)),
            pl.BlockSpec((1, s, dv), lambda bi, h: (bi, 0, v_blk + h)),
            pl.BlockSpec((1, 1, nq, 1, t), lambda bi, h: (0, h, 0, 0, 0)),
            pl.BlockSpec(memory_space=pltpu.SMEM),
            pl.BlockSpec((4, HEAD_DIM), lambda bi, h: (0, 0)),
            pl.BlockSpec((1, dv), lambda bi, h: (0, 0)),
        ],
        out_specs=pl.BlockSpec((1, s, dv), lambda bi, h: (bi, 0, h)),
        out_shape=jax.ShapeDtypeStruct((b, s, DIFF_HEADS * dv), BF16),
        scratch_shapes=_att_scratch(t, dv, streams=2),
        compiler_params=_params(("parallel", "parallel")),
        name="diff_attention",
    )(proj, proj, proj, proj, proj, kb, kbmax, lam, subln.reshape(1, dv))


def _mla_up_kernel(cq_ref, ckv_ref, kpe_ref, gq_ref, gkv_ref, wq_ref, wkv_ref, cos_ref, sin_ref,
                   qf_ref, kf_ref, v_ref, *, scale):
    cqn = _rms(cq_ref[...].astype(F32), gq_ref[...]).astype(BF16)
    ckvn = _rms(ckv_ref[...].astype(F32), gkv_ref[...]).astype(BF16)
    cos = cos_ref[...]
    sin = sin_ref[...]
    kpe = kpe_ref[...]
    kv = jnp.dot(ckvn, wkv_ref[...], preferred_element_type=F32)
    nk = MLA_HEADS * MLA_NOPE
    v_ref[...] = kv[:, nk:].astype(v_ref.dtype)
    for h in range(MLA_HEADS):
        q3 = jnp.dot(cqn, wq_ref[:, h * 3 * LANES:(h + 1) * 3 * LANES], preferred_element_type=F32)
        rot = q3[:, LANES:2 * LANES] * cos + q3[:, 2 * LANES:] * sin
        qf_ref[:, 2 * h * LANES:(2 * h + 1) * LANES] = (q3[:, :LANES] * scale).astype(qf_ref.dtype)
        qf_ref[:, (2 * h + 1) * LANES:(2 * h + 2) * LANES] = (rot * scale).astype(qf_ref.dtype)
        kf_ref[:, 2 * h * LANES:(2 * h + 1) * LANES] = kv[:, h * LANES:(h + 1) * LANES].astype(kf_ref.dtype)
        kf_ref[:, (2 * h + 1) * LANES:(2 * h + 2) * LANES] = kpe


def _mla_up(proj, kpe_r, gq, gkv, wq, wkv, cos_t, sin_t, *, tm=512):
    t = proj.shape[0]
    tm = min(tm, t)
    nrep = cos_t.shape[0] // tm
    scale = float((MLA_NOPE + MLA_ROPE) ** -0.5 * LOG2E)
    wide = 2 * LANES * MLA_HEADS
    row = lambda i: (i, 0)
    fixed = lambda i: (0, 0)
    return pl.pallas_call(
        functools.partial(_mla_up_kernel, scale=scale),
        grid=(t // tm,),
        in_specs=[
            pl.BlockSpec((tm, MLA_RANK), lambda i: (i, 0)),
            pl.BlockSpec((tm, MLA_RANK), lambda i: (i, 1)),
            pl.BlockSpec((tm, LANES), row),
            pl.BlockSpec((1, MLA_RANK), fixed),
            pl.BlockSpec((1, MLA_RANK), fixed),
            pl.BlockSpec(wq.shape, fixed),
            pl.BlockSpec(wkv.shape, fixed),
            pl.BlockSpec((tm, LANES), lambda i: (i % nrep, 0)),
            pl.BlockSpec((tm, LANES), lambda i: (i % nrep, 0)),
        ],
        out_specs=[pl.BlockSpec((tm, wide), row), pl.BlockSpec((tm, wide), row),
                   pl.BlockSpec((tm, MLA_HEADS * MLA_V), row)],
        out_shape=[jax.ShapeDtypeStruct((t, wide), BF16), jax.ShapeDtypeStruct((t, wide), BF16),
                   jax.ShapeDtypeStruct((t, MLA_HEADS * MLA_V), BF16)],
        compiler_params=_params(("parallel",)),
        name="mla_up",
    )(proj, proj, kpe_r, gq.reshape(1, -1), gkv.reshape(1, -1), wq, wkv, cos_t, sin_t)


def _kmean_kernel(k_ref, o_ref):
    j = pl.program_id(1)
    o_ref[0, pl.ds(j, 1), :] = jnp.mean(k_ref[0].astype(F32), axis=0, keepdims=True)


def _kmean(proj, *, k_blk_wide):
    b, s, _ = proj.shape
    nblk = s // MOBA_BLOCK
    w = MOBA_HEADS * HEAD_DIM
    return pl.pallas_call(
        _kmean_kernel,
        grid=(b, nblk),
        in_specs=[pl.BlockSpec((1, MOBA_BLOCK, w), lambda bi, j: (bi, j, k_blk_wide))],
        out_specs=pl.BlockSpec((1, nblk, w), lambda bi, j: (bi, 0, 0)),
        out_shape=jax.ShapeDtypeStruct((b, nblk, w), F32),
        compiler_params=_params(("parallel", "arbitrary")),
        name="moba_kmean",
    )(proj)


def _moba_kernel(q_ref, k_ref, v_ref, km_ref, kx_ref, kb_ref, kbmax_ref, o_ref,
                 m_ref, mb_ref, l_ref, s_ref, acc_ref, kn2_ref, qx_ref, *, t, nq, nblk):
    h = pl.program_id(1)
    bpt = t // MOBA_BLOCK
    _key_norm2(k_ref, kn2_ref, t, nq)
    km = km_ref[0]
    if nblk < LANES:
        km = jnp.concatenate([km, jnp.zeros((LANES - nblk, HEAD_DIM), F32)], axis=0)

    def q_tile(qi, carry):
        q0 = pl.multiple_of(qi * t, t)
        gate = lax.dot_general(q_ref[0, pl.ds(q0, t), :].astype(F32), km, _NT, preferred_element_type=F32,
                               precision=lax.Precision.HIGHEST)
        lane = lax.broadcasted_iota(jnp.int32, (t, LANES), 1)
        row = lax.broadcasted_iota(jnp.int32, (t, LANES), 0)
        own = bpt * qi + jnp.right_shift(row, int(math.log2(MOBA_BLOCK)))
        past = lane < own
        neg_inf = -jnp.inf
        g0 = jnp.where(past, gate, neg_inf)
        g = g0
        kth = jnp.max(g, axis=-1, keepdims=True)
        for _ in range(MOBA_TOPK - 1):
            g = jnp.where(g >= kth, neg_inf, g)
            kth = jnp.max(g, axis=-1, keepdims=True)
        qx_ref[...] = jnp.where(past & (g0 < kth), BLOCK_OFF, 0.0).astype(BF16)

        _attend(q_ref, k_ref, v_ref, kn2_ref, (kb_ref, kbmax_ref, h * nq), qi, t=t,
                m_ref=m_ref, mb_ref=mb_ref, l_ref=l_ref, s_ref=s_ref, acc_ref=acc_ref, extra=(qx_ref, kx_ref))
        o_ref[0, pl.ds(q0, t), :] = acc_ref[...].astype(o_ref.dtype)
        return carry

    lax.fori_loop(0, nq, q_tile, 0)


def _moba(proj, kmean, kbias, *, q_blk, k_blk, v_blk):
    b, s, _ = proj.shape
    t = min(ATT_TILE, s)
    nq = s // t
    nblk = s // MOBA_BLOCK
    kb, kbmax = _tile_bias(kbias, MOBA_HEADS, nq, t)
    key_block = (jnp.arange(s)[:, None] // MOBA_BLOCK == jnp.arange(LANES)[None, :]).astype(BF16)
    return pl.pallas_call(
        functools.partial(_moba_kernel, t=t, nq=nq, nblk=nblk),
        grid=(b, MOBA_HEADS),
        in_specs=[
            pl.BlockSpec((1, s, HEAD_DIM), lambda bi, h: (bi, 0, q_blk + h)),
            pl.BlockSpec((1, s, HEAD_DIM), lambda bi, h: (bi, 0, k_blk + h)),
            pl.BlockSpec((1, s, HEAD_DIM), lambda bi, h: (bi, 0, v_blk + h)),
            pl.BlockSpec((1, nblk, HEAD_DIM), lambda bi, h: (bi, 0, h)),
            pl.BlockSpec((s, LANES), lambda bi, h: (0, 0)),
            pl.BlockSpec((1, 1, nq, 1, t), lambda bi, h: (0, h, 0, 0, 0)),
            pl.BlockSpec(memory_space=pltpu.SMEM),
        ],
        out_specs=pl.BlockSpec((1, s, HEAD_DIM), lambda bi, h: (bi, 0, h)),
        out_shape=jax.ShapeDtypeStruct((b, s, MOBA_HEADS * HEAD_DIM), BF16),
        scratch_shapes=_att_scratch(t, HEAD_DIM) + [pltpu.VMEM((t, LANES), BF16)],
        compiler_params=_params(("parallel", "parallel")),
        name="moba_attention",
    )(proj, proj, proj, kmean, key_block, kb, kbmax)


def _outproj_kernel(a_ref, d_ref, wa_ref, wd_ref, g_ref, h_ref, o_ref):
    y = jnp.dot(a_ref[...], wa_ref[...], preferred_element_type=F32)
    y = y + jnp.dot(d_ref[...], wd_ref[...], preferred_element_type=F32)
    o_ref[...] = h_ref[...] + _rms(y, g_ref[...])


def _outproj(a, d, wa, wd, g, h, *, tm=512):
    t, dm = h.shape
    tm = min(tm, t)
    ka, kd = a.shape[1], d.shape[1]
    return pl.pallas_call(
        _outproj_kernel,
        grid=(t // tm,),
        in_specs=[pl.BlockSpec((tm, ka), lambda i: (i, 0)),
                  pl.BlockSpec((tm, kd), lambda i: (i, 0)),
                  pl.BlockSpec((ka, dm), lambda i: (0, 0)),
                  pl.BlockSpec((kd, dm), lambda i: (0, 0)),
                  pl.BlockSpec((1, dm), lambda i: (0, 0)),
                  pl.BlockSpec((tm, dm), lambda i: (i, 0))],
        out_specs=pl.BlockSpec((tm, dm), lambda i: (i, 0)),
        out_shape=jax.ShapeDtypeStruct((t, dm), F32),
        compiler_params=_params(("parallel",)),
        name="mixer_outproj",
    )(a, d, wa, wd, g.reshape(1, dm), h)


def _xattn_kernel(h_ref, g2_ref, wq_ref, mkv_ref, wo_ref, g3_ref, o_ref, *, scale):
    h = h_ref[0]
    xn = _rms(h, g2_ref[...]).astype(BF16)
    q = (jnp.dot(xn, wq_ref[...], preferred_element_type=F32) * scale).astype(BF16)
    w = XATTN_HEADS * HEAD_DIM
    outs = []
    for hd in range(XATTN_HEADS):
        mk = mkv_ref[0, :, hd * HEAD_DIM:(hd + 1) * HEAD_DIM]
        mv = mkv_ref[0, :, w + hd * HEAD_DIM:w + (hd + 1) * HEAD_DIM]
        s = lax.dot_general(q[:, hd * HEAD_DIM:(hd + 1) * HEAD_DIM], mk, _NT, preferred_element_type=F32)
        p = jnp.exp(s - jnp.max(s, axis=-1, keepdims=True))
        p = p / jnp.sum(p, axis=-1, keepdims=True)
        outs.append(jnp.dot(p.astype(BF16), mv, preferred_element_type=F32).astype(BF16))
    y = jnp.dot(jnp.concatenate(outs, axis=-1), wo_ref[...], preferred_element_type=F32)
    o_ref[0] = h + _rms(y, g3_ref[...])


def _xattn(h, g2, wq, mkv, wo, g3, *, tm=512):
    b, s, dm = h.shape
    tm = min(tm, s)
    fixed = lambda bi, i: (0, 0)
    return pl.pallas_call(
        functools.partial(_xattn_kernel, scale=float(HEAD_DIM ** -0.5)),
        grid=(b, s // tm),
        in_specs=[pl.BlockSpec((1, tm, dm), lambda bi, i: (bi, i, 0)),
                  pl.BlockSpec((1, dm), fixed),
                  pl.BlockSpec(wq.shape, fixed),
                  pl.BlockSpec((1,) + mkv.shape[1:], lambda bi, i: (bi, 0, 0)),
                  pl.BlockSpec(wo.shape, fixed),
                  pl.BlockSpec((1, dm), fixed)],
        out_specs=pl.BlockSpec((1, tm, dm), lambda bi, i: (bi, i, 0)),
        out_shape=jax.ShapeDtypeStruct((b, s, dm), F32),
        compiler_params=_params(("parallel", "parallel")),
        name="memory_xattn",
    )(h, g2.reshape(1, dm), wq, mkv, wo, g3.reshape(1, dm))


def _ffn_kernel(h_ref, g4_ref, w1_ref, w2_ref, g5_ref, o_ref, xn_ref, acc_ref):
    f = pl.program_id(1)

    @pl.when(f == 0)
    def _():
        xn_ref[...] = _rms(h_ref[...], g4_ref[...]).astype(BF16)
        acc_ref[...] = jnp.zeros_like(acc_ref)

    u = jnp.maximum(jnp.dot(xn_ref[...], w1_ref[...], preferred_element_type=F32), 0.0)
    acc_ref[...] += jnp.dot((u * u).astype(BF16), w2_ref[...], preferred_element_type=F32)

    @pl.when(f == pl.num_programs(1) - 1)
    def _():
        o_ref[...] = h_ref[...] + _rms(acc_ref[...], g5_ref[...])


def _ffn(h, g4, w1, w2, g5, layer, *, tm=512, tf=1024):
    t, dm = h.shape
    tm = min(tm, t)
    f = w1.shape[2]
    return pl.pallas_call(
        _ffn_kernel,
        grid=(t // tm, f // tf),
        in_specs=[pl.BlockSpec((tm, dm), lambda i, j: (i, 0)),
                  pl.BlockSpec((1, dm), lambda i, j: (0, 0)),
                  pl.BlockSpec((None, dm, tf), lambda i, j: (layer, 0, j)),
                  pl.BlockSpec((None, tf, dm), lambda i, j: (layer, j, 0)),
                  pl.BlockSpec((1, dm), lambda i, j: (0, 0))],
        out_specs=pl.BlockSpec((tm, dm), lambda i, j: (i, 0)),
        out_shape=jax.ShapeDtypeStruct((t, dm), F32),
        scratch_shapes=[pltpu.VMEM((tm, dm), BF16), pltpu.VMEM((tm, dm), F32)],
        compiler_params=_params(("parallel", "arbitrary")),
        name="relu2_mlp",
    )(h, g4.reshape(1, dm), w1, w2, g5.reshape(1, dm))


def _alibi_key_bias(n_heads, seq):
    slopes = jnp.asarray([2.0 ** (-8.0 * (i + 1) / n_heads) for i in range(n_heads)], dtype=F32)
    return (LOG2E * slopes[:, None] * jnp.arange(seq, dtype=F32)[None, :])[None]


def _rope_tables(seq):
    half = MLA_ROPE // 2
    inv = ROPE_THETA ** (-jnp.arange(0, MLA_ROPE, 2, dtype=F32) / MLA_ROPE)
    ang = jnp.arange(seq, dtype=F32)[:, None] * inv[None, :]
    cos, sin = jnp.cos(ang), jnp.sin(ang)
    zero = jnp.zeros((seq, LANES - 2 * half), F32)
    return jnp.concatenate([cos, cos, zero], axis=-1), jnp.concatenate([-sin, sin, zero], axis=-1)


def _rope_pair_columns(w_t1, w_t2):
    zero = jnp.zeros((w_t1.shape[0], LANES - 2 * w_t1.shape[1]), w_t1.dtype)
    return jnp.concatenate([w_t1, w_t2, zero, w_t2, w_t1, zero], axis=-1)


def kernel(x, mem, mem_norm, mem_wkv, norms, xattn_wq, xattn_wo, ffn_w1, ffn_w2, ab_w_in, ab_w_out, fox_b_f, diff_lambda, diff_subln, cd_w_in, cd_w_out, mla_q_norm, mla_kv_norm, mla_w_uq, mla_w_ukv):
    b, s, dm = x.shape
    t = b * s
    depth = norms.shape[0]
    w1_all, w2_all = ffn_w1.astype(BF16), ffn_w2.astype(BF16)
    q_scale = HEAD_DIM ** -0.5 * LOG2E
    half = MLA_ROPE // 2
    cos_t, sin_t = _rope_tables(s)

    mkv = _norm_matmul(mem.reshape(-1, dm), mem_norm, mem_wkv.astype(BF16)).reshape(b, mem.shape[1], -1)

    h = x.reshape(t, dm)
    for i in range(depth):
        n = norms[i]
        j = i // 2
        if i % 2 == 0:
            w = ab_w_in[j]
            fw = FOX_HEADS * HEAD_DIM
            dw = DIFF_HEADS * 2 * HEAD_DIM
            o_g = 3 * fw
            o_dq = o_g + FOX_HEADS
            w_main = jnp.concatenate([w[:, :fw] * q_scale, w[:, fw:o_g],
                                      w[:, o_dq:o_dq + dw] * q_scale, w[:, o_dq + dw:]], axis=-1).astype(BF16)
            w_gate = jnp.concatenate([w[:, o_g:o_dq].T, jnp.zeros((8, dm), F32)], axis=0).astype(BF16)
            proj, gate_t = _inproj(h, n[0], w_main, w_gate, "gate_t")
            proj = proj.reshape(b, s, -1)
            fox_kb = _fox_bias(gate_t, fox_b_f[j], b, s)
            a = _flash(proj, proj, proj, fox_kb, heads=FOX_HEADS, dk=HEAD_DIM, dv=HEAD_DIM,
                       q_blk=0, k_blk=FOX_HEADS, v_blk=2 * FOX_HEADS, name="fox_attention")
            lam_init = 0.8 - 0.6 * math.exp(-0.3 * i)
            d = _diff_attention(proj, _alibi_key_bias(DIFF_HEADS, s), diff_lambda[j], diff_subln[j],
                                q_blk=3 * FOX_HEADS, k_blk=3 * FOX_HEADS + 2 * DIFF_HEADS,
                                v_blk=(3 * fw + 2 * dw) // (2 * HEAD_DIM), lam_init=lam_init)
            wo = ab_w_out[j].astype(BF16)
            h = _outproj(a.reshape(t, -1), d.reshape(t, -1), wo[:fw], wo[fw:], n[1], h)
        else:
            w = cd_w_in[j]
            r2 = 2 * MLA_RANK
            o_m = r2 + MLA_ROPE
            mw = MOBA_HEADS * HEAD_DIM
            w_main = jnp.concatenate([w[:, :r2], w[:, o_m:o_m + mw] * q_scale, w[:, o_m + mw:]],
                                     axis=-1).astype(BF16)
            w_kpe = _rope_pair_columns(w[:, r2:r2 + half], w[:, r2 + half:o_m]).astype(BF16)
            proj, kpe_r = _inproj(h, n[0], w_main, w_kpe, "rope", (cos_t, sin_t))
            wq = mla_w_uq[j].reshape(MLA_RANK, MLA_HEADS, MLA_NOPE + MLA_ROPE)
            wq = jnp.concatenate(
                [jnp.concatenate([wq[:, hd, :MLA_NOPE],
                                  _rope_pair_columns(wq[:, hd, MLA_NOPE:MLA_NOPE + half], wq[:, hd, MLA_NOPE + half:])],
                                 axis=-1) for hd in range(MLA_HEADS)], axis=-1).astype(BF16)
            wkv = mla_w_ukv[j].reshape(MLA_RANK, MLA_HEADS, MLA_NOPE + MLA_V)
            wkv = jnp.concatenate([wkv[:, :, :MLA_NOPE].reshape(MLA_RANK, -1),
                                   wkv[:, :, MLA_NOPE:].reshape(MLA_RANK, -1)], axis=-1).astype(BF16)
            qf, kf, v = _mla_up(proj, kpe_r, mla_q_norm[j], mla_kv_norm[j], wq, wkv, cos_t, sin_t)
            c = _flash(qf.reshape(b, s, -1), kf.reshape(b, s, -1), v.reshape(b, s, -1), None,
                       heads=MLA_HEADS, dk=2 * LANES, dv=MLA_V, q_blk=0, k_blk=0, v_blk=0, name="mla_attention")
            proj = proj.reshape(b, s, -1)
            kmean = _kmean(proj, k_blk_wide=(r2 + mw) // mw)
            dout = _moba(proj, kmean, _alibi_key_bias(MOBA_HEADS, s),
                         q_blk=r2 // HEAD_DIM, k_blk=(r2 + mw) // HEAD_DIM, v_blk=(r2 + 2 * mw) // HEAD_DIM)
            wo = cd_w_out[j].astype(BF16)
            cw = MLA_HEADS * MLA_V
            h = _outproj(c.reshape(t, -1), dout.reshape(t, -1), wo[:cw], wo[cw:], n[1], h)
        h = _xattn(h.reshape(b, s, dm), n[2], xattn_wq[i].astype(BF16), mkv, xattn_wo[i].astype(BF16), n[3])
        h = _ffn(h.reshape(t, dm), n[4], w1_all, w2_all, n[5], i)
    return h.reshape(b, s, dm)
```

```python
import functools
import math

import jax
import jax.numpy as jnp
from jax import lax
from jax.experimental import pallas as pl
from jax.experimental.pallas import tpu as pltpu

F32 = jnp.float32
BF16 = jnp.bfloat16

NORM_EPS = 1e-6
HEAD_DIM = 128
FOX_HEADS = 8
DIFF_HEADS = 4
MLA_HEADS = 8
MLA_NOPE = 128
MLA_ROPE = 64
MLA_V = 128
MLA_RANK = 512
ROPE_THETA = 10000.0
MOBA_HEADS = 8
MOBA_BLOCK = 256
MOBA_TOPK = 3
XATTN_HEADS = 4
LANES = 128
MASKED = -1e30
VMEM_LIMIT = 48 * 1024 * 1024
LOG2E = math.log2(math.e)
SKIP_LOG2 = 160.0
MIN_ROW_SUM = 2.0 ** -64
BLOCK_OFF = -2.0 ** 100
ATT_TILE = 512
ATT_ROWS = 256

_NT = (((1,), (1,)), ((), ()))


def _params(sem):
    return pltpu.CompilerParams(dimension_semantics=sem, vmem_limit_bytes=VMEM_LIMIT)


def _rms(x, g):
    ms = jnp.mean(x * x, axis=-1, keepdims=True)
    return x * lax.rsqrt(ms + NORM_EPS) * g


def _inproj_kernel(x_ref, g_ref, w_ref, aux_w_ref, *rest, aux_mode):
    if aux_mode == "rope":
        cos_ref, sin_ref, o_ref, aux_ref, xn_ref = rest
    else:
        o_ref, aux_ref, xn_ref = rest

    @pl.when(pl.program_id(1) == 0)
    def _():
        xn = _rms(x_ref[...], g_ref[...]).astype(BF16)
        xn_ref[...] = xn
        if aux_mode == "gate_t":
            r = lax.dot_general(aux_w_ref[...], xn, _NT, preferred_element_type=F32)
            aux_ref[...] = r[:8]
        else:
            ab = jnp.dot(xn, aux_w_ref[...], preferred_element_type=F32)
            aux_ref[...] = (ab[:, :LANES] * cos_ref[...] + ab[:, LANES:] * sin_ref[...]).astype(aux_ref.dtype)
        o_ref[...] = jnp.dot(xn, w_ref[...], preferred_element_type=F32).astype(o_ref.dtype)

    @pl.when(pl.program_id(1) > 0)
    def _():
        o_ref[...] = jnp.dot(xn_ref[...], w_ref[...], preferred_element_type=F32).astype(o_ref.dtype)


def _inproj(x, g, w, aux_w, aux_mode, tables=None, *, tm=1024, tn=1024):
    t, k = x.shape
    n = w.shape[1]
    tm = min(tm, t)
    grid = (t // tm, n // tn)
    in_specs = [
        pl.BlockSpec((tm, k), lambda i, j: (i, 0)),
        pl.BlockSpec((1, k), lambda i, j: (0, 0)),
        pl.BlockSpec((k, tn), lambda i, j: (0, j)),
        pl.BlockSpec(aux_w.shape, lambda i, j: (0, 0)),
    ]
    args = [x, g.reshape(1, k), w, aux_w]
    if aux_mode == "rope":
        cos_t, sin_t = tables
        nrep = cos_t.shape[0] // tm
        in_specs += [pl.BlockSpec((tm, LANES), lambda i, j: (i % nrep, 0))] * 2
        args += [cos_t, sin_t]
        aux_shape = jax.ShapeDtypeStruct((t, LANES), BF16)
        aux_spec = pl.BlockSpec((tm, LANES), lambda i, j: (i, 0))
    else:
        aux_shape = jax.ShapeDtypeStruct((8, t), F32)
        aux_spec = pl.BlockSpec((8, tm), lambda i, j: (0, i))
    return pl.pallas_call(
        functools.partial(_inproj_kernel, aux_mode=aux_mode),
        grid=grid,
        in_specs=in_specs,
        out_specs=[pl.BlockSpec((tm, tn), lambda i, j: (i, j)), aux_spec],
        out_shape=[jax.ShapeDtypeStruct((t, n), BF16), aux_shape],
        scratch_shapes=[pltpu.VMEM((tm, k), BF16)],
        compiler_params=_params(("parallel", "arbitrary")),
        name="inproj_" + aux_mode,
    )(*args)


def _norm_matmul_kernel(x_ref, g_ref, w_ref, o_ref):
    xn = _rms(x_ref[...], g_ref[...]).astype(BF16)
    o_ref[...] = jnp.dot(xn, w_ref[...], preferred_element_type=F32).astype(o_ref.dtype)


def _norm_matmul(x, g, w, *, tm=256):
    t, k = x.shape
    n = w.shape[1]
    return pl.pallas_call(
        _norm_matmul_kernel,
        grid=(t // tm,),
        in_specs=[pl.BlockSpec((tm, k), lambda i: (i, 0)),
                  pl.BlockSpec((1, k), lambda i: (0, 0)),
                  pl.BlockSpec((k, n), lambda i: (0, 0))],
        out_specs=pl.BlockSpec((tm, n), lambda i: (i, 0)),
        out_shape=jax.ShapeDtypeStruct((t, n), BF16),
        compiler_params=_params(("parallel",)),
        name="mem_kv_proj",
    )(x, g.reshape(1, k), w)


def _fox_bias_kernel(g_ref, b_ref, o_ref, carry_ref, *, tc):
    @pl.when(pl.program_id(1) == 0)
    def _():
        carry_ref[...] = jnp.zeros_like(carry_ref)

    z = g_ref[...] + b_ref[...]
    logf = jnp.minimum(z, 0.0) - jnp.log(1.0 + jnp.exp(-jnp.abs(z)))
    upper = (lax.broadcasted_iota(jnp.int32, (tc, tc), 0)
             <= lax.broadcasted_iota(jnp.int32, (tc, tc), 1)).astype(F32)
    cum = jnp.dot(logf, upper, preferred_element_type=F32, precision=lax.Precision.HIGHEST) + carry_ref[...]
    carry_ref[...] = cum[:, tc - 1:tc]
    o_ref[0] = cum * (-LOG2E)


def _fox_bias(gate_t, b_f, batch, seq, *, tc=256):
    ns = seq // tc
    return pl.pallas_call(
        functools.partial(_fox_bias_kernel, tc=tc),
        grid=(batch, ns),
        in_specs=[pl.BlockSpec((8, tc), lambda b, s: (0, b * ns + s)),
                  pl.BlockSpec((8, 1), lambda b, s: (0, 0))],
        out_specs=pl.BlockSpec((1, 8, tc), lambda b, s: (b, 0, s)),
        out_shape=jax.ShapeDtypeStruct((batch, 8, seq), F32),
        scratch_shapes=[pltpu.VMEM((8, 1), F32)],
        compiler_params=_params(("parallel", "arbitrary")),
        name="fox_gate_cumsum",
    )(gate_t, b_f.reshape(8, 1))


def _key_norm2(k_ref, kn2_ref, t, nq):
    def body(j, mx):
        kk = k_ref[0, pl.ds(pl.multiple_of(j * t, t), t), :].astype(F32)
        return jnp.maximum(mx, jnp.sum(kk * kk, axis=-1, keepdims=True))
    mx = lax.fori_loop(0, nq, body, jnp.zeros((t, 1), F32))
    kn2_ref[...] = jnp.max(mx, axis=0, keepdims=True)


def _attend(q_ref, k_ref, v_ref, kn2_ref, bias, qi, *, t, m_ref, mb_ref, l_ref, s_ref, acc_ref, extra=None):
    rs = min(ATT_ROWS, t)
    nsub, nch = t // rs, t // LANES
    q0 = pl.multiple_of(qi * t, t)

    def rows(r):
        return slice(r * rs, (r + 1) * rs)

    if bias is not None:
        kb_ref, kbmax_ref, kb_base, kb_slope = bias
        shift = jnp.max(kb_ref[0, 0, qi], axis=-1, keepdims=True)

        def kbias(j):
            return kb_ref[0, 0, j] - shift
    else:
        def kbias(j):
            return None

    def chunk(s, kbv, c):
        sc = s[:, c * LANES:(c + 1) * LANES]
        return sc if kbv is None else sc + kbv[:, c * LANES:(c + 1) * LANES]

    def raw_logits(r, k0, width):
        lhs = q_ref[0, pl.ds(pl.multiple_of(q0 + r * rs, rs), rs), :]
        rhs = k_ref[0, pl.ds(k0, width), :]
        if extra is not None:
            qx_ref, kx_ref = extra
            lhs = jnp.concatenate([lhs, qx_ref[rows(r), :]], axis=-1)
            rhs = jnp.concatenate([rhs, kx_ref[pl.ds(k0, width), :]], axis=-1)
        return lax.dot_general(lhs, rhs, _NT, preferred_element_type=F32)

    def diag_chunks(r, s=None):
        width = (r + 1) * rs
        s = raw_logits(r, q0, width) if s is None else s
        kbv = kbias(qi)
        lo = r * rs
        out = []
        for c in range(width // LANES):
            sc = chunk(s, kbv, c)
            if (c + 1) * LANES - 1 > lo:
                row = lo + lax.broadcasted_iota(jnp.int32, (rs, LANES), 0)
                col = c * LANES + lax.broadcasted_iota(jnp.int32, (rs, LANES), 1)
                sc = jnp.where(col <= row, sc, MASKED)
            out.append(sc)
        return out

    def past_chunks(s, kbv):
        return [chunk(s, kbv, c) for c in range(nch)]

    qf = q_ref[0, pl.ds(q0, t), :].astype(F32)
    qk_bound = jnp.sqrt(jnp.sum(qf * qf, axis=-1, keepdims=True) * kn2_ref[...])
    if bias is not None:
        if kb_slope is not None:
            kb_self = kb_slope * (q0 + lax.broadcasted_iota(jnp.int32, (t, 1), 0)).astype(F32)
            kb_own = kb_self
        else:
            kb_d = kb_ref[0, 0, qi]
            row = lax.broadcasted_iota(jnp.int32, (LANES, LANES), 0)
            col = lax.broadcasted_iota(jnp.int32, (LANES, LANES), 1)
            selfs, owns = [], []
            before = jnp.full((1, 1), -jnp.inf, F32)
            for c in range(nch):
                kbc = kb_d[:, c * LANES:(c + 1) * LANES]
                selfs.append(jnp.sum(jnp.where(col == row, kbc, 0.0), axis=-1, keepdims=True))
                owns.append(jnp.maximum(jnp.max(jnp.where(col <= row, kbc, -jnp.inf), axis=-1, keepdims=True),
                                        before))
                before = jnp.maximum(before, jnp.max(kbc, axis=-1, keepdims=True))
            kb_self = jnp.concatenate(selfs, axis=0)
            kb_own = jnp.concatenate(owns, axis=0)
        self_logit = jnp.sum(qf * k_ref[0, pl.ds(q0, t), :].astype(F32), axis=-1, keepdims=True) + (kb_self - shift)
        m_min = jnp.min(self_logit, axis=0, keepdims=True)
        thresh = jnp.max(m_min - SKIP_LOG2 - jnp.max(qk_bound, axis=0, keepdims=True) + shift)
        j_start = lax.while_loop(lambda j: jnp.logical_and(j < qi, kbmax_ref[kb_base + j] < thresh),
                                 lambda j: j + 1, jnp.int32(0))
        kb_prev = lax.fori_loop(0, qi, lambda j, m: jnp.maximum(m, kbmax_ref[kb_base + j]), jnp.float32(-jnp.inf))
        m_bound = qk_bound + (jnp.maximum(kb_own, kb_prev) - shift)
    else:
        j_start = jnp.int32(0)
        m_bound = qk_bound

    def exact_row_maxima():
        for r in range(nsub):
            m_ref[rows(r), :] = functools.reduce(jnp.maximum, diag_chunks(r))

        def tile(j, carry):
            k0 = pl.multiple_of(j * t, t)
            kbv = kbias(j)
            for r in range(nsub):
                m_ref[rows(r), :] = functools.reduce(jnp.maximum, past_chunks(raw_logits(r, k0, t), kbv),
                                                     m_ref[rows(r), :])
            return carry

        lax.fori_loop(j_start, qi, tile, 0)
        mb_ref[...] = jnp.broadcast_to(jnp.max(m_ref[...], axis=-1, keepdims=True), (t, LANES))

    def probs(chunks, r):
        mb = mb_ref[rows(r), :]
        ps = [jnp.exp2(sc - mb) for sc in chunks]
        p = ps[0] if len(ps) == 1 else jnp.concatenate(ps, axis=-1)
        return functools.reduce(jnp.add, ps), p.astype(BF16)

    def accumulate():
        def logits_into(slot, j):
            k0 = pl.multiple_of(j * t, t)
            for r in range(nsub):
                s_ref[slot, rows(r), :] = raw_logits(r, k0, t)

        def consume(slot, j):
            vt = v_ref[0, pl.ds(pl.multiple_of(j * t, t), t), :]
            kbv = kbias(j)
            for r in range(nsub):
                lsum, p = probs(past_chunks(s_ref[slot, rows(r), :], kbv), r)
                l_ref[rows(r), :] += lsum
                acc_ref[rows(r), :] += jnp.dot(p, vt, preferred_element_type=F32)

        def consume_diag(slot):
            for r in range(nsub):
                width = (r + 1) * rs
                lsum, p = probs(diag_chunks(r, s_ref[slot, rows(r), :]), r)
                l_ref[rows(r), :] += lsum
                acc_ref[rows(r), :] += jnp.dot(p, v_ref[0, pl.ds(q0, width), :], preferred_element_type=F32)

        l_ref[...] = jnp.zeros(l_ref.shape, F32)
        acc_ref[...] = jnp.zeros(acc_ref.shape, F32)
        n_past = qi - j_start
        logits_into(0, j_start)

        def trip(j, tiles):
            for i in range(tiles):
                logits_into((i + 1) % 2, j + i + 1)
                consume(i % 2, j + i)

        def quad(i, carry):
            trip(j_start + 4 * i, 4)
            return carry

        lax.fori_loop(0, n_past // 4, quad, 0)

        @pl.when(n_past % 4 >= 2)
        def _():
            trip(j_start + (n_past // 4) * 4, 2)

        @pl.when(n_past % 2 == 1)
        def _():
            logits_into(1, qi)
            consume(0, qi - 1)
            consume_diag(1)

        @pl.when(n_past % 2 == 0)
        def _():
            consume_diag(0)

        l = jnp.sum(l_ref[...], axis=-1, keepdims=True)
        acc_ref[...] = acc_ref[...] * (1.0 / l)
        return jnp.min(l)

    mb_ref[...] = jnp.broadcast_to(m_bound, (t, LANES))
    l_min = accumulate()

    @pl.when(l_min < MIN_ROW_SUM)
    def _():
        exact_row_maxima()
        accumulate()


def _att_scratch(t, dv, streams=1):
    acc = (t, dv) if streams == 1 else (streams, t, dv)
    return [pltpu.VMEM((t, LANES), F32), pltpu.VMEM((t, LANES), F32), pltpu.VMEM((t, LANES), F32),
            pltpu.VMEM((2, t, t), F32), pltpu.VMEM(acc, F32)] + [pltpu.VMEM((1, 1), F32)] * streams


def _tile_bias(kbias, heads, nq, t):
    kb = kbias.reshape(kbias.shape[0], heads, nq, 1, t)
    return kb, jnp.max(kb, axis=(3, 4)).reshape(-1)


def _flash_kernel(*refs, t, nq, heads, kb_batched, has_bias):
    bi, h = pl.program_id(0), pl.program_id(1)
    if not has_bias:
        q_ref, k_ref, v_ref, o_ref, m_ref, mb_ref, l_ref, s_ref, acc_ref, kn2_ref = refs
        bias = None
    else:
        q_ref, k_ref, v_ref, kb_ref, kbmax_ref, o_ref, m_ref, mb_ref, l_ref, s_ref, acc_ref, kn2_ref = refs
        bias = (kb_ref, kbmax_ref, ((bi * heads if kb_batched else 0) + h) * nq, None)

    _key_norm2(k_ref, kn2_ref, t, nq)

    def q_tile(qi, carry):
        _attend(q_ref, k_ref, v_ref, kn2_ref, bias, qi, t=t, m_ref=m_ref, mb_ref=mb_ref, l_ref=l_ref,
                s_ref=s_ref, acc_ref=acc_ref)
        o_ref[0, pl.ds(pl.multiple_of(qi * t, t), t), :] = acc_ref[...].astype(o_ref.dtype)
        return carry

    lax.fori_loop(0, nq, q_tile, 0)


def _flash(q_arr, k_arr, v_arr, kbias, *, heads, dk, dv, q_blk, k_blk, v_blk, name):
    b, s, _ = q_arr.shape
    t = min(ATT_TILE, s)
    nq = s // t
    in_specs = [
        pl.BlockSpec((1, s, dk), lambda bi, h: (bi, 0, q_blk + h)),
        pl.BlockSpec((1, s, dk), lambda bi, h: (bi, 0, k_blk + h)),
        pl.BlockSpec((1, s, dv), lambda bi, h: (bi, 0, v_blk + h)),
    ]
    args = [q_arr, k_arr, v_arr]
    kb_batched = kbias is not None and kbias.shape[0] == b
    if kbias is not None:
        kb, kbmax = _tile_bias(kbias, heads, nq, t)
        kb_b = (lambda bi: bi) if kb_batched else (lambda bi: 0)
        in_specs += [pl.BlockSpec((1, 1, nq, 1, t), lambda bi, h: (kb_b(bi), h, 0, 0, 0)),
                     pl.BlockSpec(memory_space=pltpu.SMEM)]
        args += [kb, kbmax]
    return pl.pallas_call(
        functools.partial(_flash_kernel, t=t, nq=nq, heads=heads, kb_batched=kb_batched,
                          has_bias=kbias is not None),
        grid=(b, heads),
        in_specs=in_specs,
        out_specs=pl.BlockSpec((1, s, dv), lambda bi, h: (bi, 0, h)),
        out_shape=jax.ShapeDtypeStruct((b, s, heads * dv), BF16),
        scratch_shapes=_att_scratch(t, dv),
        compiler_params=_params(("parallel", "parallel")),
        name=name,
    )(*args)


def _diff_kernel(q1_ref, q2_ref, k1_ref, k2_ref, v_ref, kb_ref, kbmax_ref, ks_ref, lam_ref, g_ref, o_ref,
                 m_ref, mb_ref, l_ref, s_ref, acc_ref, kn2a_ref, kn2b_ref, *, t, nq, lam_init):
    h = pl.program_id(1)
    _key_norm2(k1_ref, kn2a_ref, t, nq)
    _key_norm2(k2_ref, kn2b_ref, t, nq)
    scratch = dict(m_ref=m_ref, mb_ref=mb_ref, l_ref=l_ref, s_ref=s_ref)
    bias = (kb_ref, kbmax_ref, h * nq, ks_ref[h])
    lam = lam_ref[...]
    lam_full = (jnp.exp(jnp.sum(lam[0:1] * lam[1:2], axis=-1, keepdims=True))
                - jnp.exp(jnp.sum(lam[2:3] * lam[3:4], axis=-1, keepdims=True)) + lam_init)

    def q_tile(qi, carry):
        _attend(q1_ref, k1_ref, v_ref, kn2a_ref, bias, qi, t=t, acc_ref=acc_ref.at[0], **scratch)
        _attend(q2_ref, k2_ref, v_ref, kn2b_ref, bias, qi, t=t, acc_ref=acc_ref.at[1], **scratch)
        d = acc_ref[0] - lam_full * acc_ref[1]
        o_ref[0, pl.ds(pl.multiple_of(qi * t, t), t), :] = (
            _rms(d, g_ref[...]) * (1.0 - lam_init)).astype(o_ref.dtype)
        return carry

    lax.fori_loop(0, nq, q_tile, 0)


def _diff_attention(proj, alibi, lam, subln, *, q_blk, k_blk, v_blk, lam_init):
    b, s, _ = proj.shape
    t = min(ATT_TILE, s)
    nq = s // t
    dv = 2 * HEAD_DIM
    kbias, kslope = alibi
    kb, kbmax = _tile_bias(kbias, DIFF_HEADS, nq, t)
    return pl.pallas_call(
        functools.partial(_diff_kernel, t=t, nq=nq, lam_init=lam_init),
        grid=(b, DIFF_HEADS),
        in_specs=[
            pl.BlockSpec((1, s, HEAD_DIM), lambda bi, h: (bi, 0, q_blk + 2 * h)),
            pl.BlockSpec((1, s, HEAD_DIM), lambda bi, h: (bi, 0, q_blk + 2 * h + 1)),
            pl.BlockSpec((1, s, HEAD_DIM), lambda bi, h: (bi, 0, k_blk + 2 * h)),
            pl.BlockSpec((1, s, HEAD_DIM), lambda bi, h: (bi, 0, k_blk + 2 * h + 1)),
            pl.BlockSpec((1, s, dv), lambda bi, h: (bi, 0, v_blk + h)),
            pl.BlockSpec((1, 1, nq, 1, t), lambda bi, h: (0, h, 0, 0, 0)),
            pl.BlockSpec(memory_space=pltpu.SMEM),
            pl.BlockSpec(memory_space=pltpu.SMEM),
            pl.BlockSpec((4, HEAD_DIM), lambda bi, h: (0, 0)),
            pl.BlockSpec((1, dv), lambda bi, h: (0, 0)),
        ],
        out_specs=pl.BlockSpec((1, s, dv), lambda bi, h: (bi, 0, h)),
        out_shape=jax.ShapeDtypeStruct((b, s, DIFF_HEADS * dv), BF16),
        scratch_shapes=_att_scratch(t, dv, streams=2),
        compiler_params=_params(("parallel", "parallel")),
        name="diff_attention",
    )(proj, proj, proj, proj, proj, kb, kbmax, kslope, lam, subln.reshape(1, dv))


def _mla_up_kernel(cq_ref, ckv_ref, kpe_ref, gq_ref, gkv_ref, wq_ref, wkv_ref, cos_ref, sin_ref,
                   qf_ref, kf_ref, v_ref, *, scale):
    cqn = _rms(cq_ref[...].astype(F32), gq_ref[...]).astype(BF16)
    ckvn = _rms(ckv_ref[...].astype(F32), gkv_ref[...]).astype(BF16)
    cos = cos_ref[...]
    sin = sin_ref[...]
    kpe = kpe_ref[...]
    kv = jnp.dot(ckvn, wkv_ref[...], preferred_element_type=F32)
    nk = MLA_HEADS * MLA_NOPE
    v_ref[...] = kv[:, nk:].astype(v_ref.dtype)
    for h in range(MLA_HEADS):
        q3 = jnp.dot(cqn, wq_ref[:, h * 3 * LANES:(h + 1) * 3 * LANES], preferred_element_type=F32)
        rot = q3[:, LANES:2 * LANES] * cos + q3[:, 2 * LANES:] * sin
        qf_ref[:, 2 * h * LANES:(2 * h + 1) * LANES] = (q3[:, :LANES] * scale).astype(qf_ref.dtype)
        qf_ref[:, (2 * h + 1) * LANES:(2 * h + 2) * LANES] = (rot * scale).astype(qf_ref.dtype)
        kf_ref[:, 2 * h * LANES:(2 * h + 1) * LANES] = kv[:, h * LANES:(h + 1) * LANES].astype(kf_ref.dtype)
        kf_ref[:, (2 * h + 1) * LANES:(2 * h + 2) * LANES] = kpe


def _mla_up(proj, kpe_r, gq, gkv, wq, wkv, cos_t, sin_t, *, tm=512):
    t = proj.shape[0]
    tm = min(tm, t)
    nrep = cos_t.shape[0] // tm
    scale = float((MLA_NOPE + MLA_ROPE) ** -0.5 * LOG2E)
    wide = 2 * LANES * MLA_HEADS
    row = lambda i: (i, 0)
    fixed = lambda i: (0, 0)
    return pl.pallas_call(
        functools.partial(_mla_up_kernel, scale=scale),
        grid=(t // tm,),
        in_specs=[
            pl.BlockSpec((tm, MLA_RANK), lambda i: (i, 0)),
            pl.BlockSpec((tm, MLA_RANK), lambda i: (i, 1)),
            pl.BlockSpec((tm, LANES), row),
            pl.BlockSpec((1, MLA_RANK), fixed),
            pl.BlockSpec((1, MLA_RANK), fixed),
            pl.BlockSpec(wq.shape, fixed),
            pl.BlockSpec(wkv.shape, fixed),
            pl.BlockSpec((tm, LANES), lambda i: (i % nrep, 0)),
            pl.BlockSpec((tm, LANES), lambda i: (i % nrep, 0)),
        ],
        out_specs=[pl.BlockSpec((tm, wide), row), pl.BlockSpec((tm, wide), row),
                   pl.BlockSpec((tm, MLA_HEADS * MLA_V), row)],
        out_shape=[jax.ShapeDtypeStruct((t, wide), BF16), jax.ShapeDtypeStruct((t, wide), BF16),
                   jax.ShapeDtypeStruct((t, MLA_HEADS * MLA_V), BF16)],
        compiler_params=_params(("parallel",)),
        name="mla_up",
    )(proj, proj, kpe_r, gq.reshape(1, -1), gkv.reshape(1, -1), wq, wkv, cos_t, sin_t)


def _kmean_kernel(k_ref, o_ref):
    j = pl.program_id(1)
    o_ref[0, pl.ds(j, 1), :] = jnp.mean(k_ref[0].astype(F32), axis=0, keepdims=True)


def _kmean(proj, *, k_blk_wide):
    b, s, _ = proj.shape
    nblk = s // MOBA_BLOCK
    w = MOBA_HEADS * HEAD_DIM
    return pl.pallas_call(
        _kmean_kernel,
        grid=(b, nblk),
        in_specs=[pl.BlockSpec((1, MOBA_BLOCK, w), lambda bi, j: (bi, j, k_blk_wide))],
        out_specs=pl.BlockSpec((1, nblk, w), lambda bi, j: (bi, 0, 0)),
        out_shape=jax.ShapeDtypeStruct((b, nblk, w), F32),
        compiler_params=_params(("parallel", "arbitrary")),
        name="moba_kmean",
    )(proj)


def _moba_kernel(q_ref, k_ref, v_ref, km_ref, kx_ref, kb_ref, kbmax_ref, ks_ref, o_ref,
                 m_ref, mb_ref, l_ref, s_ref, acc_ref, kn2_ref, qx_ref, *, t, nq, nblk):
    h = pl.program_id(1)
    bpt = t // MOBA_BLOCK
    _key_norm2(k_ref, kn2_ref, t, nq)
    km = km_ref[0]
    if nblk < LANES:
        km = jnp.concatenate([km, jnp.zeros((LANES - nblk, HEAD_DIM), F32)], axis=0)

    def q_tile(qi, carry):
        q0 = pl.multiple_of(qi * t, t)
        gate = lax.dot_general(q_ref[0, pl.ds(q0, t), :].astype(F32), km, _NT, preferred_element_type=F32,
                               precision=lax.Precision.HIGHEST)
        lane = lax.broadcasted_iota(jnp.int32, (t, LANES), 1)
        row = lax.broadcasted_iota(jnp.int32, (t, LANES), 0)
        own = bpt * qi + jnp.right_shift(row, int(math.log2(MOBA_BLOCK)))
        past = lane < own
        neg_inf = -jnp.inf
        g0 = jnp.where(past, gate, neg_inf)
        g = g0
        kth = jnp.max(g, axis=-1, keepdims=True)
        for _ in range(MOBA_TOPK - 1):
            g = jnp.where(g >= kth, neg_inf, g)
            kth = jnp.max(g, axis=-1, keepdims=True)
        qx_ref[...] = jnp.where(past & (g0 < kth), BLOCK_OFF, 0.0).astype(BF16)

        _attend(q_ref, k_ref, v_ref, kn2_ref, (kb_ref, kbmax_ref, h * nq, ks_ref[h]), qi, t=t,
                m_ref=m_ref, mb_ref=mb_ref, l_ref=l_ref, s_ref=s_ref, acc_ref=acc_ref, extra=(qx_ref, kx_ref))
        o_ref[0, pl.ds(q0, t), :] = acc_ref[...].astype(o_ref.dtype)
        return carry

    lax.fori_loop(0, nq, q_tile, 0)


def _moba(proj, kmean, alibi, *, q_blk, k_blk, v_blk):
    b, s, _ = proj.shape
    t = min(ATT_TILE, s)
    nq = s // t
    nblk = s // MOBA_BLOCK
    kbias, kslope = alibi
    kb, kbmax = _tile_bias(kbias, MOBA_HEADS, nq, t)
    key_block = (jnp.arange(s)[:, None] // MOBA_BLOCK == jnp.arange(LANES)[None, :]).astype(BF16)
    return pl.pallas_call(
        functools.partial(_moba_kernel, t=t, nq=nq, nblk=nblk),
        grid=(b, MOBA_HEADS),
        in_specs=[
            pl.BlockSpec((1, s, HEAD_DIM), lambda bi, h: (bi, 0, q_blk + h)),
            pl.BlockSpec((1, s, HEAD_DIM), lambda bi, h: (bi, 0, k_blk + h)),
            pl.BlockSpec((1, s, HEAD_DIM), lambda bi, h: (bi, 0, v_blk + h)),
            pl.BlockSpec((1, nblk, HEAD_DIM), lambda bi, h: (bi, 0, h)),
            pl.BlockSpec((s, LANES), lambda bi, h: (0, 0)),
            pl.BlockSpec((1, 1, nq, 1, t), lambda bi, h: (0, h, 0, 0, 0)),
            pl.BlockSpec(memory_space=pltpu.SMEM),
            pl.BlockSpec(memory_space=pltpu.SMEM),
        ],
        out_specs=pl.BlockSpec((1, s, HEAD_DIM), lambda bi, h: (bi, 0, h)),
        out_shape=jax.ShapeDtypeStruct((b, s, MOBA_HEADS * HEAD_DIM), BF16),
        scratch_shapes=_att_scratch(t, HEAD_DIM) + [pltpu.VMEM((t, LANES), BF16)],
        compiler_params=_params(("parallel", "parallel")),
        name="moba_attention",
    )(proj, proj, proj, kmean, key_block, kb, kbmax, kslope)


def _outproj_kernel(a_ref, d_ref, wa_ref, wd_ref, g_ref, h_ref, o_ref):
    y = jnp.dot(a_ref[...], wa_ref[...], preferred_element_type=F32)
    y = y + jnp.dot(d_ref[...], wd_ref[...], preferred_element_type=F32)
    o_ref[...] = h_ref[...] + _rms(y, g_ref[...])


def _outproj(a, d, wa, wd, g, h, *, tm=512):
    t, dm = h.shape
    tm = min(tm, t)
    ka, kd = a.shape[1], d.shape[1]
    return pl.pallas_call(
        _outproj_kernel,
        grid=(t // tm,),
        in_specs=[pl.BlockSpec((tm, ka), lambda i: (i, 0)),
                  pl.BlockSpec((tm, kd), lambda i: (i, 0)),
                  pl.BlockSpec((ka, dm), lambda i: (0, 0)),
                  pl.BlockSpec((kd, dm), lambda i: (0, 0)),
                  pl.BlockSpec((1, dm), lambda i: (0, 0)),
                  pl.BlockSpec((tm, dm), lambda i: (i, 0))],
        out_specs=pl.BlockSpec((tm, dm), lambda i: (i, 0)),
        out_shape=jax.ShapeDtypeStruct((t, dm), F32),
        compiler_params=_params(("parallel",)),
        name="mixer_outproj",
    )(a, d, wa, wd, g.reshape(1, dm), h)


def _xattn_kernel(h_ref, g2_ref, wq_ref, mkv_ref, wo_ref, g3_ref, o_ref, *, scale):
    h = h_ref[0]
    xn = _rms(h, g2_ref[...]).astype(BF16)
    q = (jnp.dot(xn, wq_ref[...], preferred_element_type=F32) * scale).astype(BF16)
    w = XATTN_HEADS * HEAD_DIM
    outs = []
    for hd in range(XATTN_HEADS):
        mk = mkv_ref[0, :, hd * HEAD_DIM:(hd + 1) * HEAD_DIM]
        mv = mkv_ref[0, :, w + hd * HEAD_DIM:w + (hd + 1) * HEAD_DIM]
        s = lax.dot_general(q[:, hd * HEAD_DIM:(hd + 1) * HEAD_DIM], mk, _NT, preferred_element_type=F32)
        p = jnp.exp(s - jnp.max(s, axis=-1, keepdims=True))
        p = p / jnp.sum(p, axis=-1, keepdims=True)
        outs.append(jnp.dot(p.astype(BF16), mv, preferred_element_type=F32).astype(BF16))
    y = jnp.dot(jnp.concatenate(outs, axis=-1), wo_ref[...], preferred_element_type=F32)
    o_ref[0] = h + _rms(y, g3_ref[...])


def _xattn(h, g2, wq, mkv, wo, g3, *, tm=512):
    b, s, dm = h.shape
    tm = min(tm, s)
    fixed = lambda bi, i: (0, 0)
    return pl.pallas_call(
        functools.partial(_xattn_kernel, scale=float(HEAD_DIM ** -0.5)),
        grid=(b, s // tm),
        in_specs=[pl.BlockSpec((1, tm, dm), lambda bi, i: (bi, i, 0)),
                  pl.BlockSpec((1, dm), fixed),
                  pl.BlockSpec(wq.shape, fixed),
                  pl.BlockSpec((1,) + mkv.shape[1:], lambda bi, i: (bi, 0, 0)),
                  pl.BlockSpec(wo.shape, fixed),
                  pl.BlockSpec((1, dm), fixed)],
        out_specs=pl.BlockSpec((1, tm, dm), lambda bi, i: (bi, i, 0)),
        out_shape=jax.ShapeDtypeStruct((b, s, dm), F32),
        compiler_params=_params(("parallel", "parallel")),
        name="memory_xattn",
    )(h, g2.reshape(1, dm), wq, mkv, wo, g3.reshape(1, dm))


def _ffn_kernel(h_ref, g4_ref, w1_ref, w2_ref, g5_ref, o_ref, xn_ref, acc_ref):
    f = pl.program_id(1)

    @pl.when(f == 0)
    def _():
        xn_ref[...] = _rms(h_ref[...], g4_ref[...]).astype(BF16)
        acc_ref[...] = jnp.zeros_like(acc_ref)

    u = jnp.maximum(jnp.dot(xn_ref[...], w1_ref[...], preferred_element_type=F32), 0.0)
    acc_ref[...] += jnp.dot((u * u).astype(BF16), w2_ref[...], preferred_element_type=F32)

    @pl.when(f == pl.num_programs(1) - 1)
    def _():
        o_ref[...] = h_ref[...] + _rms(acc_ref[...], g5_ref[...])


def _ffn(h, g4, w1, w2, g5, layer, *, tm=512, tf=1024):
    t, dm = h.shape
    tm = min(tm, t)
    f = w1.shape[2]
    return pl.pallas_call(
        _ffn_kernel,
        grid=(t // tm, f // tf),
        in_specs=[pl.BlockSpec((tm, dm), lambda i, j: (i, 0)),
                  pl.BlockSpec((1, dm), lambda i, j: (0, 0)),
                  pl.BlockSpec((None, dm, tf), lambda i, j: (layer, 0, j)),
                  pl.BlockSpec((None, tf, dm), lambda i, j: (layer, j, 0)),
                  pl.BlockSpec((1, dm), lambda i, j: (0, 0))],
        out_specs=pl.BlockSpec((tm, dm), lambda i, j: (i, 0)),
        out_shape=jax.ShapeDtypeStruct((t, dm), F32),
        scratch_shapes=[pltpu.VMEM((tm, dm), BF16), pltpu.VMEM((tm, dm), F32)],
        compiler_params=_params(("parallel", "arbitrary")),
        name="relu2_mlp",
    )(h, g4.reshape(1, dm), w1, w2, g5.reshape(1, dm))


def _alibi_key_bias(n_heads, seq):
    slopes = LOG2E * jnp.asarray([2.0 ** (-8.0 * (i + 1) / n_heads) for i in range(n_heads)], dtype=F32)
    return (slopes[:, None] * jnp.arange(seq, dtype=F32)[None, :])[None], slopes


def _rope_tables(seq):
    half = MLA_ROPE // 2
    inv = ROPE_THETA ** (-jnp.arange(0, MLA_ROPE, 2, dtype=F32) / MLA_ROPE)
    ang = jnp.arange(seq, dtype=F32)[:, None] * inv[None, :]
    cos, sin = jnp.cos(ang), jnp.sin(ang)
    zero = jnp.zeros((seq, LANES - 2 * half), F32)
    return jnp.concatenate([cos, cos, zero], axis=-1), jnp.concatenate([-sin, sin, zero], axis=-1)


def _rope_pair_columns(w_t1, w_t2):
    zero = jnp.zeros((w_t1.shape[0], LANES - 2 * w_t1.shape[1]), w_t1.dtype)
    return jnp.concatenate([w_t1, w_t2, zero, w_t2, w_t1, zero], axis=-1)


def kernel(x, mem, mem_norm, mem_wkv, norms, xattn_wq, xattn_wo, ffn_w1, ffn_w2, ab_w_in, ab_w_out, fox_b_f, diff_lambda, diff_subln, cd_w_in, cd_w_out, mla_q_norm, mla_kv_norm, mla_w_uq, mla_w_ukv):
    b, s, dm = x.shape
    t = b * s
    depth = norms.shape[0]
    w1_all, w2_all = ffn_w1.astype(BF16), ffn_w2.astype(BF16)
    q_scale = HEAD_DIM ** -0.5 * LOG2E
    half = MLA_ROPE // 2
    cos_t, sin_t = _rope_tables(s)

    mkv = _norm_matmul(mem.reshape(-1, dm), mem_norm, mem_wkv.astype(BF16)).reshape(b, mem.shape[1], -1)

    h = x.reshape(t, dm)
    for i in range(depth):
        n = norms[i]
        j = i // 2
        if i % 2 == 0:
            w = ab_w_in[j]
            fw = FOX_HEADS * HEAD_DIM
            dw = DIFF_HEADS * 2 * HEAD_DIM
            o_g = 3 * fw
            o_dq = o_g + FOX_HEADS
            w_main = jnp.concatenate([w[:, :fw] * q_scale, w[:, fw:o_g],
                                      w[:, o_dq:o_dq + dw] * q_scale, w[:, o_dq + dw:]], axis=-1).astype(BF16)
            w_gate = jnp.concatenate([w[:, o_g:o_dq].T, jnp.zeros((8, dm), F32)], axis=0).astype(BF16)
            proj, gate_t = _inproj(h, n[0], w_main, w_gate, "gate_t")
            proj = proj.reshape(b, s, -1)
            fox_kb = _fox_bias(gate_t, fox_b_f[j], b, s)
            a = _flash(proj, proj, proj, fox_kb, heads=FOX_HEADS, dk=HEAD_DIM, dv=HEAD_DIM,
                       q_blk=0, k_blk=FOX_HEADS, v_blk=2 * FOX_HEADS, name="fox_attention")
            lam_init = 0.8 - 0.6 * math.exp(-0.3 * i)
            d = _diff_attention(proj, _alibi_key_bias(DIFF_HEADS, s), diff_lambda[j], diff_subln[j],
                                q_blk=3 * FOX_HEADS, k_blk=3 * FOX_HEADS + 2 * DIFF_HEADS,
                                v_blk=(3 * fw + 2 * dw) // (2 * HEAD_DIM), lam_init=lam_init)
            wo = ab_w_out[j].astype(BF16)
            h = _outproj(a.reshape(t, -1), d.reshape(t, -1), wo[:fw], wo[fw:], n[1], h)
        else:
            w = cd_w_in[j]
            r2 = 2 * MLA_RANK
            o_m = r2 + MLA_ROPE
            mw = MOBA_HEADS * HEAD_DIM
            w_main = jnp.concatenate([w[:, :r2], w[:, o_m:o_m + mw] * q_scale, w[:, o_m + mw:]],
                                     axis=-1).astype(BF16)
            w_kpe = _rope_pair_columns(w[:, r2:r2 + half], w[:, r2 + half:o_m]).astype(BF16)
            proj, kpe_r = _inproj(h, n[0], w_main, w_kpe, "rope", (cos_t, sin_t))
            wq = mla_w_uq[j].reshape(MLA_RANK, MLA_HEADS, MLA_NOPE + MLA_ROPE)
            wq = jnp.concatenate(
                [jnp.concatenate([wq[:, hd, :MLA_NOPE],
                                  _rope_pair_columns(wq[:, hd, MLA_NOPE:MLA_NOPE + half], wq[:, hd, MLA_NOPE + half:])],
                                 axis=-1) for hd in range(MLA_HEADS)], axis=-1).astype(BF16)
            wkv = mla_w_ukv[j].reshape(MLA_RANK, MLA_HEADS, MLA_NOPE + MLA_V)
            wkv = jnp.concatenate([wkv[:, :, :MLA_NOPE].reshape(MLA_RANK, -1),
                                   wkv[:, :, MLA_NOPE:].reshape(MLA_RANK, -1)], axis=-1).astype(BF16)
            qf, kf, v = _mla_up(proj, kpe_r, mla_q_norm[j], mla_kv_norm[j], wq, wkv, cos_t, sin_t)
            c = _flash(qf.reshape(b, s, -1), kf.reshape(b, s, -1), v.reshape(b, s, -1), None,
                       heads=MLA_HEADS, dk=2 * LANES, dv=MLA_V, q_blk=0, k_blk=0, v_blk=0, name="mla_attention")
            proj = proj.reshape(b, s, -1)
            kmean = _kmean(proj, k_blk_wide=(r2 + mw) // mw)
            dout = _moba(proj, kmean, _alibi_key_bias(MOBA_HEADS, s),
                         q_blk=r2 // HEAD_DIM, k_blk=(r2 + mw) // HEAD_DIM, v_blk=(r2 + 2 * mw) // HEAD_DIM)
            wo = cd_w_out[j].astype(BF16)
            cw = MLA_HEADS * MLA_V
            h = _outproj(c.reshape(t, -1), dout.reshape(t, -1), wo[:cw], wo[cw:], n[1], h)
        h = _xattn(h.reshape(b, s, dm), n[2], xattn_wq[i].astype(BF16), mkv, xattn_wo[i].astype(BF16), n[3])
        h = _ffn(h.reshape(t, dm), n[4], w1_all, w2_all, n[5], i)
    return h.reshape(b, s, dm)
```

```python
import functools
import math

import jax
import jax.numpy as jnp
from jax import lax
from jax.experimental import pallas as pl
from jax.experimental.pallas import tpu as pltpu

F32 = jnp.float32
BF16 = jnp.bfloat16

NORM_EPS = 1e-6
HEAD_DIM = 128
FOX_HEADS = 8
DIFF_HEADS = 4
MLA_HEADS = 8
MLA_NOPE = 128
MLA_ROPE = 64
MLA_V = 128
MLA_RANK = 512
ROPE_THETA = 10000.0
MOBA_HEADS = 8
MOBA_BLOCK = 256
MOBA_TOPK = 3
XATTN_HEADS = 4
LANES = 128
MASKED = -1e30
VMEM_LIMIT = 48 * 1024 * 1024
LOG2E = math.log2(math.e)
SKIP_LOG2 = 160.0
MIN_ROW_SUM = 2.0 ** -64
BLOCK_OFF = -2.0 ** 100
ATT_TILE = 512
ATT_ROWS = 256

_NT = (((1,), (1,)), ((), ()))


def _params(sem):
    return pltpu.CompilerParams(dimension_semantics=sem, vmem_limit_bytes=VMEM_LIMIT)


def _rms(x, g):
    ms = jnp.mean(x * x, axis=-1, keepdims=True)
    return x * lax.rsqrt(ms + NORM_EPS) * g


def _inproj_kernel(x_ref, g_ref, w_ref, aux_w_ref, *rest, aux_mode):
    if aux_mode == "rope":
        cos_ref, sin_ref, o_ref, aux_ref, xn_ref = rest
    else:
        o_ref, aux_ref, xn_ref = rest

    @pl.when(pl.program_id(1) == 0)
    def _():
        xn = _rms(x_ref[...], g_ref[...]).astype(BF16)
        xn_ref[...] = xn
        if aux_mode == "gate_t":
            r = lax.dot_general(aux_w_ref[...], xn, _NT, preferred_element_type=F32)
            aux_ref[...] = r[:8]
        else:
            ab = jnp.dot(xn, aux_w_ref[...], preferred_element_type=F32)
            aux_ref[...] = (ab[:, :LANES] * cos_ref[...] + ab[:, LANES:] * sin_ref[...]).astype(aux_ref.dtype)
        o_ref[...] = jnp.dot(xn, w_ref[...], preferred_element_type=F32).astype(o_ref.dtype)

    @pl.when(pl.program_id(1) > 0)
    def _():
        o_ref[...] = jnp.dot(xn_ref[...], w_ref[...], preferred_element_type=F32).astype(o_ref.dtype)


def _inproj(x, g, w, aux_w, aux_mode, tables=None, *, tm=1024, tn=1024):
    t, k = x.shape
    n = w.shape[1]
    tm = min(tm, t)
    assert t % tm == 0 and n % tn == 0, (t, tm, n, tn)
    grid = (t // tm, n // tn)
    in_specs = [
        pl.BlockSpec((tm, k), lambda i, j: (i, 0)),
        pl.BlockSpec((1, k), lambda i, j: (0, 0)),
        pl.BlockSpec((k, tn), lambda i, j: (0, j)),
        pl.BlockSpec(aux_w.shape, lambda i, j: (0, 0)),
    ]
    args = [x, g.reshape(1, k), w, aux_w]
    if aux_mode == "rope":
        cos_t, sin_t = tables
        nrep = cos_t.shape[0] // tm
        in_specs += [pl.BlockSpec((tm, LANES), lambda i, j: (i % nrep, 0))] * 2
        args += [cos_t, sin_t]
        aux_shape = jax.ShapeDtypeStruct((t, LANES), BF16)
        aux_spec = pl.BlockSpec((tm, LANES), lambda i, j: (i, 0))
    else:
        aux_shape = jax.ShapeDtypeStruct((8, t), F32)
        aux_spec = pl.BlockSpec((8, tm), lambda i, j: (0, i))
    return pl.pallas_call(
        functools.partial(_inproj_kernel, aux_mode=aux_mode),
        grid=grid,
        in_specs=in_specs,
        out_specs=[pl.BlockSpec((tm, tn), lambda i, j: (i, j)), aux_spec],
        out_shape=[jax.ShapeDtypeStruct((t, n), BF16), aux_shape],
        scratch_shapes=[pltpu.VMEM((tm, k), BF16)],
        compiler_params=_params(("parallel", "arbitrary")),
        name="inproj_" + aux_mode,
    )(*args)


def _norm_matmul_kernel(x_ref, g_ref, w_ref, o_ref):
    xn = _rms(x_ref[...], g_ref[...]).astype(BF16)
    o_ref[...] = jnp.dot(xn, w_ref[...], preferred_element_type=F32).astype(o_ref.dtype)


def _norm_matmul(x, g, w, *, tm=256):
    t, k = x.shape
    n = w.shape[1]
    return pl.pallas_call(
        _norm_matmul_kernel,
        grid=(t // tm,),
        in_specs=[pl.BlockSpec((tm, k), lambda i: (i, 0)),
                  pl.BlockSpec((1, k), lambda i: (0, 0)),
                  pl.BlockSpec((k, n), lambda i: (0, 0))],
        out_specs=pl.BlockSpec((tm, n), lambda i: (i, 0)),
        out_shape=jax.ShapeDtypeStruct((t, n), BF16),
        compiler_params=_params(("parallel",)),
        name="mem_kv_proj",
    )(x, g.reshape(1, k), w)


def _fox_bias_kernel(g_ref, b_ref, o_ref, carry_ref, *, tc):
    @pl.when(pl.program_id(1) == 0)
    def _():
        carry_ref[...] = jnp.zeros_like(carry_ref)

    z = g_ref[...] + b_ref[...]
    logf = jnp.minimum(z, 0.0) - jnp.log(1.0 + jnp.exp(-jnp.abs(z)))
    upper = (lax.broadcasted_iota(jnp.int32, (tc, tc), 0)
             <= lax.broadcasted_iota(jnp.int32, (tc, tc), 1)).astype(F32)
    cum = jnp.dot(logf, upper, preferred_element_type=F32, precision=lax.Precision.HIGHEST) + carry_ref[...]
    carry_ref[...] = cum[:, tc - 1:tc]
    o_ref[0] = cum * (-LOG2E)


def _fox_bias(gate_t, b_f, batch, seq, *, tc=256):
    ns = seq // tc
    return pl.pallas_call(
        functools.partial(_fox_bias_kernel, tc=tc),
        grid=(batch, ns),
        in_specs=[pl.BlockSpec((8, tc), lambda b, s: (0, b * ns + s)),
                  pl.BlockSpec((8, 1), lambda b, s: (0, 0))],
        out_specs=pl.BlockSpec((1, 8, tc), lambda b, s: (b, 0, s)),
        out_shape=jax.ShapeDtypeStruct((batch, 8, seq), F32),
        scratch_shapes=[pltpu.VMEM((8, 1), F32)],
        compiler_params=_params(("parallel", "arbitrary")),
        name="fox_gate_cumsum",
    )(gate_t, b_f.reshape(8, 1))


def _key_norm2(k_ref, kn2_ref, t, nq):
    def body(j, mx):
        kk = k_ref[0, pl.ds(pl.multiple_of(j * t, t), t), :].astype(F32)
        return jnp.maximum(mx, jnp.sum(kk * kk, axis=-1, keepdims=True))
    mx = lax.fori_loop(0, nq, body, jnp.zeros((t, 1), F32))
    kn2_ref[...] = jnp.max(mx, axis=0, keepdims=True)


def _attend(q_ref, k_ref, v_ref, kn2_ref, bias, qi, *, t, m_ref, mb_ref, l_ref, s_ref, acc_ref, extra=None):
    rs = min(ATT_ROWS, t)
    nsub, nch = t // rs, t // LANES
    q0 = pl.multiple_of(qi * t, t)

    def rows(r):
        return slice(r * rs, (r + 1) * rs)

    if bias is not None:
        kb_ref, kbmax_ref, kb_base, kb_slope = bias
        shift = jnp.max(kb_ref[0, 0, qi], axis=-1, keepdims=True)

        def kbias(j):
            return kb_ref[0, 0, j] - shift
    else:
        def kbias(j):
            return None

    def chunk(s, kbv, c):
        sc = s[:, c * LANES:(c + 1) * LANES]
        return sc if kbv is None else sc + kbv[:, c * LANES:(c + 1) * LANES]

    def raw_logits(r, k0, width):
        lhs = q_ref[0, pl.ds(pl.multiple_of(q0 + r * rs, rs), rs), :]
        rhs = k_ref[0, pl.ds(k0, width), :]
        if extra is not None:
            qx_ref, kx_ref = extra
            lhs = jnp.concatenate([lhs, qx_ref[rows(r), :]], axis=-1)
            rhs = jnp.concatenate([rhs, kx_ref[pl.ds(k0, width), :]], axis=-1)
        return lax.dot_general(lhs, rhs, _NT, preferred_element_type=F32)

    def diag_chunks(r, s=None):
        width = (r + 1) * rs
        s = raw_logits(r, q0, width) if s is None else s
        kbv = kbias(qi)
        lo = r * rs
        out = []
        for c in range(width // LANES):
            sc = chunk(s, kbv, c)
            if (c + 1) * LANES - 1 > lo:
                row = lo + lax.broadcasted_iota(jnp.int32, (rs, LANES), 0)
                col = c * LANES + lax.broadcasted_iota(jnp.int32, (rs, LANES), 1)
                sc = jnp.where(col <= row, sc, MASKED)
            out.append(sc)
        return out

    def past_chunks(s, kbv):
        return [chunk(s, kbv, c) for c in range(nch)]

    qf = q_ref[0, pl.ds(q0, t), :].astype(F32)
    qk_bound = jnp.sqrt(jnp.sum(qf * qf, axis=-1, keepdims=True) * kn2_ref[...])
    if bias is not None:
        if kb_slope is not None:
            kb_self = kb_slope * (q0 + lax.broadcasted_iota(jnp.int32, (t, 1), 0)).astype(F32)
            kb_own = kb_self
        else:
            kb_d = kb_ref[0, 0, qi]
            row = lax.broadcasted_iota(jnp.int32, (LANES, LANES), 0)
            col = lax.broadcasted_iota(jnp.int32, (LANES, LANES), 1)
            selfs, owns = [], []
            before = jnp.full((1, 1), -jnp.inf, F32)
            for c in range(nch):
                kbc = kb_d[:, c * LANES:(c + 1) * LANES]
                selfs.append(jnp.sum(jnp.where(col == row, kbc, 0.0), axis=-1, keepdims=True))
                owns.append(jnp.maximum(jnp.max(jnp.where(col <= row, kbc, -jnp.inf), axis=-1, keepdims=True),
                                        before))
                before = jnp.maximum(before, jnp.max(kbc, axis=-1, keepdims=True))
            kb_self = jnp.concatenate(selfs, axis=0)
            kb_own = jnp.concatenate(owns, axis=0)
        self_logit = jnp.sum(qf * k_ref[0, pl.ds(q0, t), :].astype(F32), axis=-1, keepdims=True) + (kb_self - shift)
        m_min = jnp.min(self_logit, axis=0, keepdims=True)
        thresh = jnp.max(m_min - SKIP_LOG2 - jnp.max(qk_bound, axis=0, keepdims=True) + shift)
        j_start = lax.while_loop(lambda j: jnp.logical_and(j < qi, kbmax_ref[kb_base + j] < thresh),
                                 lambda j: j + 1, jnp.int32(0))
        kb_prev = lax.fori_loop(0, qi, lambda j, m: jnp.maximum(m, kbmax_ref[kb_base + j]), jnp.float32(-jnp.inf))
        m_bound = qk_bound + (jnp.maximum(kb_own, kb_prev) - shift)
    else:
        j_start = jnp.int32(0)
        m_bound = qk_bound

    def exact_row_maxima():
        for r in range(nsub):
            m_ref[rows(r), :] = functools.reduce(jnp.maximum, diag_chunks(r))

        def tile(j, carry):
            k0 = pl.multiple_of(j * t, t)
            kbv = kbias(j)
            for r in range(nsub):
                m_ref[rows(r), :] = functools.reduce(jnp.maximum, past_chunks(raw_logits(r, k0, t), kbv),
                                                     m_ref[rows(r), :])
            return carry

        lax.fori_loop(j_start, qi, tile, 0)
        mb_ref[...] = jnp.broadcast_to(jnp.max(m_ref[...], axis=-1, keepdims=True), (t, LANES))

    def probs(chunks, r):
        mb = mb_ref[rows(r), :]
        ps = [jnp.exp2(sc - mb) for sc in chunks]
        p = ps[0] if len(ps) == 1 else jnp.concatenate(ps, axis=-1)
        return functools.reduce(jnp.add, ps), p.astype(BF16)

    def accumulate():
        def logits_into(slot, j):
            k0 = pl.multiple_of(j * t, t)
            for r in range(nsub):
                s_ref[slot, rows(r), :] = raw_logits(r, k0, t)

        def consume(slot, j):
            vt = v_ref[0, pl.ds(pl.multiple_of(j * t, t), t), :]
            kbv = kbias(j)
            for r in range(nsub):
                lsum, p = probs(past_chunks(s_ref[slot, rows(r), :], kbv), r)
                l_ref[rows(r), :] += lsum
                acc_ref[rows(r), :] += jnp.dot(p, vt, preferred_element_type=F32)

        def consume_diag(slot):
            for r in range(nsub):
                width = (r + 1) * rs
                lsum, p = probs(diag_chunks(r, s_ref[slot, rows(r), :]), r)
                l_ref[rows(r), :] += lsum
                acc_ref[rows(r), :] += jnp.dot(p, v_ref[0, pl.ds(q0, width), :], preferred_element_type=F32)

        l_ref[...] = jnp.zeros(l_ref.shape, F32)
        acc_ref[...] = jnp.zeros(acc_ref.shape, F32)
        n_past = qi - j_start
        logits_into(0, j_start)

        def trip(j, tiles):
            for i in range(tiles):
                logits_into((i + 1) % 2, j + i + 1)
                consume(i % 2, j + i)

        def quad(i, carry):
            trip(j_start + 4 * i, 4)
            return carry

        lax.fori_loop(0, n_past // 4, quad, 0)

        @pl.when(n_past % 4 >= 2)
        def _():
            trip(j_start + (n_past // 4) * 4, 2)

        @pl.when(n_past % 2 == 1)
        def _():
            logits_into(1, qi)
            consume(0, qi - 1)
            consume_diag(1)

        @pl.when(n_past % 2 == 0)
        def _():
            consume_diag(0)

        l = jnp.sum(l_ref[...], axis=-1, keepdims=True)
        acc_ref[...] = acc_ref[...] * (1.0 / l)
        return jnp.min(l)

    mb_ref[...] = jnp.broadcast_to(m_bound, (t, LANES))
    l_min = accumulate()

    @pl.when(l_min < MIN_ROW_SUM)
    def _():
        exact_row_maxima()
        accumulate()


def _att_scratch(t, dv, streams=1):
    acc = (t, dv) if streams == 1 else (streams, t, dv)
    return [pltpu.VMEM((t, LANES), F32), pltpu.VMEM((t, LANES), F32), pltpu.VMEM((t, LANES), F32),
            pltpu.VMEM((2, t, t), F32), pltpu.VMEM(acc, F32)] + [pltpu.VMEM((1, 1), F32)] * streams


def _tile_bias(kbias, heads, nq, t):
    kb = kbias.reshape(kbias.shape[0], heads, nq, 1, t)
    return kb, jnp.max(kb, axis=(3, 4)).reshape(-1)


def _flash_kernel(*refs, t, nq, heads, kb_batched, has_bias):
    bi, h = pl.program_id(0), pl.program_id(1)
    if not has_bias:
        q_ref, k_ref, v_ref, o_ref, m_ref, mb_ref, l_ref, s_ref, acc_ref, kn2_ref = refs
        bias = None
    else:
        q_ref, k_ref, v_ref, kb_ref, kbmax_ref, o_ref, m_ref, mb_ref, l_ref, s_ref, acc_ref, kn2_ref = refs
        bias = (kb_ref, kbmax_ref, ((bi * heads if kb_batched else 0) + h) * nq, None)

    _key_norm2(k_ref, kn2_ref, t, nq)

    def q_tile(qi, carry):
        _attend(q_ref, k_ref, v_ref, kn2_ref, bias, qi, t=t, m_ref=m_ref, mb_ref=mb_ref, l_ref=l_ref,
                s_ref=s_ref, acc_ref=acc_ref)
        o_ref[0, pl.ds(pl.multiple_of(qi * t, t), t), :] = acc_ref[...].astype(o_ref.dtype)
        return carry

    lax.fori_loop(0, nq, q_tile, 0)


def _flash(q_arr, k_arr, v_arr, kbias, *, heads, dk, dv, q_blk, k_blk, v_blk, name):
    b, s, _ = q_arr.shape
    t = min(ATT_TILE, s)
    nq = s // t
    in_specs = [
        pl.BlockSpec((1, s, dk), lambda bi, h: (bi, 0, q_blk + h)),
        pl.BlockSpec((1, s, dk), lambda bi, h: (bi, 0, k_blk + h)),
        pl.BlockSpec((1, s, dv), lambda bi, h: (bi, 0, v_blk + h)),
    ]
    args = [q_arr, k_arr, v_arr]
    kb_batched = kbias is not None and kbias.shape[0] == b
    if kbias is not None:
        kb, kbmax = _tile_bias(kbias, heads, nq, t)
        kb_b = (lambda bi: bi) if kb_batched else (lambda bi: 0)
        in_specs += [pl.BlockSpec((1, 1, nq, 1, t), lambda bi, h: (kb_b(bi), h, 0, 0, 0)),
                     pl.BlockSpec(memory_space=pltpu.SMEM)]
        args += [kb, kbmax]
    return pl.pallas_call(
        functools.partial(_flash_kernel, t=t, nq=nq, heads=heads, kb_batched=kb_batched,
                          has_bias=kbias is not None),
        grid=(b, heads),
        in_specs=in_specs,
        out_specs=pl.BlockSpec((1, s, dv), lambda bi, h: (bi, 0, h)),
        out_shape=jax.ShapeDtypeStruct((b, s, heads * dv), BF16),
        scratch_shapes=_att_scratch(t, dv),
        compiler_params=_params(("parallel", "parallel")),
        name=name,
    )(*args)


def _diff_kernel(q1_ref, q2_ref, k1_ref, k2_ref, v_ref, kb_ref, kbmax_ref, ks_ref, lam_ref, g_ref, o_ref,
                 m_ref, mb_ref, l_ref, s_ref, acc_ref, kn2a_ref, kn2b_ref, *, t, nq, lam_init):
    h = pl.program_id(1)
    _key_norm2(k1_ref, kn2a_ref, t, nq)
    _key_norm2(k2_ref, kn2b_ref, t, nq)
    scratch = dict(m_ref=m_ref, mb_ref=mb_ref, l_ref=l_ref, s_ref=s_ref)
    bias = (kb_ref, kbmax_ref, h * nq, ks_ref[h])
    lam = lam_ref[...]
    lam_full = (jnp.exp(jnp.sum(lam[0:1] * lam[1:2], axis=-1, keepdims=True))
                - jnp.exp(jnp.sum(lam[2:3] * lam[3:4], axis=-1, keepdims=True)) + lam_init)

    def q_tile(qi, carry):
        _attend(q1_ref, k1_ref, v_ref, kn2a_ref, bias, qi, t=t, acc_ref=acc_ref.at[0], **scratch)
        _attend(q2_ref, k2_ref, v_ref, kn2b_ref, bias, qi, t=t, acc_ref=acc_ref.at[1], **scratch)
        d = acc_ref[0] - lam_full * acc_ref[1]
        o_ref[0, pl.ds(pl.multiple_of(qi * t, t), t), :] = (
            _rms(d, g_ref[...]) * (1.0 - lam_init)).astype(o_ref.dtype)
        return carry

    lax.fori_loop(0, nq, q_tile, 0)


def _diff_attention(proj, alibi, lam, subln, *, q_blk, k_blk, v_blk, lam_init):
    b, s, _ = proj.shape
    t = min(ATT_TILE, s)
    nq = s // t
    dv = 2 * HEAD_DIM
    kbias, kslope = alibi
    kb, kbmax = _tile_bias(kbias, DIFF_HEADS, nq, t)
    return pl.pallas_call(
        functools.partial(_diff_kernel, t=t, nq=nq, lam_init=lam_init),
        grid=(b, DIFF_HEADS),
        in_specs=[
            pl.BlockSpec((1, s, HEAD_DIM), lambda bi, h: (bi, 0, q_blk + 2 * h)),
            pl.BlockSpec((1, s, HEAD_DIM), lambda bi, h: (bi, 0, q_blk + 2 * h + 1)),
            pl.BlockSpec((1, s, HEAD_DIM), lambda bi, h: (bi, 0, k_blk + 2 * h)),
            pl.BlockSpec((1, s, HEAD_DIM), lambda bi, h: (bi, 0, k_blk + 2 * h + 1)),
            pl.BlockSpec((1, s, dv), lambda bi, h: (bi, 0, v_blk + h)),
            pl.BlockSpec((1, 1, nq, 1, t), lambda bi, h: (0, h, 0, 0, 0)),
            pl.BlockSpec(memory_space=pltpu.SMEM),
            pl.BlockSpec(memory_space=pltpu.SMEM),
            pl.BlockSpec((4, HEAD_DIM), lambda bi, h: (0, 0)),
            pl.BlockSpec((1, dv), lambda bi, h: (0, 0)),
        ],
        out_specs=pl.BlockSpec((1, s, dv), lambda bi, h: (bi, 0, h)),
        out_shape=jax.ShapeDtypeStruct((b, s, DIFF_HEADS * dv), BF16),
        scratch_shapes=_att_scratch(t, dv, streams=2),
        compiler_params=_params(("parallel", "parallel")),
        name="diff_attention",
    )(proj, proj, proj, proj, proj, kb, kbmax, kslope, lam, subln.reshape(1, dv))


def _mla_up_kernel(cq_ref, ckv_ref, kpe_ref, gq_ref, gkv_ref, wq_ref, wkv_ref, cos_ref, sin_ref,
                   qf_ref, kf_ref, v_ref, *, scale):
    cqn = _rms(cq_ref[...].astype(F32), gq_ref[...]).astype(BF16)
    ckvn = _rms(ckv_ref[...].astype(F32), gkv_ref[...]).astype(BF16)
    cos = cos_ref[...]
    sin = sin_ref[...]
    kpe = kpe_ref[...]
    kv = jnp.dot(ckvn, wkv_ref[...], preferred_element_type=F32)
    nk = MLA_HEADS * MLA_NOPE
    v_ref[...] = kv[:, nk:].astype(v_ref.dtype)
    for h in range(MLA_HEADS):
        q3 = jnp.dot(cqn, wq_ref[:, h * 3 * LANES:(h + 1) * 3 * LANES], preferred_element_type=F32)
        rot = q3[:, LANES:2 * LANES] * cos + q3[:, 2 * LANES:] * sin
        qf_ref[:, 2 * h * LANES:(2 * h + 1) * LANES] = (q3[:, :LANES] * scale).astype(qf_ref.dtype)
        qf_ref[:, (2 * h + 1) * LANES:(2 * h + 2) * LANES] = (rot * scale).astype(qf_ref.dtype)
        kf_ref[:, 2 * h * LANES:(2 * h + 1) * LANES] = kv[:, h * LANES:(h + 1) * LANES].astype(kf_ref.dtype)
        kf_ref[:, (2 * h + 1) * LANES:(2 * h + 2) * LANES] = kpe


def _mla_up(proj, kpe_r, gq, gkv, wq, wkv, cos_t, sin_t, *, tm=512):
    t = proj.shape[0]
    tm = min(tm, t)
    nrep = cos_t.shape[0] // tm
    scale = float((MLA_NOPE + MLA_ROPE) ** -0.5 * LOG2E)
    wide = 2 * LANES * MLA_HEADS
    row = lambda i: (i, 0)
    fixed = lambda i: (0, 0)
    return pl.pallas_call(
        functools.partial(_mla_up_kernel, scale=scale),
        grid=(t // tm,),
        in_specs=[
            pl.BlockSpec((tm, MLA_RANK), lambda i: (i, 0)),
            pl.BlockSpec((tm, MLA_RANK), lambda i: (i, 1)),
            pl.BlockSpec((tm, LANES), row),
            pl.BlockSpec((1, MLA_RANK), fixed),
            pl.BlockSpec((1, MLA_RANK), fixed),
            pl.BlockSpec(wq.shape, fixed),
            pl.BlockSpec(wkv.shape, fixed),
            pl.BlockSpec((tm, LANES), lambda i: (i % nrep, 0)),
            pl.BlockSpec((tm, LANES), lambda i: (i % nrep, 0)),
        ],
        out_specs=[pl.BlockSpec((tm, wide), row), pl.BlockSpec((tm, wide), row),
                   pl.BlockSpec((tm, MLA_HEADS * MLA_V), row)],
        out_shape=[jax.ShapeDtypeStruct((t, wide), BF16), jax.ShapeDtypeStruct((t, wide), BF16),
                   jax.ShapeDtypeStruct((t, MLA_HEADS * MLA_V), BF16)],
        compiler_params=_params(("parallel",)),
        name="mla_up",
    )(proj, proj, kpe_r, gq.reshape(1, -1), gkv.reshape(1, -1), wq, wkv, cos_t, sin_t)


def _kmean_kernel(k_ref, o_ref):
    j = pl.program_id(1)
    o_ref[0, pl.ds(j, 1), :] = jnp.mean(k_ref[0].astype(F32), axis=0, keepdims=True)


def _kmean(proj, *, k_blk_wide):
    b, s, _ = proj.shape
    nblk = s // MOBA_BLOCK
    w = MOBA_HEADS * HEAD_DIM
    return pl.pallas_call(
        _kmean_kernel,
        grid=(b, nblk),
        in_specs=[pl.BlockSpec((1, MOBA_BLOCK, w), lambda bi, j: (bi, j, k_blk_wide))],
        out_specs=pl.BlockSpec((1, nblk, w), lambda bi, j: (bi, 0, 0)),
        out_shape=jax.ShapeDtypeStruct((b, nblk, w), F32),
        compiler_params=_params(("parallel", "arbitrary")),
        name="moba_kmean",
    )(proj)


def _moba_kernel(q_ref, k_ref, v_ref, km_ref, kx_ref, kb_ref, kbmax_ref, ks_ref, o_ref,
                 m_ref, mb_ref, l_ref, s_ref, acc_ref, kn2_ref, qx_ref, *, t, nq, nblk):
    h = pl.program_id(1)
    bpt = t // MOBA_BLOCK
    _key_norm2(k_ref, kn2_ref, t, nq)
    km = km_ref[0]
    if nblk < LANES:
        km = jnp.concatenate([km, jnp.zeros((LANES - nblk, HEAD_DIM), F32)], axis=0)

    def q_tile(qi, carry):
        q0 = pl.multiple_of(qi * t, t)
        gate = lax.dot_general(q_ref[0, pl.ds(q0, t), :].astype(F32), km, _NT, preferred_element_type=F32,
                               precision=lax.Precision.HIGHEST)
        lane = lax.broadcasted_iota(jnp.int32, (t, LANES), 1)
        row = lax.broadcasted_iota(jnp.int32, (t, LANES), 0)
        own = bpt * qi + jnp.right_shift(row, int(math.log2(MOBA_BLOCK)))
        past = lane < own
        neg_inf = -jnp.inf
        g0 = jnp.where(past, gate, neg_inf)
        g = g0
        kth = jnp.max(g, axis=-1, keepdims=True)
        for _ in range(MOBA_TOPK - 1):
            g = jnp.where(g >= kth, neg_inf, g)
            kth = jnp.max(g, axis=-1, keepdims=True)
        qx_ref[...] = jnp.where(past & (g0 < kth), BLOCK_OFF, 0.0).astype(BF16)

        _attend(q_ref, k_ref, v_ref, kn2_ref, (kb_ref, kbmax_ref, h * nq, ks_ref[h]), qi, t=t,
                m_ref=m_ref, mb_ref=mb_ref, l_ref=l_ref, s_ref=s_ref, acc_ref=acc_ref, extra=(qx_ref, kx_ref))
        o_ref[0, pl.ds(q0, t), :] = acc_ref[...].astype(o_ref.dtype)
        return carry

    lax.fori_loop(0, nq, q_tile, 0)


def _moba(proj, kmean, alibi, *, q_blk, k_blk, v_blk):
    b, s, _ = proj.shape
    t = min(ATT_TILE, s)
    nq = s // t
    nblk = s // MOBA_BLOCK
    kbias, kslope = alibi
    kb, kbmax = _tile_bias(kbias, MOBA_HEADS, nq, t)
    key_block = (jnp.arange(s)[:, None] // MOBA_BLOCK == jnp.arange(LANES)[None, :]).astype(BF16)
    return pl.pallas_call(
        functools.partial(_moba_kernel, t=t, nq=nq, nblk=nblk),
        grid=(b, MOBA_HEADS),
        in_specs=[
            pl.BlockSpec((1, s, HEAD_DIM), lambda bi, h: (bi, 0, q_blk + h)),
            pl.BlockSpec((1, s, HEAD_DIM), lambda bi, h: (bi, 0, k_blk + h)),
            pl.BlockSpec((1, s, HEAD_DIM), lambda bi, h: (bi, 0, v_blk + h)),
            pl.BlockSpec((1, nblk, HEAD_DIM), lambda bi, h: (bi, 0, h)),
            pl.BlockSpec((s, LANES), lambda bi, h: (0, 0)),
            pl.BlockSpec((1, 1, nq, 1, t), lambda bi, h: (0, h, 0, 0, 0)),
            pl.BlockSpec(memory_space=pltpu.SMEM),
            pl.BlockSpec(memory_space=pltpu.SMEM),
        ],
        out_specs=pl.BlockSpec((1, s, HEAD_DIM), lambda bi, h: (bi, 0, h)),
        out_shape=jax.ShapeDtypeStruct((b, s, MOBA_HEADS * HEAD_DIM), BF16),
        scratch_shapes=_att_scratch(t, HEAD_DIM) + [pltpu.VMEM((t, LANES), BF16)],
        compiler_params=_params(("parallel", "parallel")),
        name="moba_attention",
    )(proj, proj, proj, kmean, key_block, kb, kbmax, kslope)


def _outproj_kernel(a_ref, d_ref, wa_ref, wd_ref, g_ref, h_ref, o_ref):
    y = jnp.dot(a_ref[...], wa_ref[...], preferred_element_type=F32)
    y = y + jnp.dot(d_ref[...], wd_ref[...], preferred_element_type=F32)
    o_ref[...] = h_ref[...] + _rms(y, g_ref[...])


def _outproj(a, d, wa, wd, g, h, *, tm=512):
    t, dm = h.shape
    tm = min(tm, t)
    ka, kd = a.shape[1], d.shape[1]
    return pl.pallas_call(
        _outproj_kernel,
        grid=(t // tm,),
        in_specs=[pl.BlockSpec((tm, ka), lambda i: (i, 0)),
                  pl.BlockSpec((tm, kd), lambda i: (i, 0)),
                  pl.BlockSpec((ka, dm), lambda i: (0, 0)),
                  pl.BlockSpec((kd, dm), lambda i: (0, 0)),
                  pl.BlockSpec((1, dm), lambda i: (0, 0)),
                  pl.BlockSpec((tm, dm), lambda i: (i, 0))],
        out_specs=pl.BlockSpec((tm, dm), lambda i: (i, 0)),
        out_shape=jax.ShapeDtypeStruct((t, dm), F32),
        compiler_params=_params(("parallel",)),
        name="mixer_outproj",
    )(a, d, wa, wd, g.reshape(1, dm), h)


def _xattn_kernel(h_ref, g2_ref, wq_ref, mkv_ref, wo_ref, g3_ref, o_ref, *, scale):
    h = h_ref[0]
    xn = _rms(h, g2_ref[...]).astype(BF16)
    q = (jnp.dot(xn, wq_ref[...], preferred_element_type=F32) * scale).astype(BF16)
    w = XATTN_HEADS * HEAD_DIM
    outs = []
    for hd in range(XATTN_HEADS):
        mk = mkv_ref[0, :, hd * HEAD_DIM:(hd + 1) * HEAD_DIM]
        mv = mkv_ref[0, :, w + hd * HEAD_DIM:w + (hd + 1) * HEAD_DIM]
        s = lax.dot_general(q[:, hd * HEAD_DIM:(hd + 1) * HEAD_DIM], mk, _NT, preferred_element_type=F32)
        p = jnp.exp(s - jnp.max(s, axis=-1, keepdims=True))
        p = p / jnp.sum(p, axis=-1, keepdims=True)
        outs.append(jnp.dot(p.astype(BF16), mv, preferred_element_type=F32).astype(BF16))
    y = jnp.dot(jnp.concatenate(outs, axis=-1), wo_ref[...], preferred_element_type=F32)
    o_ref[0] = h + _rms(y, g3_ref[...])


def _xattn(h, g2, wq, mkv, wo, g3, *, tm=512):
    b, s, dm = h.shape
    tm = min(tm, s)
    fixed = lambda bi, i: (0, 0)
    return pl.pallas_call(
        functools.partial(_xattn_kernel, scale=float(HEAD_DIM ** -0.5)),
        grid=(b, s // tm),
        in_specs=[pl.BlockSpec((1, tm, dm), lambda bi, i: (bi, i, 0)),
                  pl.BlockSpec((1, dm), fixed),
                  pl.BlockSpec(wq.shape, fixed),
                  pl.BlockSpec((1,) + mkv.shape[1:], lambda bi, i: (bi, 0, 0)),
                  pl.BlockSpec(wo.shape, fixed),
                  pl.BlockSpec((1, dm), fixed)],
        out_specs=pl.BlockSpec((1, tm, dm), lambda bi, i: (bi, i, 0)),
        out_shape=jax.ShapeDtypeStruct((b, s, dm), F32),
        compiler_params=_params(("parallel", "parallel")),
        name="memory_xattn",
    )(h, g2.reshape(1, dm), wq, mkv, wo, g3.reshape(1, dm))


def _ffn_kernel(h_ref, g4_ref, w1_ref, w2_ref, g5_ref, o_ref, xn_ref, acc_ref):
    f = pl.program_id(1)
    last = pl.num_programs(1) - 1

    def partial_out(xn):
        u = jnp.maximum(jnp.dot(xn, w1_ref[...], preferred_element_type=F32), 0.0)
        return jnp.dot((u * u).astype(BF16), w2_ref[...], preferred_element_type=F32)

    @pl.when(f == 0)
    def _():
        xn = _rms(h_ref[...], g4_ref[...]).astype(BF16)
        xn_ref[...] = xn
        acc_ref[...] = partial_out(xn)

    @pl.when(jnp.logical_and(f > 0, f < last))
    def _():
        acc_ref[...] += partial_out(xn_ref[...])

    @pl.when(f == last)
    def _():
        y = acc_ref[...] + partial_out(xn_ref[...])
        o_ref[...] = h_ref[...] + _rms(y, g5_ref[...])


def _ffn(h, g4, w1, w2, g5, layer, *, tm=512, tf=1024):
    t, dm = h.shape
    tm = min(tm, t)
    f = w1.shape[2]
    assert t % tm == 0 and f % tf == 0 and f // tf >= 2, (t, tm, f, tf)
    return pl.pallas_call(
        _ffn_kernel,
        grid=(t // tm, f // tf),
        in_specs=[pl.BlockSpec((tm, dm), lambda i, j: (i, 0)),
                  pl.BlockSpec((1, dm), lambda i, j: (0, 0)),
                  pl.BlockSpec((None, dm, tf), lambda i, j: (layer, 0, j)),
                  pl.BlockSpec((None, tf, dm), lambda i, j: (layer, j, 0)),
                  pl.BlockSpec((1, dm), lambda i, j: (0, 0))],
        out_specs=pl.BlockSpec((tm, dm), lambda i, j: (i, 0)),
        out_shape=jax.ShapeDtypeStruct((t, dm), F32),
        scratch_shapes=[pltpu.VMEM((tm, dm), BF16), pltpu.VMEM((tm, dm), F32)],
        compiler_params=_params(("parallel", "arbitrary")),
        name="relu2_mlp",
    )(h, g4.reshape(1, dm), w1, w2, g5.reshape(1, dm))


def _alibi_key_bias(n_heads, seq):
    slopes = LOG2E * jnp.asarray([2.0 ** (-8.0 * (i + 1) / n_heads) for i in range(n_heads)], dtype=F32)
    return (slopes[:, None] * jnp.arange(seq, dtype=F32)[None, :])[None], slopes


def _rope_tables(seq):
    half = MLA_ROPE // 2
    inv = ROPE_THETA ** (-jnp.arange(0, MLA_ROPE, 2, dtype=F32) / MLA_ROPE)
    ang = jnp.arange(seq, dtype=F32)[:, None] * inv[None, :]
    cos, sin = jnp.cos(ang), jnp.sin(ang)
    zero = jnp.zeros((seq, LANES - 2 * half), F32)
    return jnp.concatenate([cos, cos, zero], axis=-1), jnp.concatenate([-sin, sin, zero], axis=-1)


def _rope_pair_columns(w_t1, w_t2):
    zero = jnp.zeros((w_t1.shape[0], LANES - 2 * w_t1.shape[1]), w_t1.dtype)
    return jnp.concatenate([w_t1, w_t2, zero, w_t2, w_t1, zero], axis=-1)


def kernel(x, mem, mem_norm, mem_wkv, norms, xattn_wq, xattn_wo, ffn_w1, ffn_w2, ab_w_in, ab_w_out, fox_b_f, diff_lambda, diff_subln, cd_w_in, cd_w_out, mla_q_norm, mla_kv_norm, mla_w_uq, mla_w_ukv):
    b, s, dm = x.shape
    t = b * s
    depth = norms.shape[0]
    w1_all, w2_all = ffn_w1.astype(BF16), ffn_w2.astype(BF16)
    q_scale = HEAD_DIM ** -0.5 * LOG2E
    half = MLA_ROPE // 2
    cos_t, sin_t = _rope_tables(s)

    mkv = _norm_matmul(mem.reshape(-1, dm), mem_norm, mem_wkv.astype(BF16)).reshape(b, mem.shape[1], -1)

    h = x.reshape(t, dm)
    for i in range(depth):
        n = norms[i]
        j = i // 2
        if i % 2 == 0:
            w = ab_w_in[j]
            fw = FOX_HEADS * HEAD_DIM
            dw = DIFF_HEADS * 2 * HEAD_DIM
            o_g = 3 * fw
            o_dq = o_g + FOX_HEADS
            w_main = jnp.concatenate([w[:, :fw] * q_scale, w[:, fw:o_g],
                                      w[:, o_dq:o_dq + dw] * q_scale, w[:, o_dq + dw:]], axis=-1).astype(BF16)
            w_gate = jnp.concatenate([w[:, o_g:o_dq].T, jnp.zeros((8, dm), F32)], axis=0).astype(BF16)
            proj, gate_t = _inproj(h, n[0], w_main, w_gate, "gate_t")
            proj = proj.reshape(b, s, -1)
            fox_kb = _fox_bias(gate_t, fox_b_f[j], b, s)
            a = _flash(proj, proj, proj, fox_kb, heads=FOX_HEADS, dk=HEAD_DIM, dv=HEAD_DIM,
                       q_blk=0, k_blk=FOX_HEADS, v_blk=2 * FOX_HEADS, name="fox_attention")
            lam_init = 0.8 - 0.6 * math.exp(-0.3 * i)
            d = _diff_attention(proj, _alibi_key_bias(DIFF_HEADS, s), diff_lambda[j], diff_subln[j],
                                q_blk=3 * FOX_HEADS, k_blk=3 * FOX_HEADS + 2 * DIFF_HEADS,
                                v_blk=(3 * fw + 2 * dw) // (2 * HEAD_DIM), lam_init=lam_init)
            wo = ab_w_out[j].astype(BF16)
            h = _outproj(a.reshape(t, -1), d.reshape(t, -1), wo[:fw], wo[fw:], n[1], h)
        else:
            w = cd_w_in[j]
            r2 = 2 * MLA_RANK
            o_m = r2 + MLA_ROPE
            mw = MOBA_HEADS * HEAD_DIM
            w_main = jnp.concatenate([w[:, :r2], w[:, o_m:o_m + mw] * q_scale, w[:, o_m + mw:]],
                                     axis=-1).astype(BF16)
            w_kpe = _rope_pair_columns(w[:, r2:r2 + half], w[:, r2 + half:o_m]).astype(BF16)
            proj, kpe_r = _inproj(h, n[0], w_main, w_kpe, "rope", (cos_t, sin_t))
            wq = mla_w_uq[j].reshape(MLA_RANK, MLA_HEADS, MLA_NOPE + MLA_ROPE)
            wq = jnp.concatenate(
                [jnp.concatenate([wq[:, hd, :MLA_NOPE],
                                  _rope_pair_columns(wq[:, hd, MLA_NOPE:MLA_NOPE + half], wq[:, hd, MLA_NOPE + half:])],
                                 axis=-1) for hd in range(MLA_HEADS)], axis=-1).astype(BF16)
            wkv = mla_w_ukv[j].reshape(MLA_RANK, MLA_HEADS, MLA_NOPE + MLA_V)
            wkv = jnp.concatenate([wkv[:, :, :MLA_NOPE].reshape(MLA_RANK, -1),
                                   wkv[:, :, MLA_NOPE:].reshape(MLA_RANK, -1)], axis=-1).astype(BF16)
            qf, kf, v = _mla_up(proj, kpe_r, mla_q_norm[j], mla_kv_norm[j], wq, wkv, cos_t, sin_t)
            c = _flash(qf.reshape(b, s, -1), kf.reshape(b, s, -1), v.reshape(b, s, -1), None,
                       heads=MLA_HEADS, dk=2 * LANES, dv=MLA_V, q_blk=0, k_blk=0, v_blk=0, name="mla_attention")
            proj = proj.reshape(b, s, -1)
            kmean = _kmean(proj, k_blk_wide=(r2 + mw) // mw)
            dout = _moba(proj, kmean, _alibi_key_bias(MOBA_HEADS, s),
                         q_blk=r2 // HEAD_DIM, k_blk=(r2 + mw) // HEAD_DIM, v_blk=(r2 + 2 * mw) // HEAD_DIM)
            wo = cd_w_out[j].astype(BF16)
            cw = MLA_HEADS * MLA_V
            h = _outproj(c.reshape(t, -1), dout.reshape(t, -1), wo[:cw], wo[cw:], n[1], h)
        h = _xattn(h.reshape(b, s, dm), n[2], xattn_wq[i].astype(BF16), mkv, xattn_wo[i].astype(BF16), n[3])
        h = _ffn(h.reshape(t, dm), n[4], w1_all, w2_all, n[5], i)
    return h.reshape(b, s, dm)
```

```python
import functools
import math

import jax
import jax.numpy as jnp
from jax import lax
from jax.experimental import pallas as pl
from jax.experimental.pallas import tpu as pltpu

F32 = jnp.float32
BF16 = jnp.bfloat16

NORM_EPS = 1e-6
HEAD_DIM = 128
FOX_HEADS = 8
DIFF_HEADS = 4
MLA_HEADS = 8
MLA_NOPE = 128
MLA_ROPE = 64
MLA_V = 128
MLA_RANK = 512
ROPE_THETA = 10000.0
MOBA_HEADS = 8
MOBA_BLOCK = 256
MOBA_TOPK = 3
XATTN_HEADS = 4
LANES = 128
MASKED = -1e30
VMEM_LIMIT = 48 * 1024 * 1024
LOG2E = math.log2(math.e)
SKIP_LOG2 = 160.0
MIN_ROW_SUM = 2.0 ** -64
BLOCK_OFF = -2.0 ** 100
ATT_TILE = 512
ATT_ROWS = 256

_NT = (((1,), (1,)), ((), ()))


def _params(sem):
    return pltpu.CompilerParams(dimension_semantics=sem, vmem_limit_bytes=VMEM_LIMIT)


def _rms(x, g):
    ms = jnp.mean(x * x, axis=-1, keepdims=True)
    return x * lax.rsqrt(ms + NORM_EPS) * g


def _inproj_kernel(x_ref, g_ref, w_ref, aux_w_ref, *rest, aux_mode):
    if aux_mode == "rope":
        cos_ref, sin_ref, o_ref, aux_ref, xn_ref = rest
    else:
        o_ref, aux_ref, xn_ref = rest

    @pl.when(pl.program_id(1) == 0)
    def _():
        xn = _rms(x_ref[...], g_ref[...]).astype(BF16)
        xn_ref[...] = xn
        if aux_mode == "gate_t":
            r = lax.dot_general(aux_w_ref[...], xn, _NT, preferred_element_type=F32)
            aux_ref[...] = r[:8]
        else:
            ab = jnp.dot(xn, aux_w_ref[...], preferred_element_type=F32)
            aux_ref[...] = (ab[:, :LANES] * cos_ref[...] + ab[:, LANES:] * sin_ref[...]).astype(aux_ref.dtype)
        o_ref[...] = jnp.dot(xn, w_ref[...], preferred_element_type=F32).astype(o_ref.dtype)

    @pl.when(pl.program_id(1) > 0)
    def _():
        o_ref[...] = jnp.dot(xn_ref[...], w_ref[...], preferred_element_type=F32).astype(o_ref.dtype)


def _inproj(x, g, w, aux_w, aux_mode, tables=None, *, tm=1024, tn=1024):
    t, k = x.shape
    n = w.shape[1]
    tm = min(tm, t)
    assert t % tm == 0 and n % tn == 0, (t, tm, n, tn)
    grid = (t // tm, n // tn)
    in_specs = [
        pl.BlockSpec((tm, k), lambda i, j: (i, 0)),
        pl.BlockSpec((1, k), lambda i, j: (0, 0)),
        pl.BlockSpec((k, tn), lambda i, j: (0, j)),
        pl.BlockSpec(aux_w.shape, lambda i, j: (0, 0)),
    ]
    args = [x, g.reshape(1, k), w, aux_w]
    if aux_mode == "rope":
        cos_t, sin_t = tables
        nrep = cos_t.shape[0] // tm
        in_specs += [pl.BlockSpec((tm, LANES), lambda i, j: (i % nrep, 0))] * 2
        args += [cos_t, sin_t]
        aux_shape = jax.ShapeDtypeStruct((t, LANES), BF16)
        aux_spec = pl.BlockSpec((tm, LANES), lambda i, j: (i, 0))
    else:
        aux_shape = jax.ShapeDtypeStruct((8, t), F32)
        aux_spec = pl.BlockSpec((8, tm), lambda i, j: (0, i))
    return pl.pallas_call(
        functools.partial(_inproj_kernel, aux_mode=aux_mode),
        grid=grid,
        in_specs=in_specs,
        out_specs=[pl.BlockSpec((tm, tn), lambda i, j: (i, j)), aux_spec],
        out_shape=[jax.ShapeDtypeStruct((t, n), BF16), aux_shape],
        scratch_shapes=[pltpu.VMEM((tm, k), BF16)],
        compiler_params=_params(("parallel", "arbitrary")),
        name="inproj_" + aux_mode,
    )(*args)


def _norm_matmul_kernel(x_ref, g_ref, w_ref, o_ref):
    xn = _rms(x_ref[...], g_ref[...]).astype(BF16)
    o_ref[...] = jnp.dot(xn, w_ref[...], preferred_element_type=F32).astype(o_ref.dtype)


def _norm_matmul(x, g, w, *, tm=256):
    t, k = x.shape
    n = w.shape[1]
    return pl.pallas_call(
        _norm_matmul_kernel,
        grid=(t // tm,),
        in_specs=[pl.BlockSpec((tm, k), lambda i: (i, 0)),
                  pl.BlockSpec((1, k), lambda i: (0, 0)),
                  pl.BlockSpec((k, n), lambda i: (0, 0))],
        out_specs=pl.BlockSpec((tm, n), lambda i: (i, 0)),
        out_shape=jax.ShapeDtypeStruct((t, n), BF16),
        compiler_params=_params(("parallel",)),
        name="mem_kv_proj",
    )(x, g.reshape(1, k), w)


def _fox_bias_kernel(g_ref, b_ref, o_ref, carry_ref, *, tc):
    @pl.when(pl.program_id(1) == 0)
    def _():
        carry_ref[...] = jnp.zeros_like(carry_ref)

    z = g_ref[...] + b_ref[...]
    logf = jnp.minimum(z, 0.0) - jnp.log(1.0 + jnp.exp(-jnp.abs(z)))
    upper = (lax.broadcasted_iota(jnp.int32, (tc, tc), 0)
             <= lax.broadcasted_iota(jnp.int32, (tc, tc), 1)).astype(F32)
    cum = jnp.dot(logf, upper, preferred_element_type=F32, precision=lax.Precision.HIGHEST) + carry_ref[...]
    carry_ref[...] = cum[:, tc - 1:tc]
    o_ref[0] = cum * (-LOG2E)


def _fox_bias(gate_t, b_f, batch, seq, *, tc=256):
    ns = seq // tc
    return pl.pallas_call(
        functools.partial(_fox_bias_kernel, tc=tc),
        grid=(batch, ns),
        in_specs=[pl.BlockSpec((8, tc), lambda b, s: (0, b * ns + s)),
                  pl.BlockSpec((8, 1), lambda b, s: (0, 0))],
        out_specs=pl.BlockSpec((1, 8, tc), lambda b, s: (b, 0, s)),
        out_shape=jax.ShapeDtypeStruct((batch, 8, seq), F32),
        scratch_shapes=[pltpu.VMEM((8, 1), F32)],
        compiler_params=_params(("parallel", "arbitrary")),
        name="fox_gate_cumsum",
    )(gate_t, b_f.reshape(8, 1))


def _key_norm2(k_ref, kn2_ref, t, nq):
    def body(j, mx):
        kk = k_ref[0, pl.ds(pl.multiple_of(j * t, t), t), :].astype(F32)
        return jnp.maximum(mx, jnp.sum(kk * kk, axis=-1, keepdims=True))
    mx = lax.fori_loop(0, nq, body, jnp.zeros((t, 1), F32))
    kn2_ref[...] = jnp.max(mx, axis=0, keepdims=True)


def _attend(q_ref, k_ref, v_ref, kn2_ref, bias, qi, *, t, m_ref, mb_ref, l_ref, s_ref, acc_ref, extra=None):
    rs = min(ATT_ROWS, t)
    nsub, nch = t // rs, t // LANES
    q0 = pl.multiple_of(qi * t, t)

    def rows(r):
        return slice(r * rs, (r + 1) * rs)

    if bias is not None:
        kb_ref, kbmax_ref, kb_base, kb_slope = bias
        shift = jnp.max(kb_ref[0, 0, qi], axis=-1, keepdims=True)

        def kbias(j):
            return kb_ref[0, 0, j] - shift
    else:
        def kbias(j):
            return None

    def chunk(s, kbv, c):
        sc = s[:, c * LANES:(c + 1) * LANES]
        return sc if kbv is None else sc + kbv[:, c * LANES:(c + 1) * LANES]

    def raw_logits(r, k0, width):
        lhs = q_ref[0, pl.ds(pl.multiple_of(q0 + r * rs, rs), rs), :]
        rhs = k_ref[0, pl.ds(k0, width), :]
        if extra is not None:
            qx_ref, kx_ref = extra
            lhs = jnp.concatenate([lhs, qx_ref[rows(r), :]], axis=-1)
            rhs = jnp.concatenate([rhs, kx_ref[pl.ds(k0, width), :]], axis=-1)
        return lax.dot_general(lhs, rhs, _NT, preferred_element_type=F32)

    def diag_chunks(r, s=None):
        width = (r + 1) * rs
        s = raw_logits(r, q0, width) if s is None else s
        kbv = kbias(qi)
        lo = r * rs
        out = []
        for c in range(width // LANES):
            sc = chunk(s, kbv, c)
            if (c + 1) * LANES - 1 > lo:
                row = lo + lax.broadcasted_iota(jnp.int32, (rs, LANES), 0)
                col = c * LANES + lax.broadcasted_iota(jnp.int32, (rs, LANES), 1)
                sc = jnp.where(col <= row, sc, MASKED)
            out.append(sc)
        return out

    def past_chunks(s, kbv):
        return [chunk(s, kbv, c) for c in range(nch)]

    qf = q_ref[0, pl.ds(q0, t), :].astype(F32)
    qk_bound = jnp.sqrt(jnp.sum(qf * qf, axis=-1, keepdims=True) * kn2_ref[...])
    if bias is not None:
        if kb_slope is not None:
            kb_self = kb_slope * (q0 + lax.broadcasted_iota(jnp.int32, (t, 1), 0)).astype(F32)
            kb_own = kb_self
        else:
            kb_d = kb_ref[0, 0, qi]
            row = lax.broadcasted_iota(jnp.int32, (LANES, LANES), 0)
            col = lax.broadcasted_iota(jnp.int32, (LANES, LANES), 1)
            selfs, owns = [], []
            before = jnp.full((1, 1), -jnp.inf, F32)
            for c in range(nch):
                kbc = kb_d[:, c * LANES:(c + 1) * LANES]
                selfs.append(jnp.sum(jnp.where(col == row, kbc, 0.0), axis=-1, keepdims=True))
                owns.append(jnp.maximum(jnp.max(jnp.where(col <= row, kbc, -jnp.inf), axis=-1, keepdims=True),
                                        before))
                before = jnp.maximum(before, jnp.max(kbc, axis=-1, keepdims=True))
            kb_self = jnp.concatenate(selfs, axis=0)
            kb_own = jnp.concatenate(owns, axis=0)
        self_logit = jnp.sum(qf * k_ref[0, pl.ds(q0, t), :].astype(F32), axis=-1, keepdims=True) + (kb_self - shift)
        m_min = jnp.min(self_logit, axis=0, keepdims=True)
        thresh = jnp.max(m_min - SKIP_LOG2 - jnp.max(qk_bound, axis=0, keepdims=True) + shift)
        j_start = lax.while_loop(lambda j: jnp.logical_and(j < qi, kbmax_ref[kb_base + j] < thresh),
                                 lambda j: j + 1, jnp.int32(0))
        kb_prev = lax.fori_loop(0, qi, lambda j, m: jnp.maximum(m, kbmax_ref[kb_base + j]), jnp.float32(-jnp.inf))
        m_bound = qk_bound + (jnp.maximum(kb_own, kb_prev) - shift)
    else:
        j_start = jnp.int32(0)
        m_bound = qk_bound

    def exact_row_maxima():
        for r in range(nsub):
            m_ref[rows(r), :] = functools.reduce(jnp.maximum, diag_chunks(r))

        def tile(j, carry):
            k0 = pl.multiple_of(j * t, t)
            kbv = kbias(j)
            for r in range(nsub):
                m_ref[rows(r), :] = functools.reduce(jnp.maximum, past_chunks(raw_logits(r, k0, t), kbv),
                                                     m_ref[rows(r), :])
            return carry

        lax.fori_loop(j_start, qi, tile, 0)
        mb_ref[...] = jnp.broadcast_to(jnp.max(m_ref[...], axis=-1, keepdims=True), (t, LANES))

    def probs(chunks, r):
        mb = mb_ref[rows(r), :]
        ps = [jnp.exp2(sc - mb) for sc in chunks]
        p = ps[0] if len(ps) == 1 else jnp.concatenate(ps, axis=-1)
        return functools.reduce(jnp.add, ps), p.astype(BF16)

    def accumulate():
        def logits_into(slot, j):
            k0 = pl.multiple_of(j * t, t)
            for r in range(nsub):
                s_ref[slot, rows(r), :] = raw_logits(r, k0, t)

        def consume(slot, j):
            vt = v_ref[0, pl.ds(pl.multiple_of(j * t, t), t), :]
            kbv = kbias(j)
            for r in range(nsub):
                lsum, p = probs(past_chunks(s_ref[slot, rows(r), :], kbv), r)
                l_ref[rows(r), :] += lsum
                acc_ref[rows(r), :] += jnp.dot(p, vt, preferred_element_type=F32)

        def consume_diag(slot):
            for r in range(nsub):
                width = (r + 1) * rs
                lsum, p = probs(diag_chunks(r, s_ref[slot, rows(r), :]), r)
                l_ref[rows(r), :] += lsum
                acc_ref[rows(r), :] += jnp.dot(p, v_ref[0, pl.ds(q0, width), :], preferred_element_type=F32)

        l_ref[...] = jnp.zeros(l_ref.shape, F32)
        acc_ref[...] = jnp.zeros(acc_ref.shape, F32)
        n_past = qi - j_start
        logits_into(0, j_start)

        def trip(j, tiles):
            for i in range(tiles):
                logits_into((i + 1) % 2, j + i + 1)
                consume(i % 2, j + i)

        def quad(i, carry):
            trip(j_start + 4 * i, 4)
            return carry

        lax.fori_loop(0, n_past // 4, quad, 0)

        @pl.when(n_past % 4 >= 2)
        def _():
            trip(j_start + (n_past // 4) * 4, 2)

        @pl.when(n_past % 2 == 1)
        def _():
            logits_into(1, qi)
            consume(0, qi - 1)
            consume_diag(1)

        @pl.when(n_past % 2 == 0)
        def _():
            consume_diag(0)

        l = jnp.sum(l_ref[...], axis=-1, keepdims=True)
        acc_ref[...] = acc_ref[...] * (1.0 / l)
        return jnp.min(l)

    mb_ref[...] = jnp.broadcast_to(m_bound, (t, LANES))
    l_min = accumulate()

    @pl.when(l_min < MIN_ROW_SUM)
    def _():
        exact_row_maxima()
        accumulate()


def _att_scratch(t, dv, streams=1):
    acc = (t, dv) if streams == 1 else (streams, t, dv)
    return [pltpu.VMEM((t, LANES), F32), pltpu.VMEM((t, LANES), F32), pltpu.VMEM((t, LANES), F32),
            pltpu.VMEM((2, t, t), F32), pltpu.VMEM(acc, F32)] + [pltpu.VMEM((1, 1), F32)] * streams


def _tile_bias(kbias, heads, nq, t):
    kb = kbias.reshape(kbias.shape[0], heads, nq, 1, t)
    return kb, jnp.max(kb, axis=(3, 4)).reshape(-1)


def _flash_kernel(*refs, t, nq, heads, kb_batched, has_bias):
    bi, h = pl.program_id(0), pl.program_id(1)
    if not has_bias:
        q_ref, k_ref, v_ref, o_ref, m_ref, mb_ref, l_ref, s_ref, acc_ref, kn2_ref = refs
        bias = None
    else:
        q_ref, k_ref, v_ref, kb_ref, kbmax_ref, o_ref, m_ref, mb_ref, l_ref, s_ref, acc_ref, kn2_ref = refs
        bias = (kb_ref, kbmax_ref, ((bi * heads if kb_batched else 0) + h) * nq, None)

    _key_norm2(k_ref, kn2_ref, t, nq)

    def q_tile(qi, carry):
        _attend(q_ref, k_ref, v_ref, kn2_ref, bias, qi, t=t, m_ref=m_ref, mb_ref=mb_ref, l_ref=l_ref,
                s_ref=s_ref, acc_ref=acc_ref)
        o_ref[0, pl.ds(pl.multiple_of(qi * t, t), t), :] = acc_ref[...].astype(o_ref.dtype)
        return carry

    lax.fori_loop(0, nq, q_tile, 0)


def _flash(q_arr, k_arr, v_arr, kbias, *, heads, dk, dv, q_blk, k_blk, v_blk, name):
    b, s, _ = q_arr.shape
    t = min(ATT_TILE, s)
    nq = s // t
    in_specs = [
        pl.BlockSpec((1, s, dk), lambda bi, h: (bi, 0, q_blk + h)),
        pl.BlockSpec((1, s, dk), lambda bi, h: (bi, 0, k_blk + h)),
        pl.BlockSpec((1, s, dv), lambda bi, h: (bi, 0, v_blk + h)),
    ]
    args = [q_arr, k_arr, v_arr]
    kb_batched = kbias is not None and kbias.shape[0] == b
    if kbias is not None:
        kb, kbmax = _tile_bias(kbias, heads, nq, t)
        kb_b = (lambda bi: bi) if kb_batched else (lambda bi: 0)
        in_specs += [pl.BlockSpec((1, 1, nq, 1, t), lambda bi, h: (kb_b(bi), h, 0, 0, 0)),
                     pl.BlockSpec(memory_space=pltpu.SMEM)]
        args += [kb, kbmax]
    return pl.pallas_call(
        functools.partial(_flash_kernel, t=t, nq=nq, heads=heads, kb_batched=kb_batched,
                          has_bias=kbias is not None),
        grid=(b, heads),
        in_specs=in_specs,
        out_specs=pl.BlockSpec((1, s, dv), lambda bi, h: (bi, 0, h)),
        out_shape=jax.ShapeDtypeStruct((b, s, heads * dv), BF16),
        scratch_shapes=_att_scratch(t, dv),
        compiler_params=_params(("parallel", "parallel")),
        name=name,
    )(*args)


def _diff_kernel(q1_ref, q2_ref, k1_ref, k2_ref, v_ref, kb_ref, kbmax_ref, ks_ref, lam_ref, g_ref, o_ref,
                 m_ref, mb_ref, l_ref, s_ref, acc_ref, kn2a_ref, kn2b_ref, *, t, nq, lam_init):
    h = pl.program_id(1)
    _key_norm2(k1_ref, kn2a_ref, t, nq)
    _key_norm2(k2_ref, kn2b_ref, t, nq)
    scratch = dict(m_ref=m_ref, mb_ref=mb_ref, l_ref=l_ref, s_ref=s_ref)
    bias = (kb_ref, kbmax_ref, h * nq, ks_ref[h])
    lam = lam_ref[...]
    lam_full = (jnp.exp(jnp.sum(lam[0:1] * lam[1:2], axis=-1, keepdims=True))
                - jnp.exp(jnp.sum(lam[2:3] * lam[3:4], axis=-1, keepdims=True)) + lam_init)

    def q_tile(qi, carry):
        _attend(q1_ref, k1_ref, v_ref, kn2a_ref, bias, qi, t=t, acc_ref=acc_ref.at[0], **scratch)
        _attend(q2_ref, k2_ref, v_ref, kn2b_ref, bias, qi, t=t, acc_ref=acc_ref.at[1], **scratch)
        d = acc_ref[0] - lam_full * acc_ref[1]
        o_ref[0, pl.ds(pl.multiple_of(qi * t, t), t), :] = (
            _rms(d, g_ref[...]) * (1.0 - lam_init)).astype(o_ref.dtype)
        return carry

    lax.fori_loop(0, nq, q_tile, 0)


def _diff_attention(proj, alibi, lam, subln, *, q_blk, k_blk, v_blk, lam_init):
    b, s, _ = proj.shape
    t = min(ATT_TILE, s)
    nq = s // t
    dv = 2 * HEAD_DIM
    kbias, kslope = alibi
    kb, kbmax = _tile_bias(kbias, DIFF_HEADS, nq, t)
    return pl.pallas_call(
        functools.partial(_diff_kernel, t=t, nq=nq, lam_init=lam_init),
        grid=(b, DIFF_HEADS),
        in_specs=[
            pl.BlockSpec((1, s, HEAD_DIM), lambda bi, h: (bi, 0, q_blk + 2 * h)),
            pl.BlockSpec((1, s, HEAD_DIM), lambda bi, h: (bi, 0, q_blk + 2 * h + 1)),
            pl.BlockSpec((1, s, HEAD_DIM), lambda bi, h: (bi, 0, k_blk + 2 * h)),
            pl.BlockSpec((1, s, HEAD_DIM), lambda bi, h: (bi, 0, k_blk + 2 * h + 1)),
            pl.BlockSpec((1, s, dv), lambda bi, h: (bi, 0, v_blk + h)),
            pl.BlockSpec((1, 1, nq, 1, t), lambda bi, h: (0, h, 0, 0, 0)),
            pl.BlockSpec(memory_space=pltpu.SMEM),
            pl.BlockSpec(memory_space=pltpu.SMEM),
            pl.BlockSpec((4, HEAD_DIM), lambda bi, h: (0, 0)),
            pl.BlockSpec((1, dv), lambda bi, h: (0, 0)),
        ],
        out_specs=pl.BlockSpec((1, s, dv), lambda bi, h: (bi, 0, h)),
        out_shape=jax.ShapeDtypeStruct((b, s, DIFF_HEADS * dv), BF16),
        scratch_shapes=_att_scratch(t, dv, streams=2),
        compiler_params=_params(("parallel", "parallel")),
        name="diff_attention",
    )(proj, proj, proj, proj, proj, kb, kbmax, kslope, lam, subln.reshape(1, dv))


def _mla_up_kernel(cq_ref, ckv_ref, kpe_ref, gq_ref, gkv_ref, wq_ref, wkv_ref, cos_ref, sin_ref,
                   qf_ref, kf_ref, v_ref, *, scale):
    cqn = _rms(cq_ref[...].astype(F32), gq_ref[...]).astype(BF16)
    ckvn = _rms(ckv_ref[...].astype(F32), gkv_ref[...]).astype(BF16)
    cos = cos_ref[...]
    sin = sin_ref[...]
    kpe = kpe_ref[...]
    kv = jnp.dot(ckvn, wkv_ref[...], preferred_element_type=F32)
    nk = MLA_HEADS * MLA_NOPE
    v_ref[...] = kv[:, nk:].astype(v_ref.dtype)
    for h in range(MLA_HEADS):
        q3 = jnp.dot(cqn, wq_ref[:, h * 3 * LANES:(h + 1) * 3 * LANES], preferred_element_type=F32)
        rot = q3[:, LANES:2 * LANES] * cos + q3[:, 2 * LANES:] * sin
        qf_ref[:, 2 * h * LANES:(2 * h + 1) * LANES] = (q3[:, :LANES] * scale).astype(qf_ref.dtype)
        qf_ref[:, (2 * h + 1) * LANES:(2 * h + 2) * LANES] = (rot * scale).astype(qf_ref.dtype)
        kf_ref[:, 2 * h * LANES:(2 * h + 1) * LANES] = kv[:, h * LANES:(h + 1) * LANES].astype(kf_ref.dtype)
        kf_ref[:, (2 * h + 1) * LANES:(2 * h + 2) * LANES] = kpe


def _mla_up(proj, kpe_r, gq, gkv, wq, wkv, cos_t, sin_t, *, tm=512):
    t = proj.shape[0]
    tm = min(tm, t)
    nrep = cos_t.shape[0] // tm
    scale = float((MLA_NOPE + MLA_ROPE) ** -0.5 * LOG2E)
    wide = 2 * LANES * MLA_HEADS
    row = lambda i: (i, 0)
    fixed = lambda i: (0, 0)
    return pl.pallas_call(
        functools.partial(_mla_up_kernel, scale=scale),
        grid=(t // tm,),
        in_specs=[
            pl.BlockSpec((tm, MLA_RANK), lambda i: (i, 0)),
            pl.BlockSpec((tm, MLA_RANK), lambda i: (i, 1)),
            pl.BlockSpec((tm, LANES), row),
            pl.BlockSpec((1, MLA_RANK), fixed),
            pl.BlockSpec((1, MLA_RANK), fixed),
            pl.BlockSpec(wq.shape, fixed),
            pl.BlockSpec(wkv.shape, fixed),
            pl.BlockSpec((tm, LANES), lambda i: (i % nrep, 0)),
            pl.BlockSpec((tm, LANES), lambda i: (i % nrep, 0)),
        ],
        out_specs=[pl.BlockSpec((tm, wide), row), pl.BlockSpec((tm, wide), row),
                   pl.BlockSpec((tm, MLA_HEADS * MLA_V), row)],
        out_shape=[jax.ShapeDtypeStruct((t, wide), BF16), jax.ShapeDtypeStruct((t, wide), BF16),
                   jax.ShapeDtypeStruct((t, MLA_HEADS * MLA_V), BF16)],
        compiler_params=_params(("parallel",)),
        name="mla_up",
    )(proj, proj, kpe_r, gq.reshape(1, -1), gkv.reshape(1, -1), wq, wkv, cos_t, sin_t)


def _kmean_kernel(k_ref, o_ref):
    j = pl.program_id(1)
    o_ref[0, pl.ds(j, 1), :] = jnp.mean(k_ref[0].astype(F32), axis=0, keepdims=True)


def _kmean(proj, *, k_blk_wide):
    b, s, _ = proj.shape
    nblk = s // MOBA_BLOCK
    w = MOBA_HEADS * HEAD_DIM
    return pl.pallas_call(
        _kmean_kernel,
        grid=(b, nblk),
        in_specs=[pl.BlockSpec((1, MOBA_BLOCK, w), lambda bi, j: (bi, j, k_blk_wide))],
        out_specs=pl.BlockSpec((1, nblk, w), lambda bi, j: (bi, 0, 0)),
        out_shape=jax.ShapeDtypeStruct((b, nblk, w), F32),
        compiler_params=_params(("parallel", "arbitrary")),
        name="moba_kmean",
    )(proj)


def _moba_kernel(q_ref, k_ref, v_ref, km_ref, kx_ref, kb_ref, kbmax_ref, ks_ref, o_ref,
                 m_ref, mb_ref, l_ref, s_ref, acc_ref, kn2_ref, qx_ref, *, t, nq, nblk):
    h = pl.program_id(1)
    bpt = t // MOBA_BLOCK
    _key_norm2(k_ref, kn2_ref, t, nq)
    km = km_ref[0]
    if nblk < LANES:
        km = jnp.concatenate([km, jnp.zeros((LANES - nblk, HEAD_DIM), F32)], axis=0)

    def q_tile(qi, carry):
        q0 = pl.multiple_of(qi * t, t)
        gate = lax.dot_general(q_ref[0, pl.ds(q0, t), :].astype(F32), km, _NT, preferred_element_type=F32,
                               precision=lax.Precision.HIGHEST)
        lane = lax.broadcasted_iota(jnp.int32, (t, LANES), 1)
        row = lax.broadcasted_iota(jnp.int32, (t, LANES), 0)
        own = bpt * qi + jnp.right_shift(row, int(math.log2(MOBA_BLOCK)))
        past = lane < own
        neg_inf = -jnp.inf
        g = jnp.where(past, gate, neg_inf)
        lane_f = lane.astype(F32)
        keep = lane < 0
        for _ in range(MOBA_TOPK):
            top = jnp.max(g, axis=-1, keepdims=True)
            first = jnp.min(jnp.where(g == top, lane_f, float(LANES)), axis=-1, keepdims=True)
            hit = (lane_f == first) & (top > neg_inf)
            keep = keep | hit
            g = jnp.where(hit, neg_inf, g)
        qx_ref[...] = jnp.where(past & jnp.logical_not(keep), BLOCK_OFF, 0.0).astype(BF16)

        _attend(q_ref, k_ref, v_ref, kn2_ref, (kb_ref, kbmax_ref, h * nq, ks_ref[h]), qi, t=t,
                m_ref=m_ref, mb_ref=mb_ref, l_ref=l_ref, s_ref=s_ref, acc_ref=acc_ref, extra=(qx_ref, kx_ref))
        o_ref[0, pl.ds(q0, t), :] = acc_ref[...].astype(o_ref.dtype)
        return carry

    lax.fori_loop(0, nq, q_tile, 0)


def _moba(proj, kmean, alibi, *, q_blk, k_blk, v_blk):
    b, s, _ = proj.shape
    t = min(ATT_TILE, s)
    nq = s // t
    nblk = s // MOBA_BLOCK
    kbias, kslope = alibi
    kb, kbmax = _tile_bias(kbias, MOBA_HEADS, nq, t)
    key_block = (jnp.arange(s)[:, None] // MOBA_BLOCK == jnp.arange(LANES)[None, :]).astype(BF16)
    return pl.pallas_call(
        functools.partial(_moba_kernel, t=t, nq=nq, nblk=nblk),
        grid=(b, MOBA_HEADS),
        in_specs=[
            pl.BlockSpec((1, s, HEAD_DIM), lambda bi, h: (bi, 0, q_blk + h)),
            pl.BlockSpec((1, s, HEAD_DIM), lambda bi, h: (bi, 0, k_blk + h)),
            pl.BlockSpec((1, s, HEAD_DIM), lambda bi, h: (bi, 0, v_blk + h)),
            pl.BlockSpec((1, nblk, HEAD_DIM), lambda bi, h: (bi, 0, h)),
            pl.BlockSpec((s, LANES), lambda bi, h: (0, 0)),
            pl.BlockSpec((1, 1, nq, 1, t), lambda bi, h: (0, h, 0, 0, 0)),
            pl.BlockSpec(memory_space=pltpu.SMEM),
            pl.BlockSpec(memory_space=pltpu.SMEM),
        ],
        out_specs=pl.BlockSpec((1, s, HEAD_DIM), lambda bi, h: (bi, 0, h)),
        out_shape=jax.ShapeDtypeStruct((b, s, MOBA_HEADS * HEAD_DIM), BF16),
        scratch_shapes=_att_scratch(t, HEAD_DIM) + [pltpu.VMEM((t, LANES), BF16)],
        compiler_params=_params(("parallel", "parallel")),
        name="moba_attention",
    )(proj, proj, proj, kmean, key_block, kb, kbmax, kslope)


def _outproj_kernel(a_ref, d_ref, wa_ref, wd_ref, g_ref, h_ref, o_ref):
    y = jnp.dot(a_ref[...], wa_ref[...], preferred_element_type=F32)
    y = y + jnp.dot(d_ref[...], wd_ref[...], preferred_element_type=F32)
    o_ref[...] = h_ref[...] + _rms(y, g_ref[...])


def _outproj(a, d, wa, wd, g, h, *, tm=512):
    t, dm = h.shape
    tm = min(tm, t)
    ka, kd = a.shape[1], d.shape[1]
    return pl.pallas_call(
        _outproj_kernel,
        grid=(t // tm,),
        in_specs=[pl.BlockSpec((tm, ka), lambda i: (i, 0)),
                  pl.BlockSpec((tm, kd), lambda i: (i, 0)),
                  pl.BlockSpec((ka, dm), lambda i: (0, 0)),
                  pl.BlockSpec((kd, dm), lambda i: (0, 0)),
                  pl.BlockSpec((1, dm), lambda i: (0, 0)),
                  pl.BlockSpec((tm, dm), lambda i: (i, 0))],
        out_specs=pl.BlockSpec((tm, dm), lambda i: (i, 0)),
        out_shape=jax.ShapeDtypeStruct((t, dm), F32),
        compiler_params=_params(("parallel",)),
        name="mixer_outproj",
    )(a, d, wa, wd, g.reshape(1, dm), h)


def _xattn_kernel(h_ref, g2_ref, wq_ref, mkv_ref, wo_ref, g3_ref, o_ref, *, scale):
    h = h_ref[0]
    xn = _rms(h, g2_ref[...]).astype(BF16)
    q = (jnp.dot(xn, wq_ref[...], preferred_element_type=F32) * scale).astype(BF16)
    w = XATTN_HEADS * HEAD_DIM
    outs = []
    for hd in range(XATTN_HEADS):
        mk = mkv_ref[0, :, hd * HEAD_DIM:(hd + 1) * HEAD_DIM]
        mv = mkv_ref[0, :, w + hd * HEAD_DIM:w + (hd + 1) * HEAD_DIM]
        s = lax.dot_general(q[:, hd * HEAD_DIM:(hd + 1) * HEAD_DIM], mk, _NT, preferred_element_type=F32)
        p = jnp.exp(s - jnp.max(s, axis=-1, keepdims=True))
        p = p / jnp.sum(p, axis=-1, keepdims=True)
        outs.append(jnp.dot(p.astype(BF16), mv, preferred_element_type=F32).astype(BF16))
    y = jnp.dot(jnp.concatenate(outs, axis=-1), wo_ref[...], preferred_element_type=F32)
    o_ref[0] = h + _rms(y, g3_ref[...])


def _xattn(h, g2, wq, mkv, wo, g3, *, tm=512):
    b, s, dm = h.shape
    tm = min(tm, s)
    fixed = lambda bi, i: (0, 0)
    return pl.pallas_call(
        functools.partial(_xattn_kernel, scale=float(HEAD_DIM ** -0.5)),
        grid=(b, s // tm),
        in_specs=[pl.BlockSpec((1, tm, dm), lambda bi, i: (bi, i, 0)),
                  pl.BlockSpec((1, dm), fixed),
                  pl.BlockSpec(wq.shape, fixed),
                  pl.BlockSpec((1,) + mkv.shape[1:], lambda bi, i: (bi, 0, 0)),
                  pl.BlockSpec(wo.shape, fixed),
                  pl.BlockSpec((1, dm), fixed)],
        out_specs=pl.BlockSpec((1, tm, dm), lambda bi, i: (bi, i, 0)),
        out_shape=jax.ShapeDtypeStruct((b, s, dm), F32),
        compiler_params=_params(("parallel", "parallel")),
        name="memory_xattn",
    )(h, g2.reshape(1, dm), wq, mkv, wo, g3.reshape(1, dm))


def _ffn_kernel(h_ref, g4_ref, w1_ref, w2_ref, g5_ref, o_ref, xn_ref, acc_ref):
    f = pl.program_id(1)
    last = pl.num_programs(1) - 1

    def partial_out(xn):
        u = jnp.maximum(jnp.dot(xn, w1_ref[...], preferred_element_type=F32), 0.0)
        return jnp.dot((u * u).astype(BF16), w2_ref[...], preferred_element_type=F32)

    @pl.when(f == 0)
    def _():
        xn = _rms(h_ref[...], g4_ref[...]).astype(BF16)
        xn_ref[...] = xn
        acc_ref[...] = partial_out(xn)

    @pl.when(jnp.logical_and(f > 0, f < last))
    def _():
        acc_ref[...] += partial_out(xn_ref[...])

    @pl.when(f == last)
    def _():
        y = acc_ref[...] + partial_out(xn_ref[...])
        o_ref[...] = h_ref[...] + _rms(y, g5_ref[...])


def _ffn(h, g4, w1, w2, g5, layer, *, tm=512, tf=1024):
    t, dm = h.shape
    tm = min(tm, t)
    f = w1.shape[2]
    assert t % tm == 0 and f % tf == 0 and f // tf >= 2, (t, tm, f, tf)
    return pl.pallas_call(
        _ffn_kernel,
        grid=(t // tm, f // tf),
        in_specs=[pl.BlockSpec((tm, dm), lambda i, j: (i, 0)),
                  pl.BlockSpec((1, dm), lambda i, j: (0, 0)),
                  pl.BlockSpec((None, dm, tf), lambda i, j: (layer, 0, j)),
                  pl.BlockSpec((None, tf, dm), lambda i, j: (layer, j, 0)),
                  pl.BlockSpec((1, dm), lambda i, j: (0, 0))],
        out_specs=pl.BlockSpec((tm, dm), lambda i, j: (i, 0)),
        out_shape=jax.ShapeDtypeStruct((t, dm), F32),
        scratch_shapes=[pltpu.VMEM((tm, dm), BF16), pltpu.VMEM((tm, dm), F32)],
        compiler_params=_params(("parallel", "arbitrary")),
        name="relu2_mlp",
    )(h, g4.reshape(1, dm), w1, w2, g5.reshape(1, dm))


def _alibi_key_bias(n_heads, seq):
    slopes = LOG2E * jnp.asarray([2.0 ** (-8.0 * (i + 1) / n_heads) for i in range(n_heads)], dtype=F32)
    return (slopes[:, None] * jnp.arange(seq, dtype=F32)[None, :])[None], slopes


def _rope_tables(seq):
    half = MLA_ROPE // 2
    inv = ROPE_THETA ** (-jnp.arange(0, MLA_ROPE, 2, dtype=F32) / MLA_ROPE)
    ang = jnp.arange(seq, dtype=F32)[:, None] * inv[None, :]
    cos, sin = jnp.cos(ang), jnp.sin(ang)
    zero = jnp.zeros((seq, LANES - 2 * half), F32)
    return jnp.concatenate([cos, cos, zero], axis=-1), jnp.concatenate([-sin, sin, zero], axis=-1)


def _rope_pair_columns(w_t1, w_t2):
    zero = jnp.zeros((w_t1.shape[0], LANES - 2 * w_t1.shape[1]), w_t1.dtype)
    return jnp.concatenate([w_t1, w_t2, zero, w_t2, w_t1, zero], axis=-1)


def kernel(x, mem, mem_norm, mem_wkv, norms, xattn_wq, xattn_wo, ffn_w1, ffn_w2, ab_w_in, ab_w_out, fox_b_f, diff_lambda, diff_subln, cd_w_in, cd_w_out, mla_q_norm, mla_kv_norm, mla_w_uq, mla_w_ukv):
    b, s, dm = x.shape
    t = b * s
    depth = norms.shape[0]
    w1_all, w2_all = ffn_w1.astype(BF16), ffn_w2.astype(BF16)
    q_scale = HEAD_DIM ** -0.5 * LOG2E
    half = MLA_ROPE // 2
    cos_t, sin_t = _rope_tables(s)

    mkv = _norm_matmul(mem.reshape(-1, dm), mem_norm, mem_wkv.astype(BF16)).reshape(b, mem.shape[1], -1)

    h = x.reshape(t, dm)
    for i in range(depth):
        n = norms[i]
        j = i // 2
        if i % 2 == 0:
            w = ab_w_in[j]
            fw = FOX_HEADS * HEAD_DIM
            dw = DIFF_HEADS * 2 * HEAD_DIM
            o_g = 3 * fw
            o_dq = o_g + FOX_HEADS
            w_main = jnp.concatenate([w[:, :fw] * q_scale, w[:, fw:o_g],
                                      w[:, o_dq:o_dq + dw] * q_scale, w[:, o_dq + dw:]], axis=-1).astype(BF16)
            w_gate = jnp.concatenate([w[:, o_g:o_dq].T, jnp.zeros((8, dm), F32)], axis=0).astype(BF16)
            proj, gate_t = _inproj(h, n[0], w_main, w_gate, "gate_t")
            proj = proj.reshape(b, s, -1)
            fox_kb = _fox_bias(gate_t, fox_b_f[j], b, s)
            a = _flash(proj, proj, proj, fox_kb, heads=FOX_HEADS, dk=HEAD_DIM, dv=HEAD_DIM,
                       q_blk=0, k_blk=FOX_HEADS, v_blk=2 * FOX_HEADS, name="fox_attention")
            lam_init = 0.8 - 0.6 * math.exp(-0.3 * i)
            d = _diff_attention(proj, _alibi_key_bias(DIFF_HEADS, s), diff_lambda[j], diff_subln[j],
                                q_blk=3 * FOX_HEADS, k_blk=3 * FOX_HEADS + 2 * DIFF_HEADS,
                                v_blk=(3 * fw + 2 * dw) // (2 * HEAD_DIM), lam_init=lam_init)
            wo = ab_w_out[j].astype(BF16)
            h = _outproj(a.reshape(t, -1), d.reshape(t, -1), wo[:fw], wo[fw:], n[1], h)
        else:
            w = cd_w_in[j]
            r2 = 2 * MLA_RANK
            o_m = r2 + MLA_ROPE
            mw = MOBA_HEADS * HEAD_DIM
            w_main = jnp.concatenate([w[:, :r2], w[:, o_m:o_m + mw] * q_scale, w[:, o_m + mw:]],
                                     axis=-1).astype(BF16)
            w_kpe = _rope_pair_columns(w[:, r2:r2 + half], w[:, r2 + half:o_m]).astype(BF16)
            proj, kpe_r = _inproj(h, n[0], w_main, w_kpe, "rope", (cos_t, sin_t))
            wq = mla_w_uq[j].reshape(MLA_RANK, MLA_HEADS, MLA_NOPE + MLA_ROPE)
            wq = jnp.concatenate(
                [jnp.concatenate([wq[:, hd, :MLA_NOPE],
                                  _rope_pair_columns(wq[:, hd, MLA_NOPE:MLA_NOPE + half], wq[:, hd, MLA_NOPE + half:])],
                                 axis=-1) for hd in range(MLA_HEADS)], axis=-1).astype(BF16)
            wkv = mla_w_ukv[j].reshape(MLA_RANK, MLA_HEADS, MLA_NOPE + MLA_V)
            wkv = jnp.concatenate([wkv[:, :, :MLA_NOPE].reshape(MLA_RANK, -1),
                                   wkv[:, :, MLA_NOPE:].reshape(MLA_RANK, -1)], axis=-1).astype(BF16)
            qf, kf, v = _mla_up(proj, kpe_r, mla_q_norm[j], mla_kv_norm[j], wq, wkv, cos_t, sin_t)
            c = _flash(qf.reshape(b, s, -1), kf.reshape(b, s, -1), v.reshape(b, s, -1), None,
                       heads=MLA_HEADS, dk=2 * LANES, dv=MLA_V, q_blk=0, k_blk=0, v_blk=0, name="mla_attention")
            proj = proj.reshape(b, s, -1)
            kmean = _kmean(proj, k_blk_wide=(r2 + mw) // mw)
            dout = _moba(proj, kmean, _alibi_key_bias(MOBA_HEADS, s),
                         q_blk=r2 // HEAD_DIM, k_blk=(r2 + mw) // HEAD_DIM, v_blk=(r2 + 2 * mw) // HEAD_DIM)
            wo = cd_w_out[j].astype(BF16)
            cw = MLA_HEADS * MLA_V
            h = _outproj(c.reshape(t, -1), dout.reshape(t, -1), wo[:cw], wo[cw:], n[1], h)
        h = _xattn(h.reshape(b, s, dm), n[2], xattn_wq[i].astype(BF16), mkv, xattn_wo[i].astype(BF16), n[3])
        h = _ffn(h.reshape(t, dm), n[4], w1_all, w2_all, n[5], i)
    return h.reshape(b, s, dm)
```

```python
import functools
import math

import jax
import jax.numpy as jnp
from jax import lax
from jax.experimental import pallas as pl
from jax.experimental.pallas import tpu as pltpu

F32 = jnp.float32
BF16 = jnp.bfloat16

NORM_EPS = 1e-6
HEAD_DIM = 128
FOX_HEADS = 8
DIFF_HEADS = 4
MLA_HEADS = 8
MLA_NOPE = 128
MLA_ROPE = 64
MLA_V = 128
MLA_RANK = 512
ROPE_THETA = 10000.0
MOBA_HEADS = 8
MOBA_BLOCK = 256
MOBA_TOPK = 3
XATTN_HEADS = 4
LANES = 128
MASKED = -1e30
VMEM_LIMIT = 48 * 1024 * 1024
LOG2E = math.log2(math.e)
SKIP_LOG2 = 160.0
MIN_ROW_SUM = 2.0 ** -64
BLOCK_OFF = -2.0 ** 100
ATT_TILE = 512
ATT_ROWS = 256

_NT = (((1,), (1,)), ((), ()))


def _params(sem, fuse_inputs=None):
    return pltpu.CompilerParams(dimension_semantics=sem, vmem_limit_bytes=VMEM_LIMIT,
                                allow_input_fusion=fuse_inputs)


def _rms(x, g):
    ms = jnp.mean(x * x, axis=-1, keepdims=True)
    return x * lax.rsqrt(ms + NORM_EPS) * g


def _inproj_kernel(x_ref, g_ref, w_ref, aux_w_ref, *rest, aux_mode):
    if aux_mode == "rope":
        cos_ref, sin_ref, o_ref, aux_ref, xn_ref = rest
    else:
        o_ref, aux_ref, xn_ref = rest

    @pl.when(pl.program_id(1) == 0)
    def _():
        xn = _rms(x_ref[...], g_ref[...]).astype(BF16)
        xn_ref[...] = xn
        if aux_mode == "gate_t":
            r = lax.dot_general(aux_w_ref[...], xn, _NT, preferred_element_type=F32)
            aux_ref[...] = r[:8]
        else:
            ab = jnp.dot(xn, aux_w_ref[...], preferred_element_type=F32)
            aux_ref[...] = (ab[:, :LANES] * cos_ref[...] + ab[:, LANES:] * sin_ref[...]).astype(aux_ref.dtype)
        o_ref[...] = jnp.dot(xn, w_ref[...], preferred_element_type=F32).astype(o_ref.dtype)

    @pl.when(pl.program_id(1) > 0)
    def _():
        o_ref[...] = jnp.dot(xn_ref[...], w_ref[...], preferred_element_type=F32).astype(o_ref.dtype)


def _inproj(x, g, w, aux_w, aux_mode, tables=None, *, tm=1024, tn=1024):
    t, k = x.shape
    n = w.shape[1]
    tm = min(tm, t)
    assert t % tm == 0 and n % tn == 0, (t, tm, n, tn)
    grid = (t // tm, n // tn)
    in_specs = [
        pl.BlockSpec((tm, k), lambda i, j: (i, 0)),
        pl.BlockSpec((1, k), lambda i, j: (0, 0)),
        pl.BlockSpec((k, tn), lambda i, j: (0, j)),
        pl.BlockSpec(aux_w.shape, lambda i, j: (0, 0)),
    ]
    args = [x, g.reshape(1, k), w, aux_w]
    if aux_mode == "rope":
        cos_t, sin_t = tables
        nrep = cos_t.shape[0] // tm
        in_specs += [pl.BlockSpec((tm, LANES), lambda i, j: (i % nrep, 0))] * 2
        args += [cos_t, sin_t]
        aux_shape = jax.ShapeDtypeStruct((t, LANES), BF16)
        aux_spec = pl.BlockSpec((tm, LANES), lambda i, j: (i, 0))
    else:
        aux_shape = jax.ShapeDtypeStruct((8, t), F32)
        aux_spec = pl.BlockSpec((8, tm), lambda i, j: (0, i))
    return pl.pallas_call(
        functools.partial(_inproj_kernel, aux_mode=aux_mode),
        grid=grid,
        in_specs=in_specs,
        out_specs=[pl.BlockSpec((tm, tn), lambda i, j: (i, j)), aux_spec],
        out_shape=[jax.ShapeDtypeStruct((t, n), BF16), aux_shape],
        scratch_shapes=[pltpu.VMEM((tm, k), BF16)],
        compiler_params=_params(("parallel", "arbitrary")),
        name="inproj_" + aux_mode,
    )(*args)


def _norm_matmul_kernel(x_ref, g_ref, w_ref, o_ref):
    xn = _rms(x_ref[...], g_ref[...]).astype(BF16)
    o_ref[...] = jnp.dot(xn, w_ref[...], preferred_element_type=F32).astype(o_ref.dtype)


def _norm_matmul(x, g, w, *, tm=256):
    t, k = x.shape
    n = w.shape[1]
    return pl.pallas_call(
        _norm_matmul_kernel,
        grid=(t // tm,),
        in_specs=[pl.BlockSpec((tm, k), lambda i: (i, 0)),
                  pl.BlockSpec((1, k), lambda i: (0, 0)),
                  pl.BlockSpec((k, n), lambda i: (0, 0))],
        out_specs=pl.BlockSpec((tm, n), lambda i: (i, 0)),
        out_shape=jax.ShapeDtypeStruct((t, n), BF16),
        compiler_params=_params(("parallel",)),
        name="mem_kv_proj",
    )(x, g.reshape(1, k), w)


def _fox_bias_kernel(g_ref, b_ref, o_ref, carry_ref, *, tc):
    @pl.when(pl.program_id(1) == 0)
    def _():
        carry_ref[...] = jnp.zeros_like(carry_ref)

    z = g_ref[...] + b_ref[...]
    logf = jnp.minimum(z, 0.0) - jnp.log(1.0 + jnp.exp(-jnp.abs(z)))
    upper = (lax.broadcasted_iota(jnp.int32, (tc, tc), 0)
             <= lax.broadcasted_iota(jnp.int32, (tc, tc), 1)).astype(F32)
    cum = jnp.dot(logf, upper, preferred_element_type=F32, precision=lax.Precision.HIGHEST) + carry_ref[...]
    carry_ref[...] = cum[:, tc - 1:tc]
    o_ref[0] = cum * (-LOG2E)


def _fox_bias(gate_t, b_f, batch, seq, *, tc=256):
    ns = seq // tc
    return pl.pallas_call(
        functools.partial(_fox_bias_kernel, tc=tc),
        grid=(batch, ns),
        in_specs=[pl.BlockSpec((8, tc), lambda b, s: (0, b * ns + s)),
                  pl.BlockSpec((8, 1), lambda b, s: (0, 0))],
        out_specs=pl.BlockSpec((1, 8, tc), lambda b, s: (b, 0, s)),
        out_shape=jax.ShapeDtypeStruct((batch, 8, seq), F32),
        scratch_shapes=[pltpu.VMEM((8, 1), F32)],
        compiler_params=_params(("parallel", "arbitrary")),
        name="fox_gate_cumsum",
    )(gate_t, b_f.reshape(8, 1))


def _key_norm2(k_ref, kn2_ref, t, nq):
    def body(j, mx):
        kk = k_ref[0, pl.ds(pl.multiple_of(j * t, t), t), :].astype(F32)
        return jnp.maximum(mx, jnp.sum(kk * kk, axis=-1, keepdims=True))
    mx = lax.fori_loop(0, nq, body, jnp.zeros((t, 1), F32))
    kn2_ref[...] = jnp.max(mx, axis=0, keepdims=True)


def _attend(q_ref, k_ref, v_ref, kn2_ref, bias, qi, *, t, m_ref, mb_ref, l_ref, s_ref, acc_ref, extra=None):
    rs = min(ATT_ROWS, t)
    nsub, nch = t // rs, t // LANES
    q0 = pl.multiple_of(qi * t, t)

    def rows(r):
        return slice(r * rs, (r + 1) * rs)

    if bias is not None:
        kb_ref, kbmax_ref, kb_base, kb_slope = bias
        shift = jnp.max(kb_ref[0, 0, qi], axis=-1, keepdims=True)

        def kbias(j):
            return kb_ref[0, 0, j] - shift
    else:
        def kbias(j):
            return None

    def chunk(s, kbv, c):
        sc = s[:, c * LANES:(c + 1) * LANES]
        return sc if kbv is None else sc + kbv[:, c * LANES:(c + 1) * LANES]

    def raw_logits(r, k0, width):
        lhs = q_ref[0, pl.ds(pl.multiple_of(q0 + r * rs, rs), rs), :]
        rhs = k_ref[0, pl.ds(k0, width), :]
        if extra is not None:
            qx_ref, kx_ref = extra
            lhs = jnp.concatenate([lhs, qx_ref[rows(r), :]], axis=-1)
            rhs = jnp.concatenate([rhs, kx_ref[pl.ds(k0, width), :]], axis=-1)
        return lax.dot_general(lhs, rhs, _NT, preferred_element_type=F32)

    def diag_chunks(r, s=None):
        width = (r + 1) * rs
        s = raw_logits(r, q0, width) if s is None else s
        kbv = kbias(qi)
        lo = r * rs
        out = []
        for c in range(width // LANES):
            sc = chunk(s, kbv, c)
            if (c + 1) * LANES - 1 > lo:
                row = lo + lax.broadcasted_iota(jnp.int32, (rs, LANES), 0)
                col = c * LANES + lax.broadcasted_iota(jnp.int32, (rs, LANES), 1)
                sc = jnp.where(col <= row, sc, MASKED)
            out.append(sc)
        return out

    def past_chunks(s, kbv):
        return [chunk(s, kbv, c) for c in range(nch)]

    qf = q_ref[0, pl.ds(q0, t), :].astype(F32)
    qk_bound = jnp.sqrt(jnp.sum(qf * qf, axis=-1, keepdims=True) * kn2_ref[...])
    if bias is not None:
        if kb_slope is not None:
            kb_self = kb_slope * (q0 + lax.broadcasted_iota(jnp.int32, (t, 1), 0)).astype(F32)
            kb_own = kb_self
        else:
            kb_d = kb_ref[0, 0, qi]
            row = lax.broadcasted_iota(jnp.int32, (LANES, LANES), 0)
            col = lax.broadcasted_iota(jnp.int32, (LANES, LANES), 1)
            selfs, owns = [], []
            before = jnp.full((1, 1), -jnp.inf, F32)
            for c in range(nch):
                kbc = kb_d[:, c * LANES:(c + 1) * LANES]
                selfs.append(jnp.sum(jnp.where(col == row, kbc, 0.0), axis=-1, keepdims=True))
                owns.append(jnp.maximum(jnp.max(jnp.where(col <= row, kbc, -jnp.inf), axis=-1, keepdims=True),
                                        before))
                before = jnp.maximum(before, jnp.max(kbc, axis=-1, keepdims=True))
            kb_self = jnp.concatenate(selfs, axis=0)
            kb_own = jnp.concatenate(owns, axis=0)
        self_logit = jnp.sum(qf * k_ref[0, pl.ds(q0, t), :].astype(F32), axis=-1, keepdims=True) + (kb_self - shift)
        m_min = jnp.min(self_logit, axis=0, keepdims=True)
        thresh = jnp.max(m_min - SKIP_LOG2 - jnp.max(qk_bound, axis=0, keepdims=True) + shift)
        j_start = lax.while_loop(lambda j: jnp.logical_and(j < qi, kbmax_ref[kb_base + j] < thresh),
                                 lambda j: j + 1, jnp.int32(0))
        kb_prev = lax.fori_loop(0, qi, lambda j, m: jnp.maximum(m, kbmax_ref[kb_base + j]), jnp.float32(-jnp.inf))
        m_bound = qk_bound + (jnp.maximum(kb_own, kb_prev) - shift)
    else:
        j_start = jnp.int32(0)
        m_bound = qk_bound

    def exact_row_maxima():
        for r in range(nsub):
            m_ref[rows(r), :] = functools.reduce(jnp.maximum, diag_chunks(r))

        def tile(j, carry):
            k0 = pl.multiple_of(j * t, t)
            kbv = kbias(j)
            for r in range(nsub):
                m_ref[rows(r), :] = functools.reduce(jnp.maximum, past_chunks(raw_logits(r, k0, t), kbv),
                                                     m_ref[rows(r), :])
            return carry

        lax.fori_loop(j_start, qi, tile, 0)
        mb_ref[...] = jnp.broadcast_to(jnp.max(m_ref[...], axis=-1, keepdims=True), (t, LANES))

    def probs(chunks, r):
        mb = mb_ref[rows(r), :]
        ps = [jnp.exp2(sc - mb) for sc in chunks]
        p = ps[0] if len(ps) == 1 else jnp.concatenate(ps, axis=-1)
        return functools.reduce(jnp.add, ps), p.astype(BF16)

    def accumulate():
        def logits_into(slot, j):
            k0 = pl.multiple_of(j * t, t)
            for r in range(nsub):
                s_ref[slot, rows(r), :] = raw_logits(r, k0, t)

        def consume(slot, j):
            vt = v_ref[0, pl.ds(pl.multiple_of(j * t, t), t), :]
            kbv = kbias(j)
            for r in range(nsub):
                lsum, p = probs(past_chunks(s_ref[slot, rows(r), :], kbv), r)
                l_ref[rows(r), :] += lsum
                acc_ref[rows(r), :] += jnp.dot(p, vt, preferred_element_type=F32)

        def consume_diag(slot):
            for r in range(nsub):
                width = (r + 1) * rs
                lsum, p = probs(diag_chunks(r, s_ref[slot, rows(r), :]), r)
                l_ref[rows(r), :] += lsum
                acc_ref[rows(r), :] += jnp.dot(p, v_ref[0, pl.ds(q0, width), :], preferred_element_type=F32)

        l_ref[...] = jnp.zeros(l_ref.shape, F32)
        acc_ref[...] = jnp.zeros(acc_ref.shape, F32)
        n_past = qi - j_start
        logits_into(0, j_start)

        def trip(j, tiles):
            for i in range(tiles):
                logits_into((i + 1) % 2, j + i + 1)
                consume(i % 2, j + i)

        def quad(i, carry):
            trip(j_start + 4 * i, 4)
            return carry

        lax.fori_loop(0, n_past // 4, quad, 0)

        @pl.when(n_past % 4 >= 2)
        def _():
            trip(j_start + (n_past // 4) * 4, 2)

        @pl.when(n_past % 2 == 1)
        def _():
            logits_into(1, qi)
            consume(0, qi - 1)
            consume_diag(1)

        @pl.when(n_past % 2 == 0)
        def _():
            consume_diag(0)

        l = jnp.sum(l_ref[...], axis=-1, keepdims=True)
        acc_ref[...] = acc_ref[...] * (1.0 / l)
        return jnp.min(l)

    mb_ref[...] = jnp.broadcast_to(m_bound, (t, LANES))
    l_min = accumulate()

    @pl.when(l_min < MIN_ROW_SUM)
    def _():
        exact_row_maxima()
        accumulate()


def _att_scratch(t, dv, streams=1):
    acc = (t, dv) if streams == 1 else (streams, t, dv)
    return [pltpu.VMEM((t, LANES), F32), pltpu.VMEM((t, LANES), F32), pltpu.VMEM((t, LANES), F32),
            pltpu.VMEM((2, t, t), F32), pltpu.VMEM(acc, F32)] + [pltpu.VMEM((1, 1), F32)] * streams


def _tile_bias(kbias, heads, nq, t):
    kb = kbias.reshape(kbias.shape[0], heads, nq, 1, t)
    return kb, jnp.max(kb, axis=(3, 4)).reshape(-1)


def _flash_kernel(*refs, t, nq, heads, kb_batched, has_bias):
    bi, h = pl.program_id(0), pl.program_id(1)
    if not has_bias:
        q_ref, k_ref, v_ref, o_ref, m_ref, mb_ref, l_ref, s_ref, acc_ref, kn2_ref = refs
        bias = None
    else:
        q_ref, k_ref, v_ref, kb_ref, kbmax_ref, o_ref, m_ref, mb_ref, l_ref, s_ref, acc_ref, kn2_ref = refs
        bias = (kb_ref, kbmax_ref, ((bi * heads if kb_batched else 0) + h) * nq, None)

    _key_norm2(k_ref, kn2_ref, t, nq)

    def q_tile(qi, carry):
        _attend(q_ref, k_ref, v_ref, kn2_ref, bias, qi, t=t, m_ref=m_ref, mb_ref=mb_ref, l_ref=l_ref,
                s_ref=s_ref, acc_ref=acc_ref)
        o_ref[0, pl.ds(pl.multiple_of(qi * t, t), t), :] = acc_ref[...].astype(o_ref.dtype)
        return carry

    lax.fori_loop(0, nq, q_tile, 0)


def _flash(q_arr, k_arr, v_arr, kbias, *, heads, dk, dv, q_blk, k_blk, v_blk, name):
    b, s, _ = q_arr.shape
    t = min(ATT_TILE, s)
    nq = s // t
    in_specs = [
        pl.BlockSpec((1, s, dk), lambda bi, h: (bi, 0, q_blk + h)),
        pl.BlockSpec((1, s, dk), lambda bi, h: (bi, 0, k_blk + h)),
        pl.BlockSpec((1, s, dv), lambda bi, h: (bi, 0, v_blk + h)),
    ]
    args = [q_arr, k_arr, v_arr]
    kb_batched = kbias is not None and kbias.shape[0] == b
    if kbias is not None:
        kb, kbmax = _tile_bias(kbias, heads, nq, t)
        kb_b = (lambda bi: bi) if kb_batched else (lambda bi: 0)
        in_specs += [pl.BlockSpec((1, 1, nq, 1, t), lambda bi, h: (kb_b(bi), h, 0, 0, 0)),
                     pl.BlockSpec(memory_space=pltpu.SMEM)]
        args += [kb, kbmax]
    return pl.pallas_call(
        functools.partial(_flash_kernel, t=t, nq=nq, heads=heads, kb_batched=kb_batched,
                          has_bias=kbias is not None),
        grid=(b, heads),
        in_specs=in_specs,
        out_specs=pl.BlockSpec((1, s, dv), lambda bi, h: (bi, 0, h)),
        out_shape=jax.ShapeDtypeStruct((b, s, heads * dv), BF16),
        scratch_shapes=_att_scratch(t, dv),
        compiler_params=_params(("parallel", "parallel")),
        name=name,
    )(*args)


def _diff_kernel(q1_ref, q2_ref, k1_ref, k2_ref, v_ref, kb_ref, kbmax_ref, ks_ref, lam_ref, g_ref, o_ref,
                 m_ref, mb_ref, l_ref, s_ref, acc_ref, kn2a_ref, kn2b_ref, *, t, nq, lam_init):
    h = pl.program_id(1)
    _key_norm2(k1_ref, kn2a_ref, t, nq)
    _key_norm2(k2_ref, kn2b_ref, t, nq)
    scratch = dict(m_ref=m_ref, mb_ref=mb_ref, l_ref=l_ref, s_ref=s_ref)
    bias = (kb_ref, kbmax_ref, h * nq, ks_ref[h])
    lam = lam_ref[...]
    lam_full = (jnp.exp(jnp.sum(lam[0:1] * lam[1:2], axis=-1, keepdims=True))
                - jnp.exp(jnp.sum(lam[2:3] * lam[3:4], axis=-1, keepdims=True)) + lam_init)

    def q_tile(qi, carry):
        _attend(q1_ref, k1_ref, v_ref, kn2a_ref, bias, qi, t=t, acc_ref=acc_ref.at[0], **scratch)
        _attend(q2_ref, k2_ref, v_ref, kn2b_ref, bias, qi, t=t, acc_ref=acc_ref.at[1], **scratch)
        d = acc_ref[0] - lam_full * acc_ref[1]
        o_ref[0, pl.ds(pl.multiple_of(qi * t, t), t), :] = (
            _rms(d, g_ref[...]) * (1.0 - lam_init)).astype(o_ref.dtype)
        return carry

    lax.fori_loop(0, nq, q_tile, 0)


def _diff_attention(proj, alibi, lam, subln, *, q_blk, k_blk, v_blk, lam_init):
    b, s, _ = proj.shape
    t = min(ATT_TILE, s)
    nq = s // t
    dv = 2 * HEAD_DIM
    kbias, kslope = alibi
    kb, kbmax = _tile_bias(kbias, DIFF_HEADS, nq, t)
    return pl.pallas_call(
        functools.partial(_diff_kernel, t=t, nq=nq, lam_init=lam_init),
        grid=(b, DIFF_HEADS),
        in_specs=[
            pl.BlockSpec((1, s, HEAD_DIM), lambda bi, h: (bi, 0, q_blk + 2 * h)),
            pl.BlockSpec((1, s, HEAD_DIM), lambda bi, h: (bi, 0, q_blk + 2 * h + 1)),
            pl.BlockSpec((1, s, HEAD_DIM), lambda bi, h: (bi, 0, k_blk + 2 * h)),
            pl.BlockSpec((1, s, HEAD_DIM), lambda bi, h: (bi, 0, k_blk + 2 * h + 1)),
            pl.BlockSpec((1, s, dv), lambda bi, h: (bi, 0, v_blk + h)),
            pl.BlockSpec((1, 1, nq, 1, t), lambda bi, h: (0, h, 0, 0, 0)),
            pl.BlockSpec(memory_space=pltpu.SMEM),
            pl.BlockSpec(memory_space=pltpu.SMEM),
            pl.BlockSpec((4, HEAD_DIM), lambda bi, h: (0, 0)),
            pl.BlockSpec((1, dv), lambda bi, h: (0, 0)),
        ],
        out_specs=pl.BlockSpec((1, s, dv), lambda bi, h: (bi, 0, h)),
        out_shape=jax.ShapeDtypeStruct((b, s, DIFF_HEADS * dv), BF16),
        scratch_shapes=_att_scratch(t, dv, streams=2),
        compiler_params=_params(("parallel", "parallel")),
        name="diff_attention",
    )(proj, proj, proj, proj, proj, kb, kbmax, kslope, lam, subln.reshape(1, dv))


def _mla_up_kernel(cq_ref, ckv_ref, kpe_ref, gq_ref, gkv_ref, wq_ref, wkv_ref, cos_ref, sin_ref,
                   qf_ref, kf_ref, v_ref, *, scale):
    cqn = _rms(cq_ref[...].astype(F32), gq_ref[...]).astype(BF16)
    ckvn = _rms(ckv_ref[...].astype(F32), gkv_ref[...]).astype(BF16)
    cos = cos_ref[...]
    sin = sin_ref[...]
    kpe = kpe_ref[...]
    kv = jnp.dot(ckvn, wkv_ref[...], preferred_element_type=F32)
    nk = MLA_HEADS * MLA_NOPE
    v_ref[...] = kv[:, nk:].astype(v_ref.dtype)
    for h in range(MLA_HEADS):
        q3 = jnp.dot(cqn, wq_ref[:, h * 3 * LANES:(h + 1) * 3 * LANES], preferred_element_type=F32)
        rot = q3[:, LANES:2 * LANES] * cos + q3[:, 2 * LANES:] * sin
        qf_ref[:, 2 * h * LANES:(2 * h + 1) * LANES] = (q3[:, :LANES] * scale).astype(qf_ref.dtype)
        qf_ref[:, (2 * h + 1) * LANES:(2 * h + 2) * LANES] = (rot * scale).astype(qf_ref.dtype)
        kf_ref[:, 2 * h * LANES:(2 * h + 1) * LANES] = kv[:, h * LANES:(h + 1) * LANES].astype(kf_ref.dtype)
        kf_ref[:, (2 * h + 1) * LANES:(2 * h + 2) * LANES] = kpe


def _mla_up(proj, kpe_r, gq, gkv, wq, wkv, cos_t, sin_t, *, tm=512):
    t = proj.shape[0]
    tm = min(tm, t)
    nrep = cos_t.shape[0] // tm
    scale = float((MLA_NOPE + MLA_ROPE) ** -0.5 * LOG2E)
    wide = 2 * LANES * MLA_HEADS
    row = lambda i: (i, 0)
    fixed = lambda i: (0, 0)
    return pl.pallas_call(
        functools.partial(_mla_up_kernel, scale=scale),
        grid=(t // tm,),
        in_specs=[
            pl.BlockSpec((tm, MLA_RANK), lambda i: (i, 0)),
            pl.BlockSpec((tm, MLA_RANK), lambda i: (i, 1)),
            pl.BlockSpec((tm, LANES), row),
            pl.BlockSpec((1, MLA_RANK), fixed),
            pl.BlockSpec((1, MLA_RANK), fixed),
            pl.BlockSpec(wq.shape, fixed),
            pl.BlockSpec(wkv.shape, fixed),
            pl.BlockSpec((tm, LANES), lambda i: (i % nrep, 0)),
            pl.BlockSpec((tm, LANES), lambda i: (i % nrep, 0)),
        ],
        out_specs=[pl.BlockSpec((tm, wide), row), pl.BlockSpec((tm, wide), row),
                   pl.BlockSpec((tm, MLA_HEADS * MLA_V), row)],
        out_shape=[jax.ShapeDtypeStruct((t, wide), BF16), jax.ShapeDtypeStruct((t, wide), BF16),
                   jax.ShapeDtypeStruct((t, MLA_HEADS * MLA_V), BF16)],
        compiler_params=_params(("parallel",)),
        name="mla_up",
    )(proj, proj, kpe_r, gq.reshape(1, -1), gkv.reshape(1, -1), wq, wkv, cos_t, sin_t)


def _kmean_kernel(k_ref, o_ref):
    j = pl.program_id(1)
    o_ref[0, pl.ds(j, 1), :] = jnp.mean(k_ref[0].astype(F32), axis=0, keepdims=True)


def _kmean(proj, *, k_blk_wide):
    b, s, _ = proj.shape
    nblk = s // MOBA_BLOCK
    w = MOBA_HEADS * HEAD_DIM
    return pl.pallas_call(
        _kmean_kernel,
        grid=(b, nblk),
        in_specs=[pl.BlockSpec((1, MOBA_BLOCK, w), lambda bi, j: (bi, j, k_blk_wide))],
        out_specs=pl.BlockSpec((1, nblk, w), lambda bi, j: (bi, 0, 0)),
        out_shape=jax.ShapeDtypeStruct((b, nblk, w), F32),
        compiler_params=_params(("parallel", "arbitrary")),
        name="moba_kmean",
    )(proj)


def _moba_kernel(q_ref, k_ref, v_ref, km_ref, kx_ref, kb_ref, kbmax_ref, ks_ref, o_ref,
                 m_ref, mb_ref, l_ref, s_ref, acc_ref, kn2_ref, qx_ref, *, t, nq, nblk):
    h = pl.program_id(1)
    bpt = t // MOBA_BLOCK
    _key_norm2(k_ref, kn2_ref, t, nq)
    km = km_ref[0]
    if nblk < LANES:
        km = jnp.concatenate([km, jnp.zeros((LANES - nblk, HEAD_DIM), F32)], axis=0)

    def q_tile(qi, carry):
        q0 = pl.multiple_of(qi * t, t)
        gate = lax.dot_general(q_ref[0, pl.ds(q0, t), :].astype(F32), km, _NT, preferred_element_type=F32,
                               precision=lax.Precision.HIGHEST)
        lane = lax.broadcasted_iota(jnp.int32, (t, LANES), 1)
        row = lax.broadcasted_iota(jnp.int32, (t, LANES), 0)
        own = bpt * qi + jnp.right_shift(row, int(math.log2(MOBA_BLOCK)))
        past = lane < own
        neg_inf = -jnp.inf
        g = jnp.where(past, gate, neg_inf)
        lane_f = lane.astype(F32)
        keep = lane < 0
        for _ in range(MOBA_TOPK):
            top = jnp.max(g, axis=-1, keepdims=True)
            first = jnp.min(jnp.where(g == top, lane_f, float(LANES)), axis=-1, keepdims=True)
            hit = (lane_f == first) & (top > neg_inf)
            keep = keep | hit
            g = jnp.where(hit, neg_inf, g)
        qx_ref[...] = jnp.where(past & jnp.logical_not(keep), BLOCK_OFF, 0.0).astype(BF16)

        _attend(q_ref, k_ref, v_ref, kn2_ref, (kb_ref, kbmax_ref, h * nq, ks_ref[h]), qi, t=t,
                m_ref=m_ref, mb_ref=mb_ref, l_ref=l_ref, s_ref=s_ref, acc_ref=acc_ref, extra=(qx_ref, kx_ref))
        o_ref[0, pl.ds(q0, t), :] = acc_ref[...].astype(o_ref.dtype)
        return carry

    lax.fori_loop(0, nq, q_tile, 0)


def _moba(proj, kmean, alibi, *, q_blk, k_blk, v_blk):
    b, s, _ = proj.shape
    t = min(ATT_TILE, s)
    nq = s // t
    nblk = s // MOBA_BLOCK
    kbias, kslope = alibi
    kb, kbmax = _tile_bias(kbias, MOBA_HEADS, nq, t)
    key_block = (jnp.arange(s)[:, None] // MOBA_BLOCK == jnp.arange(LANES)[None, :]).astype(BF16)
    return pl.pallas_call(
        functools.partial(_moba_kernel, t=t, nq=nq, nblk=nblk),
        grid=(b, MOBA_HEADS),
        in_specs=[
            pl.BlockSpec((1, s, HEAD_DIM), lambda bi, h: (bi, 0, q_blk + h)),
            pl.BlockSpec((1, s, HEAD_DIM), lambda bi, h: (bi, 0, k_blk + h)),
            pl.BlockSpec((1, s, HEAD_DIM), lambda bi, h: (bi, 0, v_blk + h)),
            pl.BlockSpec((1, nblk, HEAD_DIM), lambda bi, h: (bi, 0, h)),
            pl.BlockSpec((s, LANES), lambda bi, h: (0, 0)),
            pl.BlockSpec((1, 1, nq, 1, t), lambda bi, h: (0, h, 0, 0, 0)),
            pl.BlockSpec(memory_space=pltpu.SMEM),
            pl.BlockSpec(memory_space=pltpu.SMEM),
        ],
        out_specs=pl.BlockSpec((1, s, HEAD_DIM), lambda bi, h: (bi, 0, h)),
        out_shape=jax.ShapeDtypeStruct((b, s, MOBA_HEADS * HEAD_DIM), BF16),
        scratch_shapes=_att_scratch(t, HEAD_DIM) + [pltpu.VMEM((t, LANES), BF16)],
        compiler_params=_params(("parallel", "parallel")),
        name="moba_attention",
    )(proj, proj, proj, kmean, key_block, kb, kbmax, kslope)


def _outproj_kernel(a_ref, d_ref, wa_ref, wd_ref, g_ref, h_ref, o_ref):
    y = jnp.dot(a_ref[...], wa_ref[...], preferred_element_type=F32)
    y = y + jnp.dot(d_ref[...], wd_ref[...], preferred_element_type=F32)
    o_ref[...] = h_ref[...] + _rms(y, g_ref[...])


def _outproj(a, d, wa, wd, g, h, *, tm=512):
    t, dm = h.shape
    tm = min(tm, t)
    ka, kd = a.shape[1], d.shape[1]
    return pl.pallas_call(
        _outproj_kernel,
        grid=(t // tm,),
        in_specs=[pl.BlockSpec((tm, ka), lambda i: (i, 0)),
                  pl.BlockSpec((tm, kd), lambda i: (i, 0)),
                  pl.BlockSpec((ka, dm), lambda i: (0, 0)),
                  pl.BlockSpec((kd, dm), lambda i: (0, 0)),
                  pl.BlockSpec((1, dm), lambda i: (0, 0)),
                  pl.BlockSpec((tm, dm), lambda i: (i, 0))],
        out_specs=pl.BlockSpec((tm, dm), lambda i: (i, 0)),
        out_shape=jax.ShapeDtypeStruct((t, dm), F32),
        compiler_params=_params(("parallel",), [False, False, True, True, False, False]),
        name="mixer_outproj",
    )(a, d, wa, wd, g.reshape(1, dm), h)


def _xattn_kernel(h_ref, g2_ref, wq_ref, mkv_ref, wo_ref, g3_ref, o_ref, *, scale):
    h = h_ref[0]
    xn = _rms(h, g2_ref[...]).astype(BF16)
    q = (jnp.dot(xn, wq_ref[...], preferred_element_type=F32) * scale).astype(BF16)
    w = XATTN_HEADS * HEAD_DIM
    outs = []
    for hd in range(XATTN_HEADS):
        mk = mkv_ref[0, :, hd * HEAD_DIM:(hd + 1) * HEAD_DIM]
        mv = mkv_ref[0, :, w + hd * HEAD_DIM:w + (hd + 1) * HEAD_DIM]
        s = lax.dot_general(q[:, hd * HEAD_DIM:(hd + 1) * HEAD_DIM], mk, _NT, preferred_element_type=F32)
        p = jnp.exp(s - jnp.max(s, axis=-1, keepdims=True))
        p = p / jnp.sum(p, axis=-1, keepdims=True)
        outs.append(jnp.dot(p.astype(BF16), mv, preferred_element_type=F32).astype(BF16))
    y = jnp.dot(jnp.concatenate(outs, axis=-1), wo_ref[...], preferred_element_type=F32)
    o_ref[0] = h + _rms(y, g3_ref[...])


def _xattn(h, g2, wq, mkv, wo, g3, *, tm=512):
    b, s, dm = h.shape
    tm = min(tm, s)
    fixed = lambda bi, i: (0, 0)
    return pl.pallas_call(
        functools.partial(_xattn_kernel, scale=float(HEAD_DIM ** -0.5)),
        grid=(b, s // tm),
        in_specs=[pl.BlockSpec((1, tm, dm), lambda bi, i: (bi, i, 0)),
                  pl.BlockSpec((1, dm), fixed),
                  pl.BlockSpec(wq.shape, fixed),
                  pl.BlockSpec((1,) + mkv.shape[1:], lambda bi, i: (bi, 0, 0)),
                  pl.BlockSpec(wo.shape, fixed),
                  pl.BlockSpec((1, dm), fixed)],
        out_specs=pl.BlockSpec((1, tm, dm), lambda bi, i: (bi, i, 0)),
        out_shape=jax.ShapeDtypeStruct((b, s, dm), F32),
        compiler_params=_params(("parallel", "parallel"), [False, False, True, False, True, False]),
        name="memory_xattn",
    )(h, g2.reshape(1, dm), wq, mkv, wo, g3.reshape(1, dm))


def _ffn_kernel(h_ref, g4_ref, w1_ref, w2_ref, g5_ref, o_ref, xn_ref, acc_ref):
    f = pl.program_id(1)
    last = pl.num_programs(1) - 1

    def partial_out(xn):
        u = jnp.maximum(jnp.dot(xn, w1_ref[...], preferred_element_type=F32), 0.0)
        return jnp.dot((u * u).astype(BF16), w2_ref[...], preferred_element_type=F32)

    @pl.when(f == 0)
    def _():
        xn = _rms(h_ref[...], g4_ref[...]).astype(BF16)
        xn_ref[...] = xn
        acc_ref[...] = partial_out(xn)

    @pl.when(jnp.logical_and(f > 0, f < last))
    def _():
        acc_ref[...] += partial_out(xn_ref[...])

    @pl.when(f == last)
    def _():
        y = acc_ref[...] + partial_out(xn_ref[...])
        o_ref[...] = h_ref[...] + _rms(y, g5_ref[...])


def _ffn(h, g4, w1, w2, g5, layer, *, tm=512, tf=1024):
    t, dm = h.shape
    tm = min(tm, t)
    f = w1.shape[2]
    assert t % tm == 0 and f % tf == 0 and f // tf >= 2, (t, tm, f, tf)
    return pl.pallas_call(
        _ffn_kernel,
        grid=(t // tm, f // tf),
        in_specs=[pl.BlockSpec((tm, dm), lambda i, j: (i, 0)),
                  pl.BlockSpec((1, dm), lambda i, j: (0, 0)),
                  pl.BlockSpec((None, dm, tf), lambda i, j: (layer, 0, j)),
                  pl.BlockSpec((None, tf, dm), lambda i, j: (layer, j, 0)),
                  pl.BlockSpec((1, dm), lambda i, j: (0, 0))],
        out_specs=pl.BlockSpec((tm, dm), lambda i, j: (i, 0)),
        out_shape=jax.ShapeDtypeStruct((t, dm), F32),
        scratch_shapes=[pltpu.VMEM((tm, dm), BF16), pltpu.VMEM((tm, dm), F32)],
        compiler_params=_params(("parallel", "arbitrary")),
        name="relu2_mlp",
    )(h, g4.reshape(1, dm), w1, w2, g5.reshape(1, dm))


def _alibi_key_bias(n_heads, seq):
    slopes = LOG2E * jnp.asarray([2.0 ** (-8.0 * (i + 1) / n_heads) for i in range(n_heads)], dtype=F32)
    return (slopes[:, None] * jnp.arange(seq, dtype=F32)[None, :])[None], slopes


def _rope_tables(seq):
    half = MLA_ROPE // 2
    inv = ROPE_THETA ** (-jnp.arange(0, MLA_ROPE, 2, dtype=F32) / MLA_ROPE)
    ang = jnp.arange(seq, dtype=F32)[:, None] * inv[None, :]
    cos, sin = jnp.cos(ang), jnp.sin(ang)
    zero = jnp.zeros((seq, LANES - 2 * half), F32)
    return jnp.concatenate([cos, cos, zero], axis=-1), jnp.concatenate([-sin, sin, zero], axis=-1)


def _rope_pair_columns(w_t1, w_t2):
    zero = jnp.zeros((w_t1.shape[0], LANES - 2 * w_t1.shape[1]), w_t1.dtype)
    return jnp.concatenate([w_t1, w_t2, zero, w_t2, w_t1, zero], axis=-1)


def kernel(x, mem, mem_norm, mem_wkv, norms, xattn_wq, xattn_wo, ffn_w1, ffn_w2, ab_w_in, ab_w_out, fox_b_f, diff_lambda, diff_subln, cd_w_in, cd_w_out, mla_q_norm, mla_kv_norm, mla_w_uq, mla_w_ukv):
    b, s, dm = x.shape
    t = b * s
    depth = norms.shape[0]
    w1_all, w2_all = ffn_w1.astype(BF16), ffn_w2.astype(BF16)
    q_scale = HEAD_DIM ** -0.5 * LOG2E
    half = MLA_ROPE // 2
    cos_t, sin_t = _rope_tables(s)

    mkv = _norm_matmul(mem.reshape(-1, dm), mem_norm, mem_wkv.astype(BF16)).reshape(b, mem.shape[1], -1)

    h = x.reshape(t, dm)
    for i in range(depth):
        n = norms[i]
        j = i // 2
        if i % 2 == 0:
            w = ab_w_in[j]
            fw = FOX_HEADS * HEAD_DIM
            dw = DIFF_HEADS * 2 * HEAD_DIM
            o_g = 3 * fw
            o_dq = o_g + FOX_HEADS
            w_main = jnp.concatenate([w[:, :fw] * q_scale, w[:, fw:o_g],
                                      w[:, o_dq:o_dq + dw] * q_scale, w[:, o_dq + dw:]], axis=-1).astype(BF16)
            w_gate = jnp.concatenate([w[:, o_g:o_dq].T, jnp.zeros((8, dm), F32)], axis=0).astype(BF16)
            proj, gate_t = _inproj(h, n[0], w_main, w_gate, "gate_t")
            proj = proj.reshape(b, s, -1)
            fox_kb = _fox_bias(gate_t, fox_b_f[j], b, s)
            a = _flash(proj, proj, proj, fox_kb, heads=FOX_HEADS, dk=HEAD_DIM, dv=HEAD_DIM,
                       q_blk=0, k_blk=FOX_HEADS, v_blk=2 * FOX_HEADS, name="fox_attention")
            lam_init = 0.8 - 0.6 * math.exp(-0.3 * i)
            d = _diff_attention(proj, _alibi_key_bias(DIFF_HEADS, s), diff_lambda[j], diff_subln[j],
                                q_blk=3 * FOX_HEADS, k_blk=3 * FOX_HEADS + 2 * DIFF_HEADS,
                                v_blk=(3 * fw + 2 * dw) // (2 * HEAD_DIM), lam_init=lam_init)
            wo = ab_w_out[j].astype(BF16)
            h = _outproj(a.reshape(t, -1), d.reshape(t, -1), wo[:fw], wo[fw:], n[1], h)
        else:
            w = cd_w_in[j]
            r2 = 2 * MLA_RANK
            o_m = r2 + MLA_ROPE
            mw = MOBA_HEADS * HEAD_DIM
            w_main = jnp.concatenate([w[:, :r2], w[:, o_m:o_m + mw] * q_scale, w[:, o_m + mw:]],
                                     axis=-1).astype(BF16)
            w_kpe = _rope_pair_columns(w[:, r2:r2 + half], w[:, r2 + half:o_m]).astype(BF16)
            proj, kpe_r = _inproj(h, n[0], w_main, w_kpe, "rope", (cos_t, sin_t))
            wq = mla_w_uq[j].reshape(MLA_RANK, MLA_HEADS, MLA_NOPE + MLA_ROPE)
            wq = jnp.concatenate(
                [jnp.concatenate([wq[:, hd, :MLA_NOPE],
                                  _rope_pair_columns(wq[:, hd, MLA_NOPE:MLA_NOPE + half], wq[:, hd, MLA_NOPE + half:])],
                                 axis=-1) for hd in range(MLA_HEADS)], axis=-1).astype(BF16)
            wkv = mla_w_ukv[j].reshape(MLA_RANK, MLA_HEADS, MLA_NOPE + MLA_V)
            wkv = jnp.concatenate([wkv[:, :, :MLA_NOPE].reshape(MLA_RANK, -1),
                                   wkv[:, :, MLA_NOPE:].reshape(MLA_RANK, -1)], axis=-1).astype(BF16)
            qf, kf, v = _mla_up(proj, kpe_r, mla_q_norm[j], mla_kv_norm[j], wq, wkv, cos_t, sin_t)
            c = _flash(qf.reshape(b, s, -1), kf.reshape(b, s, -1), v.reshape(b, s, -1), None,
                       heads=MLA_HEADS, dk=2 * LANES, dv=MLA_V, q_blk=0, k_blk=0, v_blk=0, name="mla_attention")
            proj = proj.reshape(b, s, -1)
            kmean = _kmean(proj, k_blk_wide=(r2 + mw) // mw)
            dout = _moba(proj, kmean, _alibi_key_bias(MOBA_HEADS, s),
                         q_blk=r2 // HEAD_DIM, k_blk=(r2 + mw) // HEAD_DIM, v_blk=(r2 + 2 * mw) // HEAD_DIM)
            wo = cd_w_out[j].astype(BF16)
            cw = MLA_HEADS * MLA_V
            h = _outproj(c.reshape(t, -1), dout.reshape(t, -1), wo[:cw], wo[cw:], n[1], h)
        h = _xattn(h.reshape(b, s, dm), n[2], xattn_wq[i].astype(BF16), mkv, xattn_wo[i].astype(BF16), n[3])
        h = _ffn(h.reshape(t, dm), n[4], w1_all, w2_all, n[5], i)
    return h.reshape(b, s, dm)
```
